```python
import math
import jax, jax.numpy as jnp
from jax import lax
import numpy as np

D_MODEL = 1024
BATCH = 8
SEQ = 2048
DEPTH = 2
DEC_BATCH = 16
DEC_SEQ = 32
PAST_LEN = 4096

CHUNK = 64
Q_BLOCK = 128
MLA_HEADS = 8
MLA_NOPE = 64
MLA_ROPE = 32
MLA_V = 64
MLA_Q_RANK = 256
MLA_KV_RANK = 128
ROPE_BASE = 10000.0
SB_HEADS = 8
SB_DIM = 64
DIFF_HEADS = 4
DIFF_DIM = 64
BRANCH_W = 512
N_BRANCH = 3
D_FF = 2816
CONV_W = 3
EPS = 1e-6

MLA_SCALE = (MLA_NOPE + MLA_ROPE) ** -0.5
SB_SCALE = SB_DIM ** -0.5
DIFF_SCALE = DIFF_DIM ** -0.5
SPLITS = (MLA_Q_RANK, MLA_KV_RANK, MLA_ROPE,
          SB_HEADS * SB_DIM, SB_HEADS * SB_DIM, SB_HEADS * SB_DIM,
          2 * DIFF_HEADS * DIFF_DIM, 2 * DIFF_HEADS * DIFF_DIM, 2 * DIFF_HEADS * DIFF_DIM,
          N_BRANCH * D_MODEL)
N_IN = sum(SPLITS)
SPLIT_IDX = tuple(int(i) for i in np.cumsum(SPLITS)[:-1])

kernel_name = 'hybrid_mla_stickbreak_diffattn_convffn_stream_step'


def rmsnorm(x, g):
    xf = x.astype(jnp.float32)
    y = xf * lax.rsqrt(jnp.mean(xf * xf, axis=-1, keepdims=True) + EPS)
    return (y * g.astype(jnp.float32)).astype(x.dtype)


def rope(x, pos):
    half = x.shape[-1] // 2
    inv = ROPE_BASE ** (-jnp.arange(half, dtype=jnp.float32) / half)
    ang = pos.astype(jnp.float32)[:, None] * inv[None, :]
    cos = jnp.cos(ang)[:, None, :]
    sin = jnp.sin(ang)[:, None, :]
    xf = x.astype(jnp.float32)
    x1, x2 = xf[..., :half], xf[..., half:]
    return jnp.concatenate([x1 * cos - x2 * sin, x1 * sin + x2 * cos], axis=-1).astype(x.dtype)


def chunk_mask(q_pos, k_pos):
    return (k_pos[None, :] // CHUNK) <= (q_pos[:, None] // CHUNK)


def alibi_slopes():
    return 2.0 ** (-8.0 * jnp.arange(1, DIFF_HEADS + 1, dtype=jnp.float32) / DIFF_HEADS)


def sweep(fn, q_args, q_pos):
    S = q_pos.shape[0]
    if S <= Q_BLOCK or S % Q_BLOCK != 0:
        return fn(*q_args, q_pos)
    nb = S // Q_BLOCK
    blocks = tuple(jnp.moveaxis(a.reshape((a.shape[0], nb, Q_BLOCK) + a.shape[2:]), 1, 0) for a in q_args)
    pos_blocks = q_pos.reshape(nb, Q_BLOCK)
    out = lax.map(lambda t: fn(*t[0], t[1]), (blocks, pos_blocks))
    o = jnp.moveaxis(out, 0, 1)
    return o.reshape((o.shape[0], nb * Q_BLOCK) + o.shape[3:])


def mla_block(qn, qr, q_pos, kn, kr, v, k_pos):
    s = (jnp.einsum('bqhd,bkhd->bhqk', qn, kn) + jnp.einsum('bqhr,bkr->bhqk', qr, kr)).astype(jnp.float32) * MLA_SCALE
    s = jnp.where(chunk_mask(q_pos, k_pos), s, -jnp.inf)
    p = jax.nn.softmax(s, axis=-1).astype(v.dtype)
    return jnp.einsum('bhqk,bkhd->bqhd', p, v)


def sb_block(q, q_pos, k, v, k_pos):
    z = jnp.einsum('bqhd,bkhd->bhqk', q, k).astype(jnp.float32) * SB_SCALE
    valid = k_pos[None, :] < q_pos[:, None]
    log_1mb = jnp.where(valid, jax.nn.log_sigmoid(-z), 0.0)
    suffix = lax.cumsum(log_1mb, axis=3, reverse=True) - log_1mb
    a = jnp.where(valid, jnp.exp(jax.nn.log_sigmoid(z) + suffix), 0.0).astype(v.dtype)
    return jnp.einsum('bhqk,bkhd->bqhd', a, v)


def diff_block(q, q_pos, k, v, k_pos, lam):
    s = jnp.einsum('bqhmd,bkhmd->bhmqk', q, k).astype(jnp.float32) * DIFF_SCALE
    dist = jnp.abs(q_pos[:, None] - k_pos[None, :]).astype(jnp.float32)
    s = s - alibi_slopes()[:, None, None, None] * dist
    s = jnp.where(chunk_mask(q_pos, k_pos), s, -jnp.inf)
    p = jax.nn.softmax(s, axis=-1)
    pd = (p[:, :, 0] - lam * p[:, :, 1]).astype(v.dtype)
    return jnp.einsum('bhqk,bkhe->bqhe', pd, v)


def mixer_layer(x, q_pos, past, w, layer_idx):
    B, S, _ = x.shape
    h = rmsnorm(x, w['mix_norm_g'])
    z = h @ w['w_in']
    z_cq, z_ckv, z_kr, z_sq, z_sk, z_sv, z_dq, z_dk, z_dv, z_g = jnp.split(z, SPLIT_IDX, axis=-1)
    cq = rmsnorm(z_cq, w['mla_q_norm_g'])
    q = (cq @ w['mla_w_uq']).reshape(B, S, MLA_HEADS, MLA_NOPE + MLA_ROPE)
    qn = rmsnorm(q[..., :MLA_NOPE], w['mla_qn_g'])
    qr = rope(rmsnorm(q[..., MLA_NOPE:], w['mla_qr_g']), q_pos)
    ckv_new = rmsnorm(z_ckv, w['mla_kv_norm_g'])
    kr_new = rope(rmsnorm(z_kr, w['mla_kr_g'])[:, :, None, :], q_pos)[:, :, 0, :]
    sbq = z_sq.reshape(B, S, SB_HEADS, SB_DIM)
    sbk_new = z_sk.reshape(B, S, SB_HEADS, SB_DIM)
    sbv_new = z_sv.reshape(B, S, SB_HEADS, SB_DIM)
    dq = rmsnorm(z_dq.reshape(B, S, DIFF_HEADS, 2, DIFF_DIM), w['diff_qn_g'])
    dk_new = rmsnorm(z_dk.reshape(B, S, DIFF_HEADS, 2, DIFF_DIM), w['diff_kn_g'])
    dv_new = z_dv.reshape(B, S, DIFF_HEADS, 2 * DIFF_DIM)
    new_rows = (ckv_new, kr_new, sbk_new, sbv_new, dk_new, dv_new)
    if past is None:
        full = new_rows
    else:
        full = tuple(jnp.concatenate([c, n], axis=1) for c, n in zip(past, new_rows))
    ckv, kr, sbk, sbv, dk, dv = full
    K = ckv.shape[1]
    k_pos = jnp.arange(K, dtype=jnp.int32)
    kn = rmsnorm((ckv @ w['mla_w_uk']).reshape(B, K, MLA_HEADS, MLA_NOPE), w['mla_kn_g'])
    mv = (ckv @ w['mla_w_uv']).reshape(B, K, MLA_HEADS, MLA_V)
    o_mla = sweep(lambda a, b, qp: mla_block(a, b, qp, kn, kr, mv, k_pos), (qn, qr), q_pos)
    o_sb = sweep(lambda a, qp: sb_block(a, qp, sbk, sbv, k_pos), (sbq,), q_pos)
    lam_init = 0.8 - 0.6 * math.exp(-0.3 * layer_idx)
    lv = w['diff_lambda'].astype(jnp.float32)
    lam = jnp.exp(jnp.sum(lv[0] * lv[1])) - jnp.exp(jnp.sum(lv[2] * lv[3])) + lam_init
    o_diff = sweep(lambda a, qp: diff_block(a, qp, dk, dv, k_pos, lam), (dq,), q_pos)
    o_diff = rmsnorm(o_diff, w['diff_subln_g']) * (1.0 - lam_init)
    g_mla, g_sb, g_diff = jnp.split(jax.nn.sigmoid(z_g), N_BRANCH, axis=-1)
    merged = (g_mla * (o_mla.reshape(B, S, BRANCH_W) @ w['w_br_mla'])
              + g_sb * (o_sb.reshape(B, S, BRANCH_W) @ w['w_br_sb'])
              + g_diff * (o_diff.reshape(B, S, BRANCH_W) @ w['w_br_diff']))
    return merged @ w['w_out'], new_rows


def ffn_layer(x, conv_past, w):
    h = rmsnorm(x, w['ffn_norm_g'])
    a, u = jnp.split(h @ w['ffn_w_up'], 2, axis=-1)
    B, S, _ = a.shape
    pad = jnp.zeros((B, CONV_W - 1, D_FF), a.dtype) if conv_past is None else conv_past
    ap = jnp.concatenate([pad, a], axis=1)
    cw = w['ffn_conv_w']
    c = ap[:, 0:S] * cw[0]
    for k in range(1, CONV_W):
        c = c + ap[:, k:k + S] * cw[k]
    c = c + w['ffn_conv_b']
    out = (jax.nn.silu(c) * u) @ w['ffn_w_down']
    return out, ap[:, -(CONV_W - 1):]


def setup_inputs(seed: int = 0) -> dict:
    key = jax.random.key(seed)
    ks = iter(jax.random.split(key, 48))

    def nrm(shape, scale=1.0):
        return jax.random.normal(next(ks), shape, jnp.float32) * scale

    def gain(shape):
        return 1.0 + 0.02 * nrm(shape)

    L = DEPTH
    return {
        'x_prompt': nrm((BATCH, SEQ, D_MODEL)),
        'x_sample': nrm((DEC_BATCH, DEC_SEQ, D_MODEL)),
        'cache_mla_ckv': nrm((L, DEC_BATCH, PAST_LEN, MLA_KV_RANK)),
        'cache_mla_krope': nrm((L, DEC_BATCH, PAST_LEN, MLA_ROPE)),
        'cache_sb_k': nrm((L, DEC_BATCH, PAST_LEN, SB_HEADS, SB_DIM)),
        'cache_sb_v': nrm((L, DEC_BATCH, PAST_LEN, SB_HEADS, SB_DIM)),
        'cache_diff_k': nrm((L, DEC_BATCH, PAST_LEN, DIFF_HEADS, 2, DIFF_DIM)),
        'cache_diff_v': nrm((L, DEC_BATCH, PAST_LEN, DIFF_HEADS, 2 * DIFF_DIM)),
        'state_ffn_conv': nrm((L, DEC_BATCH, CONV_W - 1, D_FF)),
        'mix_norm_g': gain((L, D_MODEL)),
        'w_in': nrm((L, D_MODEL, N_IN), D_MODEL ** -0.5),
        'mla_q_norm_g': gain((L, MLA_Q_RANK)),
        'mla_w_uq': nrm((L, MLA_Q_RANK, MLA_HEADS * (MLA_NOPE + MLA_ROPE)), MLA_Q_RANK ** -0.5),
        'mla_kv_norm_g': gain((L, MLA_KV_RANK)),
        'mla_w_uk': nrm((L, MLA_KV_RANK, MLA_HEADS * MLA_NOPE), MLA_KV_RANK ** -0.5),
        'mla_w_uv': nrm((L, MLA_KV_RANK, MLA_HEADS * MLA_V), MLA_KV_RANK ** -0.5),
        'mla_qn_g': gain((L, MLA_NOPE)),
        'mla_kn_g': gain((L, MLA_NOPE)),
        'mla_qr_g': gain((L, MLA_ROPE)),
        'mla_kr_g': gain((L, MLA_ROPE)),
        'diff_qn_g': gain((L, DIFF_DIM)),
        'diff_kn_g': gain((L, DIFF_DIM)),
        'diff_lambda': nrm((L, 4, DIFF_DIM), 0.1),
        'diff_subln_g': gain((L, 2 * DIFF_DIM)),
        'w_br_mla': nrm((L, BRANCH_W, D_MODEL), BRANCH_W ** -0.5),
        'w_br_sb': nrm((L, BRANCH_W, D_MODEL), BRANCH_W ** -0.5),
        'w_br_diff': nrm((L, BRANCH_W, D_MODEL), BRANCH_W ** -0.5),
        'w_out': nrm((L, D_MODEL, D_MODEL), D_MODEL ** -0.5),
        'ffn_norm_g': gain((L, D_MODEL)),
        'ffn_w_up': nrm((L, D_MODEL, 2 * D_FF), D_MODEL ** -0.5),
        'ffn_conv_w': nrm((L, CONV_W, D_FF), CONV_W ** -0.5),
        'ffn_conv_b': nrm((L, D_FF), 0.01),
        'ffn_w_down': nrm((L, D_FF, D_MODEL), D_FF ** -0.5),
    }


def reference(x_prompt, x_sample, cache_mla_ckv, cache_mla_krope, cache_sb_k, cache_sb_v,
              cache_diff_k, cache_diff_v, state_ffn_conv,
              mix_norm_g, w_in, mla_q_norm_g, mla_w_uq, mla_kv_norm_g, mla_w_uk, mla_w_uv,
              mla_qn_g, mla_kn_g, mla_qr_g, mla_kr_g, diff_qn_g, diff_kn_g, diff_lambda,
              diff_subln_g, w_br_mla, w_br_sb, w_br_diff, w_out, ffn_norm_g, ffn_w_up,
              ffn_conv_w, ffn_conv_b, ffn_w_down):
    weights = dict(mix_norm_g=mix_norm_g, w_in=w_in, mla_q_norm_g=mla_q_norm_g, mla_w_uq=mla_w_uq,
                   mla_kv_norm_g=mla_kv_norm_g, mla_w_uk=mla_w_uk, mla_w_uv=mla_w_uv,
                   mla_qn_g=mla_qn_g, mla_kn_g=mla_kn_g, mla_qr_g=mla_qr_g, mla_kr_g=mla_kr_g,
                   diff_qn_g=diff_qn_g, diff_kn_g=diff_kn_g, diff_lambda=diff_lambda,
                   diff_subln_g=diff_subln_g, w_br_mla=w_br_mla, w_br_sb=w_br_sb,
                   w_br_diff=w_br_diff, w_out=w_out, ffn_norm_g=ffn_norm_g, ffn_w_up=ffn_w_up,
                   ffn_conv_w=ffn_conv_w, ffn_conv_b=ffn_conv_b, ffn_w_down=ffn_w_down)
    past_len = cache_mla_ckv.shape[2]
    pos_p = jnp.arange(x_prompt.shape[1], dtype=jnp.int32)
    pos_s = past_len + jnp.arange(x_sample.shape[1], dtype=jnp.int32)
    xp, xs = x_prompt, x_sample
    rows_p, rows_s, conv_p, conv_s = [], [], [], []
    for l in range(DEPTH):
        w = {name: arr[l] for name, arr in weights.items()}
        yp, rp = mixer_layer(xp, pos_p, None, w, l)
        xp = xp + yp
        fp, cp = ffn_layer(xp, None, w)
        xp = xp + fp
        past = (cache_mla_ckv[l], cache_mla_krope[l], cache_sb_k[l], cache_sb_v[l],
                cache_diff_k[l], cache_diff_v[l])
        ys, rs = mixer_layer(xs, pos_s, past, w, l)
        xs = xs + ys
        fs, cs = ffn_layer(xs, state_ffn_conv[l], w)
        xs = xs + fs
        rows_p.append(rp)
        rows_s.append(rs)
        conv_p.append(cp)
        conv_s.append(cs)
    p_ckv = jnp.stack([r[0] for r in rows_p])
    p_krope = jnp.stack([r[1] for r in rows_p])
    p_sbk = jnp.stack([r[2] for r in rows_p])
    p_sbv = jnp.stack([r[3] for r in rows_p])
    p_dk = jnp.stack([r[4] for r in rows_p])
    p_dv = jnp.stack([r[5] for r in rows_p])
    p_conv = jnp.stack(conv_p)
    s_ckv = jnp.stack([r[0] for r in rows_s])
    s_krope = jnp.stack([r[1] for r in rows_s])
    s_sbk = jnp.stack([r[2] for r in rows_s])
    s_sbv = jnp.stack([r[3] for r in rows_s])
    s_dk = jnp.stack([r[4] for r in rows_s])
    s_dv = jnp.stack([r[5] for r in rows_s])
    s_conv = jnp.stack(conv_s)
    return (xp, xs, p_ckv, p_krope, p_sbk, p_sbv, p_dk, p_dv, p_conv,
            s_ckv, s_krope, s_sbk, s_sbv, s_dk, s_dv, s_conv)
```

```python
import functools
import math

import numpy as np
import jax
import jax.numpy as jnp
from jax import lax
from jax.experimental import pallas as pl
from jax.experimental.pallas import tpu as pltpu

F32 = jnp.float32
BF16 = jnp.bfloat16

EPS = 1e-6
CHUNK = 64
CHUNK_SHIFT = CHUNK.bit_length() - 1
assert 1 << CHUNK_SHIFT == CHUNK
ROPE_BASE = 10000.0
NEG_BIG = -1e30

LANES = 128
ROW_TILE = 512
KV_ROW_TILE = 1024
ATTN_TILE = 256
VMEM_LIMIT = 52 * 1024 * 1024


def _cparams(n_axes):
    return pltpu.CompilerParams(dimension_semantics=("arbitrary",) * n_axes,
                                vmem_limit_bytes=VMEM_LIMIT)


def _const_spec(shape):
    nd = len(shape)
    return pl.BlockSpec(shape, lambda *_: (0,) * nd, pipeline_mode=pl.Buffered(1))


def _dot(a, b):
    return jnp.dot(a, b, preferred_element_type=F32)


def _dot_nt(a, b):
    return lax.dot_general(a, b, (((1,), (1,)), ((), ())), preferred_element_type=F32)


def _rmsnorm(x, g):
    return x * lax.rsqrt(jnp.mean(x * x, axis=-1, keepdims=True) + EPS) * g


def _group_mean_sq(x, gmat):
    sq = x * x
    hi = sq.astype(BF16)
    lo = (sq - hi.astype(F32)).astype(BF16)
    return _dot(hi, gmat) + _dot(lo, gmat)


def _sigmoid(x):
    return 1.0 / (1.0 + jnp.exp(-x))


def _in_proj_kernel(x_ref, gmix_ref, w_ref, gcq_ref, wuq_ref, gm_mla_ref, gq_ref, gckv_ref,
                    gkr_ref, tab_ref, gm64_ref, gdq_ref, gdk_ref,
                    q_ref, ckv_ref, kr_ref, sq_ref, sk_ref, sv_ref, dq_ref, dk_ref, dv_ref,
                    *, q_rank, kv_rank, rope, sb_w, df_w):
    tm = x_ref.shape[0]
    half = rope // 2
    h = _rmsnorm(x_ref[...], gmix_ref[...]).astype(BF16)

    za = _dot(h, w_ref[:, 0:4 * LANES])
    cqn = _rmsnorm(za[:, 0:q_rank], gcq_ref[...]).astype(BF16)
    q = _dot(cqn, wuq_ref[...])
    tab = tab_ref[...]
    cos_q, sin_q = tab[:, 0:LANES], tab[:, LANES:2 * LANES]
    cos_k, sin_k = tab[:, 2 * LANES:3 * LANES], tab[:, 3 * LANES:4 * LANES]
    lane = lax.broadcasted_iota(jnp.int32, (tm, LANES), 1)
    n_pairs = q.shape[1] // (2 * LANES)
    for p in range(n_pairs):
        lo = 2 * LANES * p
        blk = q[:, lo:lo + 2 * LANES]
        y = blk * lax.rsqrt(_group_mean_sq(blk, gm_mla_ref[...]) + EPS) * gq_ref[:, lo:lo + 2 * LANES]
        rp = y[:, LANES:]
        partner = jnp.where(lane < 2 * half, pltpu.roll(rp, LANES - 2 * half, 1),
                            pltpu.roll(rp, 2 * half, 1))
        q_ref[:, lo:lo + LANES] = y[:, :LANES].astype(BF16)
        q_ref[:, lo + LANES:lo + 2 * LANES] = (rp * cos_q + partner * sin_q).astype(BF16)

    ckv_ref[...] = _rmsnorm(za[:, q_rank:q_rank + kv_rank], gckv_ref[...])

    kr = za[:, q_rank + kv_rank:4 * LANES]
    ms = jnp.sum(kr * kr, axis=-1, keepdims=True) * (1.0 / rope)
    krn = kr * lax.rsqrt(ms + EPS) * gkr_ref[...]
    partner = jnp.where(lane < half, pltpu.roll(krn, LANES - half, 1), pltpu.roll(krn, half, 1))
    kr_ref[...] = (krn * cos_k + partner * sin_k)[:, :rope]

    o = 4 * LANES
    zs = _dot(h, w_ref[:, o:o + 3 * sb_w])
    sq_ref[...] = (zs[:, 0:sb_w] * (1.0 / 8.0)).astype(BF16)
    sk_ref[...] = zs[:, sb_w:2 * sb_w]
    sv_ref[...] = zs[:, 2 * sb_w:3 * sb_w]

    o = o + 3 * sb_w
    zd = _dot(h, w_ref[:, o:o + 3 * df_w])
    for s in range(df_w // (2 * LANES)):
        lo = 2 * LANES * s
        blk = zd[:, lo:lo + 2 * LANES]
        r = lax.rsqrt(_group_mean_sq(blk, gm64_ref[...]) + EPS)
        dq_ref[:, lo:lo + 2 * LANES] = (blk * r * gdq_ref[:, lo:lo + 2 * LANES]).astype(BF16)
        blk = zd[:, df_w + lo:df_w + lo + 2 * LANES]
        r = lax.rsqrt(_group_mean_sq(blk, gm64_ref[...]) + EPS)
        dk_ref[:, lo:lo + 2 * LANES] = blk * r * gdk_ref[:, lo:lo + 2 * LANES]
    dv_ref[...] = zd[:, 2 * df_w:3 * df_w]


def _in_proj(x2d, lw, tab, dims):
    t, d = x2d.shape
    tm = ROW_TILE
    assert t % tm == 0 and tab.shape[0] % tm == 0
    n_tab = tab.shape[0] // tm
    row = lambda w: pl.BlockSpec((tm, w), lambda i: (i, 0))
    consts = [lw["gmix"], lw["w1"], lw["gcq"], lw["wuq"], lw["gm_mla"], lw["gq"], lw["gckv"], lw["gkr"]]
    consts2 = [lw["gm64"], lw["gdq"], lw["gdk"]]
    sb_w, df_w = dims["sb_w"], dims["df_w"]
    out_shape = [
        jax.ShapeDtypeStruct((t, lw["wuq"].shape[1]), BF16),
        jax.ShapeDtypeStruct((t, dims["kv_rank"]), F32),
        jax.ShapeDtypeStruct((t, dims["rope"]), F32),
        jax.ShapeDtypeStruct((t, sb_w), BF16),
        jax.ShapeDtypeStruct((t, sb_w), F32),
        jax.ShapeDtypeStruct((t, sb_w), F32),
        jax.ShapeDtypeStruct((t, df_w), BF16),
        jax.ShapeDtypeStruct((t, df_w), F32),
        jax.ShapeDtypeStruct((t, df_w), F32),
    ]
    kern = functools.partial(_in_proj_kernel, q_rank=dims["q_rank"], kv_rank=dims["kv_rank"],
                             rope=dims["rope"], sb_w=sb_w, df_w=df_w)
    return pl.pallas_call(
        kern,
        grid=(t // tm,),
        in_specs=([row(d)] + [_const_spec(c.shape) for c in consts]
                  + [pl.BlockSpec((tm, tab.shape[1]), lambda i: (i % n_tab, 0))]
                  + [_const_spec(c.shape) for c in consts2]),
        out_specs=[row(s.shape[1]) for s in out_shape],
        out_shape=out_shape,
        compiler_params=_cparams(1),
        name="in_proj",
    )(x2d, *consts, tab, *consts2)


def _kv_up_kernel(ckv_ref, kr_ref, wukv_ref, gm64_ref, gkn_ref, rep_ref, k_ref, v_ref, *, kn_w):
    c = ckv_ref[...].astype(BF16)
    kv = _dot(c, wukv_ref[...])
    krrep = _dot(kr_ref[...].astype(BF16), rep_ref[...]).astype(BF16)
    for b in range(kn_w // (2 * LANES)):
        s = 2 * LANES * b
        blk = kv[:, s:s + 2 * LANES]
        r = lax.rsqrt(_group_mean_sq(blk, gm64_ref[...]) + EPS)
        kn = (blk * r * gkn_ref[:, s:s + 2 * LANES]).astype(BF16)
        for j in range(2):
            p = 2 * b + j
            k_ref[:, 2 * LANES * p:2 * LANES * p + LANES] = kn[:, LANES * j:LANES * (j + 1)]
            k_ref[:, 2 * LANES * p + LANES:2 * LANES * (p + 1)] = krrep
    v_ref[...] = kv[:, kn_w:].astype(BF16)


def _kv_up(ckv2d, kr2d, lw):
    r, kvr = ckv2d.shape
    tr = min(KV_ROW_TILE, r)
    assert r % tr == 0
    kn_w = lw["gkn"].shape[1]
    consts = [lw["wukv"], lw["gm64"], lw["gkn"], lw["rep"]]
    out_shape = [jax.ShapeDtypeStruct((r, 2 * kn_w), BF16),
                 jax.ShapeDtypeStruct((r, lw["wukv"].shape[1] - kn_w), BF16)]
    return pl.pallas_call(
        functools.partial(_kv_up_kernel, kn_w=kn_w),
        grid=(r // tr,),
        in_specs=[pl.BlockSpec((tr, kvr), lambda i: (i, 0)),
                  pl.BlockSpec((tr, kr2d.shape[1]), lambda i: (i, 0))]
                 + [_const_spec(c.shape) for c in consts],
        out_specs=[pl.BlockSpec((tr, s.shape[1]), lambda i: (i, 0)) for s in out_shape],
        out_shape=out_shape,
        compiler_params=_cparams(1),
        name="mla_kv_up",
    )(ckv2d, kr2d, *consts)


def _stream_masks(kind, width):
    lane = lax.broadcasted_iota(jnp.int32, (1, width), 1)
    if kind == "mla":
        r = lane - 2 * 64
        first = (lane < 64) | ((r >= 0) & (r < 16)) | ((r >= 32) & (r < 48))
        second = ((lane >= 64) & (lane < 128)) | ((r >= 16) & (r < 32)) | ((r >= 48) & (r < 64))
        return first, second
    return lane < 64, lane >= 64


def _attn_kernel(*refs, kind, srcs, tq, q_pos0, lam_init):
    it = iter(refs)
    slopes_ref = next(it) if kind == "diff" else None
    q_ref = next(it)
    kv_refs = [(next(it), next(it)) for _ in srcs]
    if kind == "diff":
        lam_ref, gsub_ref = next(it), next(it)
    o_ref = next(it)
    scr = [(next(it), next(it)) if s["cast"] else None for s in srcs]

    qi = pl.program_id(2)
    head = pl.program_id(1)

    for s, sc, (k_ref, v_ref) in zip(srcs, scr, kv_refs):
        if sc is not None:
            @pl.when(qi == 0)
            def _():
                sc[0][...] = k_ref[0].astype(BF16)
                sc[1][...] = v_ref[0].astype(BF16)

    q = q_ref[0]
    masks = _stream_masks(kind, q.shape[1])
    q_streams = [jnp.where(m, q, jnp.zeros_like(q)) for m in masks]
    q_start = q_pos0 + qi * tq
    qpos = q_start + lax.broadcasted_iota(jnp.int32, (tq, 1), 0)

    if kind == "diff":
        slope = slopes_ref[head]
    if kind == "sb":
        init = tuple((jnp.zeros((tq, 1), F32), jnp.zeros((tq, LANES), F32)) for _ in range(2))
    else:
        init = tuple((jnp.full((tq, 1), NEG_BIG, F32), jnp.zeros((tq, 1), F32),
                      jnp.zeros((tq, LANES), F32)) for _ in range(2))

    def tile_step(states, k_t, v_t, k_start, tk, masked, upper):
        kpos = k_start + lax.broadcasted_iota(jnp.int32, (1, tk), 1)
        if masked:
            if kind == "sb":
                valid = kpos < qpos
            else:
                valid = (kpos >> CHUNK_SHIFT) <= (qpos >> CHUNK_SHIFT)
        if kind == "diff":
            if masked:
                bias = slope * jnp.abs(qpos - kpos).astype(F32)
            else:
                bias = slope * (qpos - kpos).astype(F32)
        out = []
        for qs, st in zip(q_streams, states):
            s = _dot_nt(qs, k_t)
            if kind == "sb":
                c, acc = st
                lp = jnp.log1p(jnp.exp(-jnp.abs(s)))
                ls = jnp.minimum(s, 0.0) - lp
                l1 = ls - s
                if masked:
                    l1 = jnp.where(valid, l1, 0.0)
                hi = l1.astype(BF16)
                lo = (l1 - hi.astype(F32)).astype(BF16)
                suffix = _dot(hi, upper) + _dot(lo, upper)
                a = jnp.exp(ls + suffix + c)
                if masked:
                    a = jnp.where(valid, a, 0.0)
                acc = acc + _dot(a.astype(BF16), v_t)
                c = c + jnp.sum(l1, axis=-1, keepdims=True)
                out.append((c, acc))
            else:
                m, l, acc = st
                if kind == "diff":
                    s = s - bias
                if masked:
                    s = jnp.where(valid, s, -jnp.inf)
                m_new = jnp.maximum(m, jnp.max(s, axis=-1, keepdims=True))
                alpha = jnp.exp(m - m_new)
                p = jnp.exp(s - m_new)
                l = alpha * l + jnp.sum(p, axis=-1, keepdims=True)
                acc = alpha * acc + _dot(p.astype(BF16), v_t)
                out.append((m_new, l, acc))
        return tuple(out)

    def upper_tri(tk):
        if kind != "sb":
            return None
        r = lax.broadcasted_iota(jnp.int32, (tk, tk), 0)
        c = lax.broadcasted_iota(jnp.int32, (tk, tk), 1)
        return jnp.where(r > c, 1.0, 0.0).astype(BF16)

    states = init
    for s, sc, (k_ref, v_ref) in reversed(list(zip(srcs, scr, kv_refs))):
        tk = s["tk"]
        upper = upper_tri(tk)

        def load(k_off, _sc=sc, _k=k_ref, _v=v_ref, _tk=tk):
            if _sc is not None:
                return _sc[0][pl.ds(k_off, _tk), :], _sc[1][pl.ds(k_off, _tk), :]
            return _k[0, pl.ds(k_off, _tk), :], _v[0, pl.ds(k_off, _tk), :]

        if s["mode"] == "self":
            k_t, v_t = load(pl.multiple_of(qi * tk, tk))
            states = tile_step(states, k_t, v_t, s["pos0"] + qi * tk, tk, True, upper)
            n_full = qi
        elif s["mode"] == "masked":
            k_t, v_t = load(0)
            states = tile_step(states, k_t, v_t, s["pos0"], tk, True, upper)
            n_full = 0
        else:
            n_full = s["sk"] // tk

        if s["mode"] != "masked":
            def body(i, st, _tk=tk, _load=load, _pos0=s["pos0"], _n=n_full, _upper=upper):
                kj = _n - 1 - i
                k_t, v_t = _load(pl.multiple_of(kj * _tk, _tk))
                return tile_step(st, k_t, v_t, _pos0 + kj * _tk, _tk, False, _upper)
            states = lax.fori_loop(0, n_full, body, states)

    lane = lax.broadcasted_iota(jnp.int32, (tq, LANES), 1)
    if kind == "sb":
        o = jnp.where(lane < 64, states[0][1], states[1][1])
    elif kind == "mla":
        o = jnp.where(lane < 64, states[0][2] / states[0][1], states[1][2] / states[1][1])
    else:
        lv = lam_ref[...]
        lam = (jnp.exp(jnp.sum(lv[0:1] * lv[1:2], axis=-1, keepdims=True))
               - jnp.exp(jnp.sum(lv[2:3] * lv[3:4], axis=-1, keepdims=True)) + lam_init)
        o = states[0][2] / states[0][1] - lam * (states[1][2] / states[1][1])
        o = _rmsnorm(o, gsub_ref[...]) * (1.0 - lam_init)
    o_ref[0] = o.astype(BF16)


def _attention(kind, q3, sources, *, q_pos0, extras=(), lam_init=0.0):
    b, sq, qtot = q3.shape
    qw = 2 * LANES if kind == "mla" else LANES
    n_groups = qtot // qw
    tq = min(ATTN_TILE, sq)
    assert sq % tq == 0
    srcs, in_specs, args, scratch = [], [], [], []
    if kind == "diff":
        in_specs.append(pl.BlockSpec(memory_space=pltpu.SMEM))
        args.append(extras[0])
    in_specs.append(pl.BlockSpec((1, tq, qw), lambda bi, g, qi: (bi, qi, g)))
    args.append(q3)
    for s in sources:
        k, v = s["k"], s["v"]
        sk = k.shape[1]
        tk = tq if s["mode"] == "self" else (sk if s["mode"] == "masked" else min(ATTN_TILE, sk))
        assert sk % tk == 0 and k.shape[2] == n_groups * qw and v.shape[2] == n_groups * LANES
        cast = k.dtype != BF16
        srcs.append(dict(mode=s["mode"], pos0=s["pos0"], sk=sk, tk=tk, cast=cast))
        in_specs.append(pl.BlockSpec((1, sk, qw), lambda bi, g, qi: (bi, 0, g)))
        in_specs.append(pl.BlockSpec((1, sk, LANES), lambda bi, g, qi: (bi, 0, g)))
        args += [k, v]
    if kind == "diff":
        in_specs += [_const_spec(extras[1].shape), _const_spec(extras[2].shape)]
        args += [extras[1], extras[2]]
    for s in srcs:
        if s["cast"]:
            scratch += [pltpu.VMEM((s["sk"], qw), BF16), pltpu.VMEM((s["sk"], LANES), BF16)]
    kern = functools.partial(_attn_kernel, kind=kind, srcs=srcs, tq=tq, q_pos0=q_pos0,
                             lam_init=lam_init)
    return pl.pallas_call(
        kern,
        grid=(b, n_groups, sq // tq),
        in_specs=in_specs,
        out_specs=pl.BlockSpec((1, tq, LANES), lambda bi, g, qi: (bi, qi, g)),
        out_shape=jax.ShapeDtypeStruct((b, sq, n_groups * LANES), BF16),
        scratch_shapes=scratch,
        compiler_params=_cparams(3),
        name="attn_" + kind,
    )(*args)


def _merge_kernel(x_ref, gmix_ref, wg_ref, om_ref, os_ref, od_ref, wbr_ref, wout_ref, o_ref):
    x = x_ref[...]
    d = x.shape[1]
    h = _rmsnorm(x, gmix_ref[...]).astype(BF16)
    merged = None
    for i, br_ref in enumerate((om_ref, os_ref, od_ref)):
        gate = _sigmoid(_dot(h, wg_ref[:, i * d:(i + 1) * d]))
        term = gate * _dot(br_ref[...], wbr_ref[i])
        merged = term if merged is None else merged + term
    o_ref[...] = x + _dot(merged.astype(BF16), wout_ref[...])


def _merge(x2d, o_mla, o_sb, o_diff, lw):
    t, d = x2d.shape
    tm = ROW_TILE
    row = lambda w: pl.BlockSpec((tm, w), lambda i: (i, 0))
    return pl.pallas_call(
        _merge_kernel,
        grid=(t // tm,),
        in_specs=[row(d), _const_spec(lw["gmix"].shape), _const_spec(lw["wg"].shape),
                  row(o_mla.shape[1]), row(o_sb.shape[1]), row(o_diff.shape[1]),
                  _const_spec(lw["wbr"].shape), _const_spec(lw["wout"].shape)],
        out_specs=row(d),
        out_shape=jax.ShapeDtypeStruct((t, d), F32),
        compiler_params=_cparams(1),
        name="merge",
    )(x2d, lw["gmix"], lw["wg"], o_mla, o_sb, o_diff, lw["wbr"], lw["wout"])


def _conv_rows(a, h0, h1, cw, cb):
    n = a.shape[0]
    row = lax.broadcasted_iota(jnp.int32, (n, 1), 0)
    p1 = jnp.where(row == 0, h1, pltpu.roll(a, 1, 0))
    p2 = jnp.where(row == 0, h0, jnp.where(row == 1, h1, pltpu.roll(a, 2, 0)))
    return p2 * cw[0:1] + p1 * cw[1:2] + a * cw[2:3] + cb


def _ffn_kernel(*refs, seg, d_ff, has_hist):
    it = iter(refs)
    x_ref, g_ref, wup_ref, cw_ref, cb_ref = next(it), next(it), next(it), next(it), next(it)
    hist_ref = next(it) if has_hist else None
    wdn_ref, o_ref, conv_ref = next(it), next(it), next(it)
    carry_ref = next(it) if not has_hist else None
    act_ref = next(it) if has_hist else None

    x = x_ref[...]
    tm = x.shape[0]
    h = _rmsnorm(x, g_ref[...]).astype(BF16)
    a = _dot(h, wup_ref[:, 0:d_ff])
    u = _dot(h, wup_ref[:, d_ff:2 * d_ff])
    cw, cb = cw_ref[...], cb_ref[...]

    if not has_hist:
        i = pl.program_id(0)
        tiles_per_seq = seg // tm
        first = (i % tiles_per_seq) == 0

        @pl.when(first)
        def _():
            carry_ref[...] = jnp.zeros_like(carry_ref)

        tail = carry_ref[...]
        c = _conv_rows(a, tail[6:7], tail[7:8], cw, cb)
        carry_ref[...] = a[tm - 8:tm]

        @pl.when((i % tiles_per_seq) == tiles_per_seq - 1)
        def _():
            conv_ref[0] = a[tm - 2:tm]

        act = (c * _sigmoid(c) * u).astype(BF16)
    else:
        for s in range(tm // seg):
            a_s = a[s * seg:(s + 1) * seg]
            hs = hist_ref[s]
            c = _conv_rows(a_s, hs[0:1], hs[1:2], cw, cb)
            act_ref[s * seg:(s + 1) * seg, :] = (c * _sigmoid(c) * u[s * seg:(s + 1) * seg]).astype(BF16)
            conv_ref[s] = a_s[seg - 2:seg]
        act = act_ref[...]

    o_ref[...] = x + _dot(act, wdn_ref[...])


def _ffn(x2d, lw, *, seg, hist=None):
    t, d = x2d.shape
    tm = ROW_TILE
    d_ff = lw["wdn"].shape[0]
    has_hist = hist is not None
    row = lambda w: pl.BlockSpec((tm, w), lambda i: (i, 0))
    in_specs = [row(d), _const_spec(lw["gffn"].shape), _const_spec(lw["wup"].shape),
                _const_spec(lw["cw"].shape), _const_spec(lw["cb"].shape)]
    args = [x2d, lw["gffn"], lw["wup"], lw["cw"], lw["cb"]]
    if has_hist:
        assert tm % seg == 0 and seg % 8 == 0
        nseq = tm // seg
        in_specs.append(pl.BlockSpec((nseq, 2, d_ff), lambda i: (i, 0, 0)))
        args.append(hist)
        conv_spec = pl.BlockSpec((nseq, 2, d_ff), lambda i: (i, 0, 0))
        scratch = [pltpu.VMEM((tm, d_ff), BF16)]
    else:
        assert seg % tm == 0
        tps = seg // tm
        conv_spec = pl.BlockSpec((1, 2, d_ff), lambda i: (i // tps, 0, 0))
        scratch = [pltpu.VMEM((8, d_ff), F32)]
    in_specs.append(_const_spec(lw["wdn"].shape))
    args.append(lw["wdn"])
    return pl.pallas_call(
        functools.partial(_ffn_kernel, seg=seg, d_ff=d_ff, has_hist=has_hist),
        grid=(t // tm,),
        in_specs=in_specs,
        out_specs=[row(d), conv_spec],
        out_shape=[jax.ShapeDtypeStruct((t, d), F32),
                   jax.ShapeDtypeStruct((t // seg, 2, d_ff), F32)],
        scratch_shapes=scratch,
        compiler_params=_cparams(1),
        name="ffn",
    )(*args)


def _group_matrix(groups, width):
    m = np.zeros((width, width), np.float32)
    for g in groups:
        for i in g:
            m[i, g] = 1.0 / len(g)
    return jnp.asarray(m, dtype=BF16)


def _pack_layer(w, l, dims):
    nope, rope, heads = dims["nope"], dims["rope"], dims["mla_heads"]
    q_rank, kv_rank = dims["q_rank"], dims["kv_rank"]
    sb_w, df_w, d = dims["sb_w"], dims["df_w"], dims["d_model"]
    half = rope // 2
    assert nope == 64 and rope == 32 and heads % 2 == 0 and q_rank + kv_rank + rope <= 4 * LANES
    assert sb_w % LANES == 0 and df_w % (2 * LANES) == 0
    w_in = w["w_in"][l]
    o_sb = q_rank + kv_rank + rope
    o_df = o_sb + 3 * sb_w
    o_g = o_df + 3 * df_w
    wa = jnp.pad(w_in[:, :o_sb], ((0, 0), (0, 4 * LANES - o_sb)))
    w1 = jnp.concatenate([wa, w_in[:, o_sb:o_g]], axis=1).astype(BF16)

    cols = np.full((heads // 2) * 2 * LANES, -1, np.int64)
    gain_src = np.full(cols.shape, -1, np.int64)
    for p in range(heads // 2):
        base = 2 * LANES * p
        for j, hd in enumerate((2 * p, 2 * p + 1)):
            hb = hd * (nope + rope)
            cols[base + nope * j:base + nope * (j + 1)] = hb + np.arange(nope)
            gain_src[base + nope * j:base + nope * (j + 1)] = np.arange(nope)
            x1 = base + 2 * nope + half * j
            x2 = base + 2 * nope + 2 * half + half * j
            cols[x1:x1 + half] = hb + nope + np.arange(half)
            cols[x2:x2 + half] = hb + nope + half + np.arange(half)
            gain_src[x1:x1 + half] = nope + np.arange(half)
            gain_src[x2:x2 + half] = nope + half + np.arange(half)
    valid = jnp.asarray(cols >= 0)
    wuq = jnp.where(valid[None, :], w["mla_w_uq"][l][:, np.maximum(cols, 0)], 0.0).astype(BF16)
    qg = jnp.concatenate([w["mla_qn_g"][l], w["mla_qr_g"][l]])
    gq = jnp.where(valid, qg[np.maximum(gain_src, 0)], 0.0) * ((nope + rope) ** -0.5)

    groups = [list(range(0, nope)), list(range(nope, 2 * nope))]
    for j in range(2):
        groups.append(list(range(2 * nope + half * j, 2 * nope + half * (j + 1)))
                      + list(range(2 * nope + 2 * half + half * j, 2 * nope + 2 * half + half * (j + 1))))
    gm_mla = _group_matrix(groups, 2 * LANES)
    gm64 = _group_matrix([list(range(64 * j, 64 * (j + 1))) for j in range(4)], 2 * LANES)

    rep = np.zeros((rope, LANES), np.float32)
    for j in range(half):
        rep[j, [j, half + j]] = 1.0
        rep[half + j, [2 * half + j, 3 * half + j]] = 1.0

    row = lambda v: v.reshape(1, -1).astype(F32)
    return dict(
        gmix=row(w["mix_norm_g"][l]), w1=w1, gcq=row(w["mla_q_norm_g"][l]), wuq=wuq, gm_mla=gm_mla,
        gq=row(gq), gckv=row(w["mla_kv_norm_g"][l]),
        gkr=row(jnp.pad(w["mla_kr_g"][l], (0, LANES - rope))),
        gm64=gm64,
        gdq=row(jnp.tile(w["diff_qn_g"][l], df_w // 64) * (1.0 / 8.0)),
        gdk=row(jnp.tile(w["diff_kn_g"][l], df_w // 64)),
        wukv=jnp.concatenate([w["mla_w_uk"][l], w["mla_w_uv"][l]], axis=1).astype(BF16),
        gkn=row(jnp.tile(w["mla_kn_g"][l], heads)),
        rep=jnp.asarray(rep, dtype=BF16),
        lam=w["diff_lambda"][l].astype(F32), gsub=row(w["diff_subln_g"][l]),
        wg=w_in[:, o_g:].astype(BF16),
        wbr=jnp.stack([w["w_br_mla"][l], w["w_br_sb"][l], w["w_br_diff"][l]]).astype(BF16),
        wout=w["w_out"][l].astype(BF16),
        gffn=row(w["ffn_norm_g"][l]), wup=w["ffn_w_up"][l].astype(BF16),
        cw=w["ffn_conv_w"][l].astype(F32), cb=row(w["ffn_conv_b"][l]),
        wdn=w["ffn_w_down"][l].astype(BF16),
    )


def _rope_table(pos, rope, n_rows):
    half = rope // 2
    inv = ROPE_BASE ** (-jnp.arange(half, dtype=F32) / half)
    ang = pos.astype(F32)[:, None] * inv[None, :]
    cos, sin = jnp.cos(ang), jnp.sin(ang)
    z = lambda n: jnp.zeros((pos.shape[0], n), F32)
    tab = jnp.concatenate([
        cos, cos, cos, cos, z(LANES - 4 * half),
        -sin, -sin, sin, sin, z(LANES - 4 * half),
        cos, cos, z(LANES - 2 * half),
        -sin, sin, z(LANES - 2 * half)], axis=1)
    return jnp.tile(tab, (n_rows // pos.shape[0], 1))


def _layer(x2d, lw, dims, *, batch, seq, pos0, tab, past, conv_hist, layer_idx, slopes):
    q, ckv, kr, sq, sk, sv, dq, dk, dv = _in_proj(x2d, lw, tab, dims)
    r3 = lambda a: a.reshape(batch, seq, a.shape[-1])
    kpk, mv = _kv_up(ckv, kr, lw)
    lam_init = 0.8 - 0.6 * math.exp(-0.3 * layer_idx)
    if past is None:
        mla_src = [dict(k=r3(kpk), v=r3(mv), pos0=0, mode="self")]
        sb_src = [dict(k=r3(sk), v=r3(sv), pos0=0, mode="self")]
        df_src = [dict(k=r3(dk), v=r3(dv), pos0=0, mode="self")]
    else:
        c_ckv, c_kr, c_sk, c_sv, c_dk, c_dv = past
        pb, plen = c_ckv.shape[0], c_ckv.shape[1]
        ckpk, cmv = _kv_up(c_ckv.reshape(pb * plen, -1), c_kr.reshape(pb * plen, -1), lw)
        c3 = lambda a: a.reshape(pb, plen, -1)
        mla_src = [dict(k=c3(ckpk), v=c3(cmv), pos0=0, mode="full"),
                   dict(k=r3(kpk), v=r3(mv), pos0=plen, mode="masked")]
        sb_src = [dict(k=c3(c_sk), v=c3(c_sv), pos0=0, mode="full"),
                  dict(k=r3(sk), v=r3(sv), pos0=plen, mode="masked")]
        df_src = [dict(k=c3(c_dk), v=c3(c_dv), pos0=0, mode="full"),
                  dict(k=r3(dk), v=r3(dv), pos0=plen, mode="masked")]
    o_mla = _attention("mla", r3(q), mla_src, q_pos0=pos0)
    o_sb = _attention("sb", r3(sq), sb_src, q_pos0=pos0)
    o_df = _attention("diff", r3(dq), df_src, q_pos0=pos0, extras=(slopes, lw["lam"], lw["gsub"]),
                      lam_init=lam_init)
    f2 = lambda a: a.reshape(batch * seq, a.shape[-1])
    x1 = _merge(x2d, f2(o_mla), f2(o_sb), f2(o_df), lw)
    x2, conv = _ffn(x1, lw, seg=seq, hist=conv_hist)
    return x2, (ckv, kr, sk, sv, dk, dv), conv


def kernel(x_prompt, x_sample, cache_mla_ckv, cache_mla_krope, cache_sb_k, cache_sb_v, cache_diff_k, cache_diff_v, state_ffn_conv, mix_norm_g, w_in, mla_q_norm_g, mla_w_uq, mla_kv_norm_g, mla_w_uk, mla_w_uv, mla_qn_g, mla_kn_g, mla_qr_g, mla_kr_g, diff_qn_g, diff_kn_g, diff_lambda, diff_subln_g, w_br_mla, w_br_sb, w_br_diff, w_out, ffn_norm_g, ffn_w_up, ffn_conv_w, ffn_conv_b, ffn_w_down):
    w = dict(mix_norm_g=mix_norm_g, w_in=w_in, mla_q_norm_g=mla_q_norm_g, mla_w_uq=mla_w_uq,
             mla_kv_norm_g=mla_kv_norm_g, mla_w_uk=mla_w_uk, mla_w_uv=mla_w_uv, mla_qn_g=mla_qn_g,
             mla_kn_g=mla_kn_g, mla_qr_g=mla_qr_g, mla_kr_g=mla_kr_g, diff_qn_g=diff_qn_g,
             diff_kn_g=diff_kn_g, diff_lambda=diff_lambda, diff_subln_g=diff_subln_g,
             w_br_mla=w_br_mla, w_br_sb=w_br_sb, w_br_diff=w_br_diff, w_out=w_out,
             ffn_norm_g=ffn_norm_g, ffn_w_up=ffn_w_up, ffn_conv_w=ffn_conv_w,
             ffn_conv_b=ffn_conv_b, ffn_w_down=ffn_w_down)
    depth = w_in.shape[0]
    bp, sp, d = x_prompt.shape
    bs, ss, _ = x_sample.shape
    past_len = cache_mla_ckv.shape[2]
    sb_heads, sb_dim = cache_sb_k.shape[3], cache_sb_k.shape[4]
    df_heads, df_dim = cache_diff_k.shape[3], cache_diff_k.shape[5]
    assert sb_dim == 64 and df_dim == 64 and ffn_conv_w.shape[1] == 3
    nope, rope = mla_qn_g.shape[1], mla_qr_g.shape[1]
    dims = dict(d_model=d, q_rank=mla_q_norm_g.shape[1], kv_rank=mla_kv_norm_g.shape[1],
                nope=nope, rope=rope, mla_heads=mla_w_uk.shape[2] // nope,
                sb_w=sb_heads * sb_dim, df_w=2 * df_heads * df_dim)
    d_ff = ffn_conv_b.shape[1]

    tab_p = _rope_table(jnp.arange(sp, dtype=jnp.int32), rope, max(sp, ROW_TILE))
    tab_s = _rope_table(past_len + jnp.arange(ss, dtype=jnp.int32), rope, max(ss, ROW_TILE))
    slopes = 2.0 ** (-8.0 * jnp.arange(1, df_heads + 1, dtype=F32) / df_heads)

    xp = x_prompt.reshape(bp * sp, d)
    xs = x_sample.reshape(bs * ss, d)
    rows_p, rows_s, conv_p, conv_s = [], [], [], []
    for l in range(depth):
        lw = _pack_layer(w, l, dims)
        xp, rp, cp = _layer(xp, lw, dims, batch=bp, seq=sp, pos0=0, tab=tab_p, past=None,
                            conv_hist=None, layer_idx=l, slopes=slopes)
        past = (cache_mla_ckv[l], cache_mla_krope[l],
                cache_sb_k[l].reshape(bs, past_len, -1), cache_sb_v[l].reshape(bs, past_len, -1),
                cache_diff_k[l].reshape(bs, past_len, -1), cache_diff_v[l].reshape(bs, past_len, -1))
        xs, rs, cs = _layer(xs, lw, dims, batch=bs, seq=ss, pos0=past_len, tab=tab_s, past=past,
                            conv_hist=state_ffn_conv[l], layer_idx=l, slopes=slopes)
        rows_p.append(rp)
        rows_s.append(rs)
        conv_p.append(cp)
        conv_s.append(cs)

    def stack(rows, i, b, s, tail):
        return jnp.stack([r[i] for r in rows]).reshape((depth, b, s) + tail)

    def group(rows, b, s):
        return (stack(rows, 0, b, s, (dims["kv_rank"],)), stack(rows, 1, b, s, (rope,)),
                stack(rows, 2, b, s, (sb_heads, sb_dim)), stack(rows, 3, b, s, (sb_heads, sb_dim)),
                stack(rows, 4, b, s, (df_heads, 2, df_dim)), stack(rows, 5, b, s, (df_heads, 2 * df_dim)))

    return ((xp.reshape(bp, sp, d), xs.reshape(bs, ss, d)) + group(rows_p, bp, sp)
            + (jnp.stack(conv_p),) + group(rows_s, bs, ss) + (jnp.stack(conv_s),))
```

```python
import functools
import math

import numpy as np
import jax
import jax.numpy as jnp
from jax import lax
from jax.experimental import pallas as pl
from jax.experimental.pallas import tpu as pltpu

F32 = jnp.float32
BF16 = jnp.bfloat16

EPS = 1e-6
CHUNK = 64
CHUNK_SHIFT = CHUNK.bit_length() - 1
assert 1 << CHUNK_SHIFT == CHUNK
ROPE_BASE = 10000.0
NEG_BIG = -1e30

LANES = 128
ROW_TILE = 512
KV_ROW_TILE = 1024
ATTN_TILE = 512
SB_BLOCK = 256
LOG2E = math.log2(math.e)
VMEM_LIMIT = 52 * 1024 * 1024


def _cparams(n_axes):
    return pltpu.CompilerParams(dimension_semantics=("arbitrary",) * n_axes,
                                vmem_limit_bytes=VMEM_LIMIT)


def _const_spec(shape):
    nd = len(shape)
    return pl.BlockSpec(shape, lambda *_: (0,) * nd, pipeline_mode=pl.Buffered(1))


def _dot(a, b):
    return jnp.dot(a, b, preferred_element_type=F32)


def _dot_nt(a, b):
    return lax.dot_general(a, b, (((1,), (1,)), ((), ())), preferred_element_type=F32)


def _rmsnorm(x, g):
    return x * lax.rsqrt(jnp.mean(x * x, axis=-1, keepdims=True) + EPS) * g


def _group_mean_sq(x, gmat):
    sq = x * x
    hi = sq.astype(BF16)
    lo = (sq - hi.astype(F32)).astype(BF16)
    return _dot(hi, gmat) + _dot(lo, gmat)


def _sigmoid(x):
    return 1.0 / (1.0 + jnp.exp(-x))


def _in_proj_kernel(x_ref, gmix_ref, w_ref, gcq_ref, wuq_ref, gm_mla_ref, gq_ref, gckv_ref,
                    gkr_ref, tab_ref, gm64_ref, gdq_ref, gdk_ref,
                    q_ref, ckv_ref, kr_ref, sq_ref, sk_ref, sv_ref, dq_ref, dk_ref, dv_ref,
                    *, q_rank, kv_rank, rope, sb_w, df_w):
    tm = x_ref.shape[0]
    half = rope // 2
    h = _rmsnorm(x_ref[...], gmix_ref[...]).astype(BF16)

    za = _dot(h, w_ref[:, 0:4 * LANES])
    cqn = _rmsnorm(za[:, 0:q_rank], gcq_ref[...]).astype(BF16)
    q = _dot(cqn, wuq_ref[...])
    tab = tab_ref[...]
    cos_q, sin_q = tab[:, 0:LANES], tab[:, LANES:2 * LANES]
    cos_k, sin_k = tab[:, 2 * LANES:3 * LANES], tab[:, 3 * LANES:4 * LANES]
    lane = lax.broadcasted_iota(jnp.int32, (tm, LANES), 1)
    n_pairs = q.shape[1] // (2 * LANES)
    for p in range(n_pairs):
        lo = 2 * LANES * p
        blk = q[:, lo:lo + 2 * LANES]
        y = blk * lax.rsqrt(_group_mean_sq(blk, gm_mla_ref[...]) + EPS) * gq_ref[:, lo:lo + 2 * LANES]
        rp = y[:, LANES:]
        partner = jnp.where(lane < 2 * half, pltpu.roll(rp, LANES - 2 * half, 1),
                            pltpu.roll(rp, 2 * half, 1))
        q_ref[:, lo:lo + LANES] = y[:, :LANES].astype(BF16)
        q_ref[:, lo + LANES:lo + 2 * LANES] = (rp * cos_q + partner * sin_q).astype(BF16)

    ckv_ref[...] = _rmsnorm(za[:, q_rank:q_rank + kv_rank], gckv_ref[...])

    kr = za[:, q_rank + kv_rank:4 * LANES]
    ms = jnp.sum(kr * kr, axis=-1, keepdims=True) * (1.0 / rope)
    krn = kr * lax.rsqrt(ms + EPS) * gkr_ref[...]
    partner = jnp.where(lane < half, pltpu.roll(krn, LANES - half, 1), pltpu.roll(krn, half, 1))
    kr_ref[...] = (krn * cos_k + partner * sin_k)[:, :rope]

    o = 4 * LANES
    zs = _dot(h, w_ref[:, o:o + 3 * sb_w])
    sq_ref[...] = (zs[:, 0:sb_w] * (LOG2E / 8.0)).astype(BF16)
    sk_ref[...] = zs[:, sb_w:2 * sb_w]
    sv_ref[...] = zs[:, 2 * sb_w:3 * sb_w]

    o = o + 3 * sb_w
    zd = _dot(h, w_ref[:, o:o + 3 * df_w])
    for s in range(df_w // (2 * LANES)):
        lo = 2 * LANES * s
        blk = zd[:, lo:lo + 2 * LANES]
        r = lax.rsqrt(_group_mean_sq(blk, gm64_ref[...]) + EPS)
        dq_ref[:, lo:lo + 2 * LANES] = (blk * r * gdq_ref[:, lo:lo + 2 * LANES]).astype(BF16)
        blk = zd[:, df_w + lo:df_w + lo + 2 * LANES]
        r = lax.rsqrt(_group_mean_sq(blk, gm64_ref[...]) + EPS)
        dk_ref[:, lo:lo + 2 * LANES] = blk * r * gdk_ref[:, lo:lo + 2 * LANES]
    dv_ref[...] = zd[:, 2 * df_w:3 * df_w]


def _in_proj(x2d, lw, tab, dims):
    t, d = x2d.shape
    tm = ROW_TILE
    assert t % tm == 0 and tab.shape[0] % tm == 0
    n_tab = tab.shape[0] // tm
    row = lambda w: pl.BlockSpec((tm, w), lambda i: (i, 0))
    consts = [lw["gmix"], lw["w1"], lw["gcq"], lw["wuq"], lw["gm_mla"], lw["gq"], lw["gckv"], lw["gkr"]]
    consts2 = [lw["gm64"], lw["gdq"], lw["gdk"]]
    sb_w, df_w = dims["sb_w"], dims["df_w"]
    out_shape = [
        jax.ShapeDtypeStruct((t, lw["wuq"].shape[1]), BF16),
        jax.ShapeDtypeStruct((t, dims["kv_rank"]), F32),
        jax.ShapeDtypeStruct((t, dims["rope"]), F32),
        jax.ShapeDtypeStruct((t, sb_w), BF16),
        jax.ShapeDtypeStruct((t, sb_w), F32),
        jax.ShapeDtypeStruct((t, sb_w), F32),
        jax.ShapeDtypeStruct((t, df_w), BF16),
        jax.ShapeDtypeStruct((t, df_w), F32),
        jax.ShapeDtypeStruct((t, df_w), F32),
    ]
    kern = functools.partial(_in_proj_kernel, q_rank=dims["q_rank"], kv_rank=dims["kv_rank"],
                             rope=dims["rope"], sb_w=sb_w, df_w=df_w)
    return pl.pallas_call(
        kern,
        grid=(t // tm,),
        in_specs=([row(d)] + [_const_spec(c.shape) for c in consts]
                  + [pl.BlockSpec((tm, tab.shape[1]), lambda i: (i % n_tab, 0))]
                  + [_const_spec(c.shape) for c in consts2]),
        out_specs=[row(s.shape[1]) for s in out_shape],
        out_shape=out_shape,
        compiler_params=_cparams(1),
        name="in_proj",
    )(x2d, *consts, tab, *consts2)


def _kv_up_kernel(ckv_ref, kr_ref, wukv_ref, gm64_ref, gkn_ref, rep_ref, k_ref, v_ref, *, kn_w):
    c = ckv_ref[...].astype(BF16)
    kv = _dot(c, wukv_ref[...])
    krrep = _dot(kr_ref[...].astype(BF16), rep_ref[...]).astype(BF16)
    for b in range(kn_w // (2 * LANES)):
        s = 2 * LANES * b
        blk = kv[:, s:s + 2 * LANES]
        r = lax.rsqrt(_group_mean_sq(blk, gm64_ref[...]) + EPS)
        kn = (blk * r * gkn_ref[:, s:s + 2 * LANES]).astype(BF16)
        for j in range(2):
            p = 2 * b + j
            k_ref[:, 2 * LANES * p:2 * LANES * p + LANES] = kn[:, LANES * j:LANES * (j + 1)]
            k_ref[:, 2 * LANES * p + LANES:2 * LANES * (p + 1)] = krrep
    v_ref[...] = kv[:, kn_w:].astype(BF16)


def _kv_up(ckv2d, kr2d, lw):
    r, kvr = ckv2d.shape
    tr = min(KV_ROW_TILE, r)
    assert r % tr == 0
    kn_w = lw["gkn"].shape[1]
    consts = [lw["wukv"], lw["gm64"], lw["gkn"], lw["rep"]]
    out_shape = [jax.ShapeDtypeStruct((r, 2 * kn_w), BF16),
                 jax.ShapeDtypeStruct((r, lw["wukv"].shape[1] - kn_w), BF16)]
    return pl.pallas_call(
        functools.partial(_kv_up_kernel, kn_w=kn_w),
        grid=(r // tr,),
        in_specs=[pl.BlockSpec((tr, kvr), lambda i: (i, 0)),
                  pl.BlockSpec((tr, kr2d.shape[1]), lambda i: (i, 0))]
                 + [_const_spec(c.shape) for c in consts],
        out_specs=[pl.BlockSpec((tr, s.shape[1]), lambda i: (i, 0)) for s in out_shape],
        out_shape=out_shape,
        compiler_params=_cparams(1),
        name="mla_kv_up",
    )(ckv2d, kr2d, *consts)


def _stream_lane_masks(kind, ns, width):
    lane = lax.broadcasted_iota(jnp.int32, (1, width), 1)
    masks = []
    for i in range(ns):
        if kind == "mla":
            r = lane - 2 * LANES * (i // 2)
            j = i % 2
            masks.append(((r >= 64 * j) & (r < 64 * (j + 1)))
                         | ((r >= 128 + 16 * j) & (r < 128 + 16 * (j + 1)))
                         | ((r >= 160 + 16 * j) & (r < 160 + 16 * (j + 1))))
        else:
            masks.append((lane >= 64 * i) & (lane < 64 * (i + 1)))
    return masks


def _attn_kernel(*refs, kind, srcs, ns, tq, q_pos0, lam_init):
    it = iter(refs)
    slopes_ref = next(it) if kind == "diff" else None
    q_ref = next(it)
    kv_refs = [(next(it), next(it)) for _ in srcs]
    if kind == "diff":
        lam_ref, gsub_ref = next(it), next(it)
    o_ref = next(it)
    scr = [(next(it), next(it)) if s["cast"] else None for s in srcs]

    grp = pl.program_id(1)
    qi = pl.program_id(2)
    rows = ns * tq

    for s, sc, (k_ref, v_ref) in zip(srcs, scr, kv_refs):
        if sc is not None:
            @pl.when(qi == 0)
            def _():
                sc[0][...] = k_ref[0].astype(BF16)
                sc[1][...] = v_ref[0].astype(BF16)

    q = q_ref[0]
    zero = jnp.zeros_like(q)
    qs = jnp.concatenate([jnp.where(m, q, zero) for m in _stream_lane_masks(kind, ns, q.shape[1])],
                         axis=0)
    q_start = q_pos0 + qi * tq
    qpos = jnp.concatenate([q_start + lax.broadcasted_iota(jnp.int32, (tq, 1), 0)] * ns, axis=0)
    vw = kv_refs[0][1].shape[2]
    if kind == "diff":
        slope = jnp.concatenate(
            [jnp.full((tq, 1), slopes_ref[grp * (ns // 2) + i // 2], F32) for i in range(ns)], axis=0)

    if kind == "sb":
        init = (jnp.zeros((rows, 1), F32), jnp.zeros((rows, vw), F32))
    else:
        init = (jnp.full((rows, 1), NEG_BIG, F32), jnp.zeros((rows, 1), F32),
                jnp.zeros((rows, vw), F32))

    def tile_step(st, k_t, v_t, k_start, tk, masked, upper):
        kpos = k_start + lax.broadcasted_iota(jnp.int32, (1, tk), 1)
        s = _dot_nt(qs, k_t)
        if masked:
            valid = (kpos < qpos) if kind == "sb" else ((kpos >> CHUNK_SHIFT) <= (qpos >> CHUNK_SHIFT))
        if kind == "sb":
            c, acc = st
            lp = jnp.log(1.0 + jnp.exp2(-jnp.abs(s))) * LOG2E
            ls = jnp.minimum(s, 0.0) - lp
            l1 = ls - s
            if masked:
                l1 = jnp.where(valid, l1, 0.0)
            hi = l1.astype(BF16)
            lo = (l1 - hi.astype(F32)).astype(BF16)
            cb = upper.shape[0]
            nb = tk // cb
            if nb > 1:
                hi = jnp.concatenate([hi[:, cb * j:cb * (j + 1)] for j in range(nb)], axis=0)
                lo = jnp.concatenate([lo[:, cb * j:cb * (j + 1)] for j in range(nb)], axis=0)
            suffix = _dot(hi, upper) + _dot(lo, upper)
            parts = [None] * nb
            for j in reversed(range(nb)):
                parts[j] = jnp.exp2(ls[:, cb * j:cb * (j + 1)] + suffix[rows * j:rows * (j + 1)] + c)
                c = c + jnp.sum(l1[:, cb * j:cb * (j + 1)], axis=-1, keepdims=True)
            a = parts[0] if nb == 1 else jnp.concatenate(parts, axis=1)
            if masked:
                a = jnp.where(valid, a, 0.0)
            return (c, acc + _dot(a.astype(BF16), v_t))
        m, l, acc = st
        if kind == "diff":
            dist = qpos - kpos
            s = s - slope * (jnp.abs(dist) if masked else dist).astype(F32)
        if masked:
            s = jnp.where(valid, s, -jnp.inf)
        m_new = jnp.maximum(m, jnp.max(s, axis=-1, keepdims=True))
        alpha = jnp.exp2(m - m_new)
        p = jnp.exp2(s - m_new)
        l = alpha * l + jnp.sum(p, axis=-1, keepdims=True)
        return (m_new, l, alpha * acc + _dot(p.astype(BF16), v_t))

    def upper_tri(tk):
        if kind != "sb":
            return None
        cb = min(tk, SB_BLOCK)
        r = lax.broadcasted_iota(jnp.int32, (cb, cb), 0)
        c = lax.broadcasted_iota(jnp.int32, (cb, cb), 1)
        return jnp.where(r > c, 1.0, 0.0).astype(BF16)

    st = init
    for s, sc, (k_ref, v_ref) in reversed(list(zip(srcs, scr, kv_refs))):
        tk = s["tk"]
        upper = upper_tri(tk)

        def load(k_off, _sc=sc, _k=k_ref, _v=v_ref, _tk=tk):
            if _sc is not None:
                return _sc[0][pl.ds(k_off, _tk), :], _sc[1][pl.ds(k_off, _tk), :]
            return _k[0, pl.ds(k_off, _tk), :], _v[0, pl.ds(k_off, _tk), :]

        if s["mode"] == "self":
            k_t, v_t = load(pl.multiple_of(qi * tk, tk))
            st = tile_step(st, k_t, v_t, s["pos0"] + qi * tk, tk, True, upper)
            n_full = qi
        elif s["mode"] == "masked":
            k_t, v_t = load(0)
            st = tile_step(st, k_t, v_t, s["pos0"], tk, True, upper)
            continue
        else:
            n_full = s["sk"] // tk

        def body(i, st_, _tk=tk, _load=load, _pos0=s["pos0"], _n=n_full, _upper=upper):
            kj = _n - 1 - i
            k_t, v_t = _load(pl.multiple_of(kj * _tk, _tk))
            return tile_step(st_, k_t, v_t, _pos0 + kj * _tk, _tk, False, _upper)

        if isinstance(n_full, int) and n_full <= 2:
            for i in range(n_full):
                st = body(i, st)
        else:
            st = lax.fori_loop(0, n_full, body, st)

    acc = st[-1]
    if kind == "diff":
        l = st[1]
        lv = lam_ref[...]
        lam = (jnp.exp(jnp.sum(lv[0:1] * lv[1:2], axis=-1, keepdims=True))
               - jnp.exp(jnp.sum(lv[2:3] * lv[3:4], axis=-1, keepdims=True)) + lam_init)
        for hh in range(ns // 2):
            lanes = slice(2 * 64 * hh, 2 * 64 * (hh + 1))
            r0, r1 = slice(tq * 2 * hh, tq * (2 * hh + 1)), slice(tq * (2 * hh + 1), tq * (2 * hh + 2))
            o = acc[r0, lanes] / l[r0] - lam * (acc[r1, lanes] / l[r1])
            o_ref[0, :, lanes] = (_rmsnorm(o, gsub_ref[...]) * (1.0 - lam_init)).astype(BF16)
    else:
        lane = lax.broadcasted_iota(jnp.int32, (tq, vw), 1)
        o = jnp.zeros((tq, vw), F32)
        for i in range(ns):
            blk = acc[tq * i:tq * (i + 1)]
            if kind == "mla":
                blk = blk / st[1][tq * i:tq * (i + 1)]
            o = jnp.where((lane >= 64 * i) & (lane < 64 * (i + 1)), blk, o)
        o_ref[0] = o.astype(BF16)


def _attention(kind, q3, sources, *, ns, q_pos0, extras=(), lam_init=0.0):
    b, sq, qtot = q3.shape
    w = (2 * 64 if kind == "mla" else 64) * ns
    vw = 64 * ns
    n_groups = qtot // w
    tq = min(ATTN_TILE, sq)
    assert sq % tq == 0 and qtot % w == 0
    srcs, in_specs, args, scratch = [], [], [], []
    if kind == "diff":
        in_specs.append(pl.BlockSpec(memory_space=pltpu.SMEM))
        args.append(extras[0])
    in_specs.append(pl.BlockSpec((1, tq, w), lambda bi, g, qi: (bi, qi, g)))
    args.append(q3)
    for s in sources:
        k, v = s["k"], s["v"]
        sk = k.shape[1]
        tk = tq if s["mode"] == "self" else (sk if s["mode"] == "masked" else s.get("tk", min(ATTN_TILE, sk)))
        assert sk % tk == 0 and k.shape[2] == n_groups * w and v.shape[2] == n_groups * vw
        assert tk <= SB_BLOCK or tk % SB_BLOCK == 0
        cast = k.dtype != BF16
        srcs.append(dict(mode=s["mode"], pos0=s["pos0"], sk=sk, tk=tk, cast=cast))
        in_specs.append(pl.BlockSpec((1, sk, w), lambda bi, g, qi: (bi, 0, g)))
        in_specs.append(pl.BlockSpec((1, sk, vw), lambda bi, g, qi: (bi, 0, g)))
        args += [k, v]
    if kind == "diff":
        in_specs += [_const_spec(extras[1].shape), _const_spec(extras[2].shape)]
        args += [extras[1], extras[2]]
    for s in srcs:
        if s["cast"]:
            scratch += [pltpu.VMEM((s["sk"], w), BF16), pltpu.VMEM((s["sk"], vw), BF16)]
    kern = functools.partial(_attn_kernel, kind=kind, srcs=srcs, ns=ns, tq=tq, q_pos0=q_pos0,
                             lam_init=lam_init)
    return pl.pallas_call(
        kern,
        grid=(b, n_groups, sq // tq),
        in_specs=in_specs,
        out_specs=pl.BlockSpec((1, tq, vw), lambda bi, g, qi: (bi, qi, g)),
        out_shape=jax.ShapeDtypeStruct((b, sq, n_groups * vw), BF16),
        scratch_shapes=scratch,
        compiler_params=_cparams(3),
        name="attn_" + kind,
    )(*args)


def _merge_kernel(x_ref, gmix_ref, wg_ref, om_ref, os_ref, od_ref, wbr_ref, wout_ref, o_ref):
    x = x_ref[...]
    d = x.shape[1]
    h = _rmsnorm(x, gmix_ref[...]).astype(BF16)
    merged = None
    for i, br_ref in enumerate((om_ref, os_ref, od_ref)):
        gate = _sigmoid(_dot(h, wg_ref[:, i * d:(i + 1) * d]))
        term = gate * _dot(br_ref[...], wbr_ref[i])
        merged = term if merged is None else merged + term
    o_ref[...] = x + _dot(merged.astype(BF16), wout_ref[...])


def _merge(x2d, o_mla, o_sb, o_diff, lw):
    t, d = x2d.shape
    tm = ROW_TILE
    row = lambda w: pl.BlockSpec((tm, w), lambda i: (i, 0))
    return pl.pallas_call(
        _merge_kernel,
        grid=(t // tm,),
        in_specs=[row(d), _const_spec(lw["gmix"].shape), _const_spec(lw["wg"].shape),
                  row(o_mla.shape[1]), row(o_sb.shape[1]), row(o_diff.shape[1]),
                  _const_spec(lw["wbr"].shape), _const_spec(lw["wout"].shape)],
        out_specs=row(d),
        out_shape=jax.ShapeDtypeStruct((t, d), F32),
        compiler_params=_cparams(1),
        name="merge",
    )(x2d, lw["gmix"], lw["wg"], o_mla, o_sb, o_diff, lw["wbr"], lw["wout"])


def _conv_rows(a, h0, h1, cw, cb):
    n = a.shape[0]
    row = lax.broadcasted_iota(jnp.int32, (n, 1), 0)
    p1 = jnp.where(row == 0, h1, pltpu.roll(a, 1, 0))
    p2 = jnp.where(row == 0, h0, jnp.where(row == 1, h1, pltpu.roll(a, 2, 0)))
    return p2 * cw[0:1] + p1 * cw[1:2] + a * cw[2:3] + cb


def _ffn_kernel(*refs, seg, d_ff, has_hist):
    it = iter(refs)
    x_ref, g_ref, wup_ref, cw_ref, cb_ref = next(it), next(it), next(it), next(it), next(it)
    hist_ref = next(it) if has_hist else None
    wdn_ref, o_ref, conv_ref = next(it), next(it), next(it)
    carry_ref = next(it) if not has_hist else None
    act_ref = next(it) if has_hist else None

    x = x_ref[...]
    tm = x.shape[0]
    h = _rmsnorm(x, g_ref[...]).astype(BF16)
    a = _dot(h, wup_ref[:, 0:d_ff])
    u = _dot(h, wup_ref[:, d_ff:2 * d_ff])
    cw, cb = cw_ref[...], cb_ref[...]

    if not has_hist:
        i = pl.program_id(0)
        tiles_per_seq = seg // tm
        first = (i % tiles_per_seq) == 0

        @pl.when(first)
        def _():
            carry_ref[...] = jnp.zeros_like(carry_ref)

        tail = carry_ref[...]
        c = _conv_rows(a, tail[6:7], tail[7:8], cw, cb)
        carry_ref[...] = a[tm - 8:tm]

        @pl.when((i % tiles_per_seq) == tiles_per_seq - 1)
        def _():
            conv_ref[0] = a[tm - 2:tm]

        act = (c * _sigmoid(c) * u).astype(BF16)
    else:
        for s in range(tm // seg):
            a_s = a[s * seg:(s + 1) * seg]
            hs = hist_ref[s]
            c = _conv_rows(a_s, hs[0:1], hs[1:2], cw, cb)
            act_ref[s * seg:(s + 1) * seg, :] = (c * _sigmoid(c) * u[s * seg:(s + 1) * seg]).astype(BF16)
            conv_ref[s] = a_s[seg - 2:seg]
        act = act_ref[...]

    o_ref[...] = x + _dot(act, wdn_ref[...])


def _ffn(x2d, lw, *, seg, hist=None):
    t, d = x2d.shape
    tm = ROW_TILE
    d_ff = lw["wdn"].shape[0]
    has_hist = hist is not None
    row = lambda w: pl.BlockSpec((tm, w), lambda i: (i, 0))
    in_specs = [row(d), _const_spec(lw["gffn"].shape), _const_spec(lw["wup"].shape),
                _const_spec(lw["cw"].shape), _const_spec(lw["cb"].shape)]
    args = [x2d, lw["gffn"], lw["wup"], lw["cw"], lw["cb"]]
    if has_hist:
        assert tm % seg == 0 and seg % 8 == 0
        nseq = tm // seg
        in_specs.append(pl.BlockSpec((nseq, 2, d_ff), lambda i: (i, 0, 0)))
        args.append(hist)
        conv_spec = pl.BlockSpec((nseq, 2, d_ff), lambda i: (i, 0, 0))
        scratch = [pltpu.VMEM((tm, d_ff), BF16)]
    else:
        assert seg % tm == 0
        tps = seg // tm
        conv_spec = pl.BlockSpec((1, 2, d_ff), lambda i: (i // tps, 0, 0))
        scratch = [pltpu.VMEM((8, d_ff), F32)]
    in_specs.append(_const_spec(lw["wdn"].shape))
    args.append(lw["wdn"])
    return pl.pallas_call(
        functools.partial(_ffn_kernel, seg=seg, d_ff=d_ff, has_hist=has_hist),
        grid=(t // tm,),
        in_specs=in_specs,
        out_specs=[row(d), conv_spec],
        out_shape=[jax.ShapeDtypeStruct((t, d), F32),
                   jax.ShapeDtypeStruct((t // seg, 2, d_ff), F32)],
        scratch_shapes=scratch,
        compiler_params=_cparams(1),
        name="ffn",
    )(*args)


def _group_matrix(groups, width):
    m = np.zeros((width, width), np.float32)
    for g in groups:
        for i in g:
            m[i, g] = 1.0 / len(g)
    return jnp.asarray(m, dtype=BF16)


def _pack_layer(w, l, dims):
    nope, rope, heads = dims["nope"], dims["rope"], dims["mla_heads"]
    q_rank, kv_rank = dims["q_rank"], dims["kv_rank"]
    sb_w, df_w, d = dims["sb_w"], dims["df_w"], dims["d_model"]
    half = rope // 2
    assert nope == 64 and rope == 32 and heads % 2 == 0 and q_rank + kv_rank + rope <= 4 * LANES
    assert sb_w % LANES == 0 and df_w % (2 * LANES) == 0
    w_in = w["w_in"][l]
    o_sb = q_rank + kv_rank + rope
    o_df = o_sb + 3 * sb_w
    o_g = o_df + 3 * df_w
    wa = jnp.pad(w_in[:, :o_sb], ((0, 0), (0, 4 * LANES - o_sb)))
    w1 = jnp.concatenate([wa, w_in[:, o_sb:o_g]], axis=1).astype(BF16)

    cols = np.full((heads // 2) * 2 * LANES, -1, np.int64)
    gain_src = np.full(cols.shape, -1, np.int64)
    for p in range(heads // 2):
        base = 2 * LANES * p
        for j, hd in enumerate((2 * p, 2 * p + 1)):
            hb = hd * (nope + rope)
            cols[base + nope * j:base + nope * (j + 1)] = hb + np.arange(nope)
            gain_src[base + nope * j:base + nope * (j + 1)] = np.arange(nope)
            x1 = base + 2 * nope + half * j
            x2 = base + 2 * nope + 2 * half + half * j
            cols[x1:x1 + half] = hb + nope + np.arange(half)
            cols[x2:x2 + half] = hb + nope + half + np.arange(half)
            gain_src[x1:x1 + half] = nope + np.arange(half)
            gain_src[x2:x2 + half] = nope + half + np.arange(half)
    valid = jnp.asarray(cols >= 0)
    wuq = jnp.where(valid[None, :], w["mla_w_uq"][l][:, np.maximum(cols, 0)], 0.0).astype(BF16)
    qg = jnp.concatenate([w["mla_qn_g"][l], w["mla_qr_g"][l]])
    gq = jnp.where(valid, qg[np.maximum(gain_src, 0)], 0.0) * ((nope + rope) ** -0.5 * LOG2E)

    groups = [list(range(0, nope)), list(range(nope, 2 * nope))]
    for j in range(2):
        groups.append(list(range(2 * nope + half * j, 2 * nope + half * (j + 1)))
                      + list(range(2 * nope + 2 * half + half * j, 2 * nope + 2 * half + half * (j + 1))))
    gm_mla = _group_matrix(groups, 2 * LANES)
    gm64 = _group_matrix([list(range(64 * j, 64 * (j + 1))) for j in range(4)], 2 * LANES)

    rep = np.zeros((rope, LANES), np.float32)
    for j in range(half):
        rep[j, [j, half + j]] = 1.0
        rep[half + j, [2 * half + j, 3 * half + j]] = 1.0

    row = lambda v: v.reshape(1, -1).astype(F32)
    return dict(
        gmix=row(w["mix_norm_g"][l]), w1=w1, gcq=row(w["mla_q_norm_g"][l]), wuq=wuq, gm_mla=gm_mla,
        gq=row(gq), gckv=row(w["mla_kv_norm_g"][l]),
        gkr=row(jnp.pad(w["mla_kr_g"][l], (0, LANES - rope))),
        gm64=gm64,
        gdq=row(jnp.tile(w["diff_qn_g"][l], df_w // 64) * (LOG2E / 8.0)),
        gdk=row(jnp.tile(w["diff_kn_g"][l], df_w // 64)),
        wukv=jnp.concatenate([w["mla_w_uk"][l], w["mla_w_uv"][l]], axis=1).astype(BF16),
        gkn=row(jnp.tile(w["mla_kn_g"][l], heads)),
        rep=jnp.asarray(rep, dtype=BF16),
        lam=w["diff_lambda"][l].astype(F32), gsub=row(w["diff_subln_g"][l]),
        wg=w_in[:, o_g:].astype(BF16),
        wbr=jnp.stack([w["w_br_mla"][l], w["w_br_sb"][l], w["w_br_diff"][l]]).astype(BF16),
        wout=w["w_out"][l].astype(BF16),
        gffn=row(w["ffn_norm_g"][l]), wup=w["ffn_w_up"][l].astype(BF16),
        cw=w["ffn_conv_w"][l].astype(F32), cb=row(w["ffn_conv_b"][l]),
        wdn=w["ffn_w_down"][l].astype(BF16),
    )


def _rope_table(pos, rope, n_rows):
    half = rope // 2
    inv = ROPE_BASE ** (-jnp.arange(half, dtype=F32) / half)
    ang = pos.astype(F32)[:, None] * inv[None, :]
    cos, sin = jnp.cos(ang), jnp.sin(ang)
    z = lambda n: jnp.zeros((pos.shape[0], n), F32)
    tab = jnp.concatenate([
        cos, cos, cos, cos, z(LANES - 4 * half),
        -sin, -sin, sin, sin, z(LANES - 4 * half),
        cos, cos, z(LANES - 2 * half),
        -sin, sin, z(LANES - 2 * half)], axis=1)
    return jnp.tile(tab, (n_rows // pos.shape[0], 1))


def _layer(x2d, lw, dims, *, batch, seq, pos0, tab, past, conv_hist, layer_idx, slopes):
    q, ckv, kr, sq, sk, sv, dq, dk, dv = _in_proj(x2d, lw, tab, dims)
    r3 = lambda a: a.reshape(batch, seq, a.shape[-1])
    kpk, mv = _kv_up(ckv, kr, lw)
    lam_init = 0.8 - 0.6 * math.exp(-0.3 * layer_idx)
    ns = 2 if past is None else 4
    if past is None:
        mla_src = [dict(k=r3(kpk), v=r3(mv), pos0=0, mode="self")]
        sb_src = [dict(k=r3(sk), v=r3(sv), pos0=0, mode="self")]
        df_src = [dict(k=r3(dk), v=r3(dv), pos0=0, mode="self")]
    else:
        c_ckv, c_kr, c_sk, c_sv, c_dk, c_dv = past
        pb, plen = c_ckv.shape[0], c_ckv.shape[1]
        ckpk, cmv = _kv_up(c_ckv.reshape(pb * plen, -1), c_kr.reshape(pb * plen, -1), lw)
        c3 = lambda a: a.reshape(pb, plen, -1)
        mla_src = [dict(k=c3(ckpk), v=c3(cmv), pos0=0, mode="full", tk=plen),
                   dict(k=r3(kpk), v=r3(mv), pos0=plen, mode="masked")]
        sb_src = [dict(k=c3(c_sk), v=c3(c_sv), pos0=0, mode="full", tk=plen),
                  dict(k=r3(sk), v=r3(sv), pos0=plen, mode="masked")]
        df_src = [dict(k=c3(c_dk), v=c3(c_dv), pos0=0, mode="full", tk=plen),
                  dict(k=r3(dk), v=r3(dv), pos0=plen, mode="masked")]
    o_mla = _attention("mla", r3(q), mla_src, ns=ns, q_pos0=pos0)
    o_sb = _attention("sb", r3(sq), sb_src, ns=ns, q_pos0=pos0)
    o_df = _attention("diff", r3(dq), df_src, ns=ns, q_pos0=pos0, extras=(slopes, lw["lam"], lw["gsub"]),
                      lam_init=lam_init)
    f2 = lambda a: a.reshape(batch * seq, a.shape[-1])
    x1 = _merge(x2d, f2(o_mla), f2(o_sb), f2(o_df), lw)
    x2, conv = _ffn(x1, lw, seg=seq, hist=conv_hist)
    return x2, (ckv, kr, sk, sv, dk, dv), conv


def kernel(x_prompt, x_sample, cache_mla_ckv, cache_mla_krope, cache_sb_k, cache_sb_v, cache_diff_k, cache_diff_v, state_ffn_conv, mix_norm_g, w_in, mla_q_norm_g, mla_w_uq, mla_kv_norm_g, mla_w_uk, mla_w_uv, mla_qn_g, mla_kn_g, mla_qr_g, mla_kr_g, diff_qn_g, diff_kn_g, diff_lambda, diff_subln_g, w_br_mla, w_br_sb, w_br_diff, w_out, ffn_norm_g, ffn_w_up, ffn_conv_w, ffn_conv_b, ffn_w_down):
    w = dict(mix_norm_g=mix_norm_g, w_in=w_in, mla_q_norm_g=mla_q_norm_g, mla_w_uq=mla_w_uq,
             mla_kv_norm_g=mla_kv_norm_g, mla_w_uk=mla_w_uk, mla_w_uv=mla_w_uv, mla_qn_g=mla_qn_g,
             mla_kn_g=mla_kn_g, mla_qr_g=mla_qr_g, mla_kr_g=mla_kr_g, diff_qn_g=diff_qn_g,
             diff_kn_g=diff_kn_g, diff_lambda=diff_lambda, diff_subln_g=diff_subln_g,
             w_br_mla=w_br_mla, w_br_sb=w_br_sb, w_br_diff=w_br_diff, w_out=w_out,
             ffn_norm_g=ffn_norm_g, ffn_w_up=ffn_w_up, ffn_conv_w=ffn_conv_w,
             ffn_conv_b=ffn_conv_b, ffn_w_down=ffn_w_down)
    depth = w_in.shape[0]
    bp, sp, d = x_prompt.shape
    bs, ss, _ = x_sample.shape
    past_len = cache_mla_ckv.shape[2]
    sb_heads, sb_dim = cache_sb_k.shape[3], cache_sb_k.shape[4]
    df_heads, df_dim = cache_diff_k.shape[3], cache_diff_k.shape[5]
    assert sb_dim == 64 and df_dim == 64 and ffn_conv_w.shape[1] == 3
    nope, rope = mla_qn_g.shape[1], mla_qr_g.shape[1]
    dims = dict(d_model=d, q_rank=mla_q_norm_g.shape[1], kv_rank=mla_kv_norm_g.shape[1],
                nope=nope, rope=rope, mla_heads=mla_w_uk.shape[2] // nope,
                sb_w=sb_heads * sb_dim, df_w=2 * df_heads * df_dim)
    d_ff = ffn_conv_b.shape[1]

    tab_p = _rope_table(jnp.arange(sp, dtype=jnp.int32), rope, max(sp, ROW_TILE))
    tab_s = _rope_table(past_len + jnp.arange(ss, dtype=jnp.int32), rope, max(ss, ROW_TILE))
    slopes = 2.0 ** (-8.0 * jnp.arange(1, df_heads + 1, dtype=F32) / df_heads) * LOG2E

    xp = x_prompt.reshape(bp * sp, d)
    xs = x_sample.reshape(bs * ss, d)
    rows_p, rows_s, conv_p, conv_s = [], [], [], []
    for l in range(depth):
        lw = _pack_layer(w, l, dims)
        xp, rp, cp = _layer(xp, lw, dims, batch=bp, seq=sp, pos0=0, tab=tab_p, past=None,
                            conv_hist=None, layer_idx=l, slopes=slopes)
        past = (cache_mla_ckv[l], cache_mla_krope[l],
                cache_sb_k[l].reshape(bs, past_len, -1), cache_sb_v[l].reshape(bs, past_len, -1),
                cache_diff_k[l].reshape(bs, past_len, -1), cache_diff_v[l].reshape(bs, past_len, -1))
        xs, rs, cs = _layer(xs, lw, dims, batch=bs, seq=ss, pos0=past_len, tab=tab_s, past=past,
                            conv_hist=state_ffn_conv[l], layer_idx=l, slopes=slopes)
        rows_p.append(rp)
        rows_s.append(rs)
        conv_p.append(cp)
        conv_s.append(cs)

    def stack(rows, i, b, s, tail):
        return jnp.stack([r[i] for r in rows]).reshape((depth, b, s) + tail)

    def group(rows, b, s):
        return (stack(rows, 0, b, s, (dims["kv_rank"],)), stack(rows, 1, b, s, (rope,)),
                stack(rows, 2, b, s, (sb_heads, sb_dim)), stack(rows, 3, b, s, (sb_heads, sb_dim)),
                stack(rows, 4, b, s, (df_heads, 2, df_dim)), stack(rows, 5, b, s, (df_heads, 2 * df_dim)))

    return ((xp.reshape(bp, sp, d), xs.reshape(bs, ss, d)) + group(rows_p, bp, sp)
            + (jnp.stack(conv_p),) + group(rows_s, bs, ss) + (jnp.stack(conv_s),))
```

```python
import functools
import math

import numpy as np
import jax
import jax.numpy as jnp
from jax import lax
from jax.experimental import pallas as pl
from jax.experimental.pallas import tpu as pltpu

F32 = jnp.float32
BF16 = jnp.bfloat16

EPS = 1e-6
CHUNK = 64
CHUNK_SHIFT = CHUNK.bit_length() - 1
assert 1 << CHUNK_SHIFT == CHUNK
ROPE_BASE = 10000.0
NEG_BIG = -1e30
HEAD = 64

LANES = 128
ROW_TILE = 512
KV_ROW_TILE = 1024
ATTN_TILE = 512
SB_BLOCK = 256
LOG2E = math.log2(math.e)
VMEM_LIMIT = 52 * 1024 * 1024


def _cparams(n_axes):
    return pltpu.CompilerParams(dimension_semantics=("arbitrary",) * n_axes,
                                vmem_limit_bytes=VMEM_LIMIT)


def _const_spec(shape):
    nd = len(shape)
    return pl.BlockSpec(shape, lambda *_: (0,) * nd, pipeline_mode=pl.Buffered(1))


def _dot(a, b):
    return jnp.dot(a, b, preferred_element_type=F32)


def _dot_nt(a, b):
    return lax.dot_general(a, b, (((1,), (1,)), ((), ())), preferred_element_type=F32)


def _rmsnorm(x, g):
    return x * lax.rsqrt(jnp.mean(x * x, axis=-1, keepdims=True) + EPS) * g


def _group_mean_sq(x, gmat):
    sq = x * x
    hi = sq.astype(BF16)
    lo = (sq - hi.astype(F32)).astype(BF16)
    return _dot(hi, gmat) + _dot(lo, gmat)


def _row_group_rmsnorm(xt, gcol):
    r, n = xt.shape
    x3 = xt.reshape(r // HEAD, HEAD, n)
    y3 = x3 * lax.rsqrt(jnp.mean(x3 * x3, axis=1, keepdims=True) + EPS)
    return y3.reshape(r, n) * gcol


def _sigmoid(x):
    return 1.0 / (1.0 + jnp.exp(-x))


def _mla_query(cq, gcq_ref, wuq_ref, gm_mla_ref, gq_ref, cos_q, sin_q, q_ref, half):
    tm = cq.shape[0]
    cqn = _rmsnorm(cq, gcq_ref[...]).astype(BF16)
    q = _dot(cqn, wuq_ref[...])
    lane = lax.broadcasted_iota(jnp.int32, (tm, LANES), 1)
    for p in range(q.shape[1] // (2 * LANES)):
        lo = 2 * LANES * p
        blk = q[:, lo:lo + 2 * LANES]
        y = blk * lax.rsqrt(_group_mean_sq(blk, gm_mla_ref[...]) + EPS) * gq_ref[:, lo:lo + 2 * LANES]
        rp = y[:, LANES:]
        partner = jnp.where(lane < 2 * half, pltpu.roll(rp, LANES - 2 * half, 1),
                            pltpu.roll(rp, 2 * half, 1))
        q_ref[:, lo:lo + LANES] = y[:, :LANES].astype(BF16)
        q_ref[:, lo + LANES:lo + 2 * LANES] = (rp * cos_q + partner * sin_q).astype(BF16)


def _diff_query(zq, gm64_ref, gdq_ref, dq_ref):
    for s in range(zq.shape[1] // (2 * LANES)):
        lo = 2 * LANES * s
        blk = zq[:, lo:lo + 2 * LANES]
        r = lax.rsqrt(_group_mean_sq(blk, gm64_ref[...]) + EPS)
        dq_ref[:, lo:lo + 2 * LANES] = (blk * r * gdq_ref[:, lo:lo + 2 * LANES]).astype(BF16)


def _in_proj_kernel(x_ref, gmix_ref, w_ref, gcq_ref, wuq_ref, gm_mla_ref, gq_ref, gckv_ref,
                    gkr_ref, tab_ref, gm64_ref, gdq_ref, gdk_ref,
                    q_ref, ckv_ref, kr_ref, sq_ref, sk_ref, sv_ref, dq_ref, dk_ref, dv_ref,
                    *, q_rank, kv_rank, rope, sb_w, df_w):
    tm = x_ref.shape[0]
    half = rope // 2
    h = _rmsnorm(x_ref[...], gmix_ref[...]).astype(BF16)
    tab = tab_ref[...]
    cos_k, sin_k = tab[:, 2 * LANES:3 * LANES], tab[:, 3 * LANES:4 * LANES]
    lane = lax.broadcasted_iota(jnp.int32, (tm, LANES), 1)

    za = _dot(h, w_ref[:, 0:4 * LANES])
    _mla_query(za[:, 0:q_rank], gcq_ref, wuq_ref, gm_mla_ref, gq_ref, tab[:, 0:LANES],
               tab[:, LANES:2 * LANES], q_ref, half)
    ckv_ref[...] = _rmsnorm(za[:, q_rank:q_rank + kv_rank], gckv_ref[...])

    kr = za[:, q_rank + kv_rank:4 * LANES]
    ms = jnp.sum(kr * kr, axis=-1, keepdims=True) * (1.0 / rope)
    krn = kr * lax.rsqrt(ms + EPS) * gkr_ref[...]
    partner = jnp.where(lane < half, pltpu.roll(krn, LANES - half, 1), pltpu.roll(krn, half, 1))
    kr_ref[...] = (krn * cos_k + partner * sin_k)[:, :rope]

    o = 4 * LANES
    zs = _dot(h, w_ref[:, o:o + 3 * sb_w])
    sq_ref[...] = (zs[:, 0:sb_w] * (LOG2E / 8.0)).astype(BF16)
    sk_ref[...] = zs[:, sb_w:2 * sb_w]
    sv_ref[...] = zs[:, 2 * sb_w:3 * sb_w]

    o = o + 3 * sb_w
    zd = _dot(h, w_ref[:, o:o + 3 * df_w])
    _diff_query(zd[:, 0:df_w], gm64_ref, gdq_ref, dq_ref)
    for s in range(df_w // (2 * LANES)):
        lo = 2 * LANES * s
        blk = zd[:, df_w + lo:df_w + lo + 2 * LANES]
        r = lax.rsqrt(_group_mean_sq(blk, gm64_ref[...]) + EPS)
        dk_ref[:, lo:lo + 2 * LANES] = blk * r * gdk_ref[:, lo:lo + 2 * LANES]
    dv_ref[...] = zd[:, 2 * df_w:3 * df_w]


def _in_proj(x2d, lw, tab, dims):
    t, d = x2d.shape
    tm = ROW_TILE
    assert t % tm == 0 and tab.shape[0] % tm == 0
    n_tab = tab.shape[0] // tm
    row = lambda w: pl.BlockSpec((tm, w), lambda i: (i, 0))
    consts = [lw["gmix"], lw["w1"], lw["gcq"], lw["wuq"], lw["gm_mla"], lw["gq"], lw["gckv"], lw["gkr"]]
    consts2 = [lw["gm64"], lw["gdq"], lw["gdk"]]
    sb_w, df_w = dims["sb_w"], dims["df_w"]
    out_shape = [
        jax.ShapeDtypeStruct((t, lw["wuq"].shape[1]), BF16),
        jax.ShapeDtypeStruct((t, dims["kv_rank"]), F32),
        jax.ShapeDtypeStruct((t, dims["rope"]), F32),
        jax.ShapeDtypeStruct((t, sb_w), BF16),
        jax.ShapeDtypeStruct((t, sb_w), F32),
        jax.ShapeDtypeStruct((t, sb_w), F32),
        jax.ShapeDtypeStruct((t, df_w), BF16),
        jax.ShapeDtypeStruct((t, df_w), F32),
        jax.ShapeDtypeStruct((t, df_w), F32),
    ]
    kern = functools.partial(_in_proj_kernel, q_rank=dims["q_rank"], kv_rank=dims["kv_rank"],
                             rope=dims["rope"], sb_w=sb_w, df_w=df_w)
    return pl.pallas_call(
        kern,
        grid=(t // tm,),
        in_specs=([row(d)] + [_const_spec(c.shape) for c in consts]
                  + [pl.BlockSpec((tm, tab.shape[1]), lambda i: (i % n_tab, 0))]
                  + [_const_spec(c.shape) for c in consts2]),
        out_specs=[row(s.shape[1]) for s in out_shape],
        out_shape=out_shape,
        compiler_params=_cparams(1),
        name="in_proj",
    )(x2d, *consts, tab, *consts2)


def _in_proj_t_kernel(*refs, q_rank, kv_rank, rope, sb_w, df_w, n_alias):
    (x_ref, gmix_ref, wn_ref, wt_ref, gcq_ref, wuq_ref, gm_mla_ref, gq_ref, gckv_ref, gkr_ref,
     tab_ref, tabt_ref, gm64_ref, gdq_ref, gdk_ref) = refs[:15]
    (q_ref, sq_ref, dq_ref, dv_ref, ckv_ref, krt_ref, skt_ref, svt_ref, dkt_ref) = refs[15 + n_alias:]
    half = rope // 2
    h = _rmsnorm(x_ref[...], gmix_ref[...]).astype(BF16)
    tab = tab_ref[...]

    o = q_rank + kv_rank
    za = _dot(h, wn_ref[:, 0:o])
    _mla_query(za[:, 0:q_rank], gcq_ref, wuq_ref, gm_mla_ref, gq_ref, tab[:, 0:LANES],
               tab[:, LANES:2 * LANES], q_ref, half)
    ckv_ref[0, 0] = _rmsnorm(za[:, q_rank:o], gckv_ref[...])

    sq_ref[...] = (_dot(h, wn_ref[:, o:o + sb_w]) * (LOG2E / 8.0)).astype(BF16)
    o = o + sb_w
    zd = _dot(h, wn_ref[:, o:o + 2 * df_w])
    _diff_query(zd[:, 0:df_w], gm64_ref, gdq_ref, dq_ref)
    dv_ref[...] = zd[:, df_w:2 * df_w]

    skt_ref[0, 0] = _dot_nt(wt_ref[0:sb_w, :], h)
    svt_ref[0, 0] = _dot_nt(wt_ref[sb_w:2 * sb_w, :], h)
    o = 2 * sb_w
    dkt_ref[0, 0] = _row_group_rmsnorm(_dot_nt(wt_ref[o:o + df_w, :], h), gdk_ref[...])
    o = o + df_w
    krt = _dot_nt(wt_ref[o:o + rope, :], h)
    krn = krt * lax.rsqrt(jnp.mean(krt * krt, axis=0, keepdims=True) + EPS) * gkr_ref[...]
    x1, x2 = krn[:half], krn[half:]
    cos_t, sin_t = tabt_ref[0:half, :], tabt_ref[half:rope, :]
    krt_ref[0, 0] = jnp.concatenate([x1 * cos_t - x2 * sin_t, x1 * sin_t + x2 * cos_t], axis=0)


def _in_proj_t(x2d, lw, tab, tabt, dims, *, batch, seq, layer, depth, prev):
    t, d = x2d.shape
    tm = ROW_TILE
    assert seq % tm == 0 and t == batch * seq
    nst = seq // tm
    row = lambda w: pl.BlockSpec((tm, w), lambda i: (i, 0))
    consts = [lw["gmix"], lw["wn"], lw["wt"], lw["gcq"], lw["wuq"], lw["gm_mla"], lw["gq"], lw["gckv"],
              lw["gkr_col"]]
    consts2 = [lw["gm64"], lw["gdq"], lw["gdk_col"]]
    sb_w, df_w, rope, kvr = dims["sb_w"], dims["df_w"], dims["rope"], dims["kv_rank"]
    n_alias = 0 if prev is None else len(prev)
    in_specs = ([row(d)] + [_const_spec(c.shape) for c in consts]
                + [pl.BlockSpec((tm, tab.shape[1]), lambda i: (i % nst, 0)),
                   pl.BlockSpec((rope, tm), lambda i: (0, i % nst))]
                + [_const_spec(c.shape) for c in consts2]
                + [pl.BlockSpec(memory_space=pl.ANY)] * n_alias)
    n_plain = len(in_specs) - n_alias
    tok = lambda w: pl.BlockSpec((1, 1, tm, w), lambda i: (layer, i // nst, i % nst, 0))
    feat = lambda r: pl.BlockSpec((1, 1, r, tm), lambda i: (layer, i // nst, 0, i % nst))
    out_shape = [
        jax.ShapeDtypeStruct((t, lw["wuq"].shape[1]), BF16),
        jax.ShapeDtypeStruct((t, sb_w), BF16),
        jax.ShapeDtypeStruct((t, df_w), BF16),
        jax.ShapeDtypeStruct((t, df_w), F32),
        jax.ShapeDtypeStruct((depth, batch, seq, kvr), F32),
        jax.ShapeDtypeStruct((depth, batch, rope, seq), F32),
        jax.ShapeDtypeStruct((depth, batch, sb_w, seq), F32),
        jax.ShapeDtypeStruct((depth, batch, sb_w, seq), F32),
        jax.ShapeDtypeStruct((depth, batch, df_w, seq), F32),
    ]
    out_specs = [row(out_shape[0].shape[1]), row(sb_w), row(df_w), row(df_w),
                 tok(kvr), feat(rope), feat(sb_w), feat(sb_w), feat(df_w)]
    kern = functools.partial(_in_proj_t_kernel, q_rank=dims["q_rank"], kv_rank=kvr, rope=rope,
                             sb_w=sb_w, df_w=df_w, n_alias=n_alias)
    return pl.pallas_call(
        kern,
        grid=(t // tm,),
        in_specs=in_specs,
        out_specs=out_specs,
        out_shape=out_shape,
        input_output_aliases={n_plain + j: 4 + j for j in range(n_alias)},
        compiler_params=_cparams(1),
        name="in_proj_t",
    )(x2d, *consts, tab, tabt, *consts2, *(prev or ()))


def _kv_up_kernel(ckv_ref, kr_ref, wukv_ref, gm64_ref, gkn_ref, rep_ref, k_ref, v_ref, *, kn_w):
    c = ckv_ref[...].astype(BF16)
    kv = _dot(c, wukv_ref[...])
    krrep = _dot(kr_ref[...].astype(BF16), rep_ref[...]).astype(BF16)
    for b in range(kn_w // (2 * LANES)):
        s = 2 * LANES * b
        blk = kv[:, s:s + 2 * LANES]
        r = lax.rsqrt(_group_mean_sq(blk, gm64_ref[...]) + EPS)
        kn = (blk * r * gkn_ref[:, s:s + 2 * LANES]).astype(BF16)
        for j in range(2):
            p = 2 * b + j
            k_ref[:, 2 * LANES * p:2 * LANES * p + LANES] = kn[:, LANES * j:LANES * (j + 1)]
            k_ref[:, 2 * LANES * p + LANES:2 * LANES * (p + 1)] = krrep
    v_ref[...] = kv[:, kn_w:].astype(BF16)


def _kv_up(ckv2d, kr2d, lw):
    r, kvr = ckv2d.shape
    tr = min(KV_ROW_TILE, r)
    assert r % tr == 0
    kn_w = lw["gkn"].shape[1]
    consts = [lw["wukv"], lw["gm64"], lw["gkn"], lw["rep"]]
    out_shape = [jax.ShapeDtypeStruct((r, 2 * kn_w), BF16),
                 jax.ShapeDtypeStruct((r, lw["wukv"].shape[1] - kn_w), BF16)]
    return pl.pallas_call(
        functools.partial(_kv_up_kernel, kn_w=kn_w),
        grid=(r // tr,),
        in_specs=[pl.BlockSpec((tr, kvr), lambda i: (i, 0)),
                  pl.BlockSpec((tr, kr2d.shape[1]), lambda i: (i, 0))]
                 + [_const_spec(c.shape) for c in consts],
        out_specs=[pl.BlockSpec((tr, s.shape[1]), lambda i: (i, 0)) for s in out_shape],
        out_shape=out_shape,
        compiler_params=_cparams(1),
        name="mla_kv_up",
    )(ckv2d, kr2d, *consts)


def _kv_up_t_kernel(ckv_ref, krt_ref, wukt_ref, wuvt_ref, gkn_ref, kt_ref, vt_ref, *, rope):
    half = rope // 2
    c = ckv_ref[0, 0].astype(BF16)
    n = c.shape[0]
    knt = _row_group_rmsnorm(_dot_nt(wukt_ref[...], c), gkn_ref[...]).astype(BF16)
    kr = krt_ref[0, 0].astype(BF16)
    x1, x2 = kr[:half], kr[half:]
    krrep = jnp.concatenate([x1, x1, x2, x2, jnp.zeros((LANES - 4 * half, n), BF16)], axis=0)
    for p in range(knt.shape[0] // LANES):
        kt_ref[0, 2 * LANES * p:2 * LANES * p + LANES, :] = knt[LANES * p:LANES * (p + 1)]
        kt_ref[0, 2 * LANES * p + LANES:2 * LANES * (p + 1), :] = krrep
    vt_ref[0] = _dot_nt(wuvt_ref[...], c).astype(BF16)


def _kv_up_t(ckv4, krt4, layer, lw, rope):
    _, b, s, kvr = ckv4.shape
    tr = min(KV_ROW_TILE, s)
    assert s % tr == 0
    consts = [lw["wukt"], lw["wuvt"], lw["gkn_col"]]
    kn_w, v_w = lw["wukt"].shape[0], lw["wuvt"].shape[0]
    out_shape = [jax.ShapeDtypeStruct((b, 2 * kn_w, s), BF16), jax.ShapeDtypeStruct((b, v_w, s), BF16)]
    return pl.pallas_call(
        functools.partial(_kv_up_t_kernel, rope=rope),
        grid=(b, s // tr),
        in_specs=[pl.BlockSpec((1, 1, tr, kvr), lambda bi, i: (layer, bi, i, 0)),
                  pl.BlockSpec((1, 1, rope, tr), lambda bi, i: (layer, bi, 0, i))]
                 + [_const_spec(c.shape) for c in consts],
        out_specs=[pl.BlockSpec((1, o.shape[1], tr), lambda bi, i: (bi, 0, i)) for o in out_shape],
        out_shape=out_shape,
        compiler_params=_cparams(2),
        name="mla_kv_up_t",
    )(ckv4, krt4, *consts)


def _stream_lane_masks(kind, ns, width):
    lane = lax.broadcasted_iota(jnp.int32, (1, width), 1)
    masks = []
    for i in range(ns):
        if kind == "mla":
            r = lane - 2 * LANES * (i // 2)
            j = i % 2
            masks.append(((r >= HEAD * j) & (r < HEAD * (j + 1)))
                         | ((r >= 128 + 16 * j) & (r < 128 + 16 * (j + 1)))
                         | ((r >= 160 + 16 * j) & (r < 160 + 16 * (j + 1))))
        else:
            masks.append((lane >= HEAD * i) & (lane < HEAD * (i + 1)))
    return masks


def _attn_kernel(*refs, kind, srcs, ns, tq, q_pos0, lam_init):
    it = iter(refs)
    slopes_ref = next(it) if kind == "diff" else None
    q_ref = next(it)
    kv_refs = [(next(it), next(it)) for _ in srcs]
    if kind == "diff":
        lam_ref, gsub_ref = next(it), next(it)
    o_ref = next(it)
    scr = [(next(it), next(it)) if s["cast"] else None for s in srcs]

    grp = pl.program_id(1)
    qi = pl.program_id(2)
    rows = ns * tq
    vw = HEAD * ns

    for s, sc, (k_ref, v_ref) in zip(srcs, scr, kv_refs):
        if sc is not None:
            @pl.when(qi == 0)
            def _():
                sc[0][...] = k_ref[0, 0].astype(BF16)
                sc[1][...] = v_ref[0, 0].astype(BF16)

    q = q_ref[0]
    zero = jnp.zeros_like(q)
    qs = jnp.concatenate([jnp.where(m, q, zero) for m in _stream_lane_masks(kind, ns, q.shape[1])],
                         axis=0)
    q_start = q_pos0 + qi * tq
    qpos = jnp.concatenate([q_start + lax.broadcasted_iota(jnp.int32, (tq, 1), 0)] * ns, axis=0)
    if kind == "diff":
        slope = jnp.concatenate(
            [jnp.full((tq, 1), slopes_ref[grp * (ns // 2) + i // 2], F32) for i in range(ns)], axis=0)

    if kind == "sb":
        init = (jnp.zeros((rows, 1), F32), jnp.zeros((rows, vw), F32))
    else:
        init = (jnp.full((rows, 1), NEG_BIG, F32), jnp.zeros((rows, 1), F32),
                jnp.zeros((rows, vw), F32))

    def tile_step(st, k_t, v_t, k_start, src, masked, upper):
        tk = src["tk"]
        kpos = k_start + lax.broadcasted_iota(jnp.int32, (1, tk), 1)
        s = _dot(qs, k_t) if src["kt"] else _dot_nt(qs, k_t)
        pv = (lambda p: _dot_nt(p, v_t)) if src["vt"] else (lambda p: _dot(p, v_t))
        if masked:
            valid = (kpos < qpos) if kind == "sb" else ((kpos >> CHUNK_SHIFT) <= (qpos >> CHUNK_SHIFT))
        if kind == "sb":
            c, acc = st
            lp = jnp.log(1.0 + jnp.exp2(-jnp.abs(s))) * LOG2E
            ls = jnp.minimum(s, 0.0) - lp
            l1 = ls - s
            if masked:
                l1 = jnp.where(valid, l1, 0.0)
            hi = l1.astype(BF16)
            lo = (l1 - hi.astype(F32)).astype(BF16)
            cb = upper.shape[0]
            nb = tk // cb
            if nb > 1:
                hi = jnp.concatenate([hi[:, cb * j:cb * (j + 1)] for j in range(nb)], axis=0)
                lo = jnp.concatenate([lo[:, cb * j:cb * (j + 1)] for j in range(nb)], axis=0)
            suffix = _dot(hi, upper) + _dot(lo, upper)
            parts = [None] * nb
            for j in reversed(range(nb)):
                parts[j] = jnp.exp2(ls[:, cb * j:cb * (j + 1)] + suffix[rows * j:rows * (j + 1)] + c)
                c = c + jnp.sum(l1[:, cb * j:cb * (j + 1)], axis=-1, keepdims=True)
            a = parts[0] if nb == 1 else jnp.concatenate(parts, axis=1)
            if masked:
                a = jnp.where(valid, a, 0.0)
            return (c, acc + pv(a.astype(BF16)))
        m, l, acc = st
        if kind == "diff":
            dist = qpos - kpos
            s = s - slope * (jnp.abs(dist) if masked else dist).astype(F32)
        if masked:
            s = jnp.where(valid, s, -jnp.inf)
        m_new = jnp.maximum(m, jnp.max(s, axis=-1, keepdims=True))
        alpha = jnp.exp2(m - m_new)
        p = jnp.exp2(s - m_new)
        l = alpha * l + jnp.sum(p, axis=-1, keepdims=True)
        return (m_new, l, alpha * acc + pv(p.astype(BF16)))

    def upper_tri(tk):
        if kind != "sb":
            return None
        cb = min(tk, SB_BLOCK)
        r = lax.broadcasted_iota(jnp.int32, (cb, cb), 0)
        c = lax.broadcasted_iota(jnp.int32, (cb, cb), 1)
        return jnp.where(r > c, 1.0, 0.0).astype(BF16)

    st = init
    for s, sc, (k_ref, v_ref) in reversed(list(zip(srcs, scr, kv_refs))):
        tk = s["tk"]
        upper = upper_tri(tk)

        def load(k_off, _s=s, _sc=sc, _k=k_ref, _v=v_ref):
            def one(ref4, ref2, transposed):
                win = pl.ds(k_off, _s["tk"])
                if ref2 is not None:
                    return ref2[:, win] if transposed else ref2[win, :]
                return ref4[0, 0, :, win] if transposed else ref4[0, 0, win, :]
            return (one(_k, _sc and _sc[0], _s["kt"]), one(_v, _sc and _sc[1], _s["vt"]))

        if s["mode"] == "self":
            k_t, v_t = load(pl.multiple_of(qi * tk, tk))
            st = tile_step(st, k_t, v_t, s["pos0"] + qi * tk, s, True, upper)
            n_full = qi
        elif s["mode"] == "masked":
            k_t, v_t = load(0)
            st = tile_step(st, k_t, v_t, s["pos0"], s, True, upper)
            continue
        else:
            n_full = s["sk"] // tk

        def body(i, st_, _s=s, _load=load, _n=n_full, _upper=upper):
            kj = _n - 1 - i
            k_t, v_t = _load(pl.multiple_of(kj * _s["tk"], _s["tk"]))
            return tile_step(st_, k_t, v_t, _s["pos0"] + kj * _s["tk"], _s, False, _upper)

        if isinstance(n_full, int) and n_full <= 2:
            for i in range(n_full):
                st = body(i, st)
        else:
            st = lax.fori_loop(0, n_full, body, st)

    acc = st[-1]
    if kind == "diff":
        l = st[1]
        lv = lam_ref[...]
        lam = (jnp.exp(jnp.sum(lv[0:1] * lv[1:2], axis=-1, keepdims=True))
               - jnp.exp(jnp.sum(lv[2:3] * lv[3:4], axis=-1, keepdims=True)) + lam_init)
        for hh in range(ns // 2):
            lanes = slice(2 * HEAD * hh, 2 * HEAD * (hh + 1))
            r0, r1 = slice(tq * 2 * hh, tq * (2 * hh + 1)), slice(tq * (2 * hh + 1), tq * (2 * hh + 2))
            o = acc[r0, lanes] / l[r0] - lam * (acc[r1, lanes] / l[r1])
            o_ref[0, :, lanes] = (_rmsnorm(o, gsub_ref[...]) * (1.0 - lam_init)).astype(BF16)
    else:
        lane = lax.broadcasted_iota(jnp.int32, (tq, vw), 1)
        o = jnp.zeros((tq, vw), F32)
        for i in range(ns):
            blk = acc[tq * i:tq * (i + 1)]
            if kind == "mla":
                blk = blk / st[1][tq * i:tq * (i + 1)]
            o = jnp.where((lane >= HEAD * i) & (lane < HEAD * (i + 1)), blk, o)
        o_ref[0] = o.astype(BF16)


def _attention(kind, q3, sources, *, ns, q_pos0, extras=(), lam_init=0.0):
    b, sq, qtot = q3.shape
    w = (2 * HEAD if kind == "mla" else HEAD) * ns
    vw = HEAD * ns
    n_groups = qtot // w
    tq = min(ATTN_TILE, sq)
    assert sq % tq == 0 and qtot % w == 0
    srcs, in_specs, args, scratch = [], [], [], []
    if kind == "diff":
        in_specs.append(pl.BlockSpec(memory_space=pltpu.SMEM))
        args.append(extras[0])
    in_specs.append(pl.BlockSpec((1, tq, w), lambda bi, g, qi: (bi, qi, g)))
    args.append(q3)

    def kv_spec(arr, width, transposed, layer):
        if transposed:
            assert arr.shape[2] == n_groups * width
            return pl.BlockSpec((1, 1, width, arr.shape[3]), lambda bi, g, qi: (layer, bi, g, 0))
        assert arr.shape[3] == n_groups * width
        return pl.BlockSpec((1, 1, arr.shape[2], width), lambda bi, g, qi: (layer, bi, 0, g))

    for s in sources:
        k, v, kt, vt = s["k"], s["v"], s["kt"], s["vt"]
        sk = k.shape[3] if kt else k.shape[2]
        tk = tq if s["mode"] == "self" else (sk if s["mode"] == "masked" else s.get("tk", min(ATTN_TILE, sk)))
        assert sk % tk == 0 and (tk <= SB_BLOCK or tk % SB_BLOCK == 0)
        cast = k.dtype != BF16
        assert cast == (v.dtype != BF16)
        srcs.append(dict(mode=s["mode"], pos0=s["pos0"], sk=sk, tk=tk, cast=cast, kt=kt, vt=vt))
        in_specs += [kv_spec(k, w, kt, s["layer"]), kv_spec(v, vw, vt, s.get("vlayer", s["layer"]))]
        args += [k, v]
        if cast:
            scratch += [pltpu.VMEM((w, sk) if kt else (sk, w), BF16),
                        pltpu.VMEM((vw, sk) if vt else (sk, vw), BF16)]
    if kind == "diff":
        in_specs += [_const_spec(extras[1].shape), _const_spec(extras[2].shape)]
        args += [extras[1], extras[2]]
    kern = functools.partial(_attn_kernel, kind=kind, srcs=srcs, ns=ns, tq=tq, q_pos0=q_pos0,
                             lam_init=lam_init)
    return pl.pallas_call(
        kern,
        grid=(b, n_groups, sq // tq),
        in_specs=in_specs,
        out_specs=pl.BlockSpec((1, tq, vw), lambda bi, g, qi: (bi, qi, g)),
        out_shape=jax.ShapeDtypeStruct((b, sq, n_groups * vw), BF16),
        scratch_shapes=scratch,
        compiler_params=_cparams(3),
        name="attn_" + kind,
    )(*args)


def _merge_kernel(x_ref, gmix_ref, wg_ref, om_ref, os_ref, od_ref, wbr_ref, wout_ref, o_ref):
    x = x_ref[...]
    d = x.shape[1]
    h = _rmsnorm(x, gmix_ref[...]).astype(BF16)
    merged = None
    for i, br_ref in enumerate((om_ref, os_ref, od_ref)):
        gate = _sigmoid(_dot(h, wg_ref[:, i * d:(i + 1) * d]))
        term = gate * _dot(br_ref[...], wbr_ref[i])
        merged = term if merged is None else merged + term
    o_ref[...] = x + _dot(merged.astype(BF16), wout_ref[...])


def _merge(x2d, o_mla, o_sb, o_diff, lw):
    t, d = x2d.shape
    tm = ROW_TILE
    row = lambda w: pl.BlockSpec((tm, w), lambda i: (i, 0))
    return pl.pallas_call(
        _merge_kernel,
        grid=(t // tm,),
        in_specs=[row(d), _const_spec(lw["gmix"].shape), _const_spec(lw["wg"].shape),
                  row(o_mla.shape[1]), row(o_sb.shape[1]), row(o_diff.shape[1]),
                  _const_spec(lw["wbr"].shape), _const_spec(lw["wout"].shape)],
        out_specs=row(d),
        out_shape=jax.ShapeDtypeStruct((t, d), F32),
        compiler_params=_cparams(1),
        name="merge",
    )(x2d, lw["gmix"], lw["wg"], o_mla, o_sb, o_diff, lw["wbr"], lw["wout"])


def _conv_rows(a, h0, h1, cw, cb):
    n = a.shape[0]
    row = lax.broadcasted_iota(jnp.int32, (n, 1), 0)
    p1 = jnp.where(row == 0, h1, pltpu.roll(a, 1, 0))
    p2 = jnp.where(row == 0, h0, jnp.where(row == 1, h1, pltpu.roll(a, 2, 0)))
    return p2 * cw[0:1] + p1 * cw[1:2] + a * cw[2:3] + cb


def _ffn_kernel(*refs, seg, d_ff, has_hist):
    it = iter(refs)
    x_ref, g_ref, wup_ref, cw_ref, cb_ref = next(it), next(it), next(it), next(it), next(it)
    hist_ref = next(it) if has_hist else None
    wdn_ref, o_ref, conv_ref = next(it), next(it), next(it)
    carry_ref = next(it) if not has_hist else None
    act_ref = next(it) if has_hist else None

    x = x_ref[...]
    tm = x.shape[0]
    h = _rmsnorm(x, g_ref[...]).astype(BF16)
    a = _dot(h, wup_ref[:, 0:d_ff])
    u = _dot(h, wup_ref[:, d_ff:2 * d_ff])
    cw, cb = cw_ref[...], cb_ref[...]

    if not has_hist:
        i = pl.program_id(0)
        tiles_per_seq = seg // tm
        first = (i % tiles_per_seq) == 0

        @pl.when(first)
        def _():
            carry_ref[...] = jnp.zeros_like(carry_ref)

        tail = carry_ref[...]
        c = _conv_rows(a, tail[6:7], tail[7:8], cw, cb)
        carry_ref[...] = a[tm - 8:tm]

        @pl.when((i % tiles_per_seq) == tiles_per_seq - 1)
        def _():
            conv_ref[0] = a[tm - 2:tm]

        act = (c * _sigmoid(c) * u).astype(BF16)
    else:
        for s in range(tm // seg):
            a_s = a[s * seg:(s + 1) * seg]
            hs = hist_ref[0, s]
            c = _conv_rows(a_s, hs[0:1], hs[1:2], cw, cb)
            act_ref[s * seg:(s + 1) * seg, :] = (c * _sigmoid(c) * u[s * seg:(s + 1) * seg]).astype(BF16)
            conv_ref[s] = a_s[seg - 2:seg]
        act = act_ref[...]

    o_ref[...] = x + _dot(act, wdn_ref[...])


def _ffn(x2d, lw, *, seg, hist=None, layer=0):
    t, d = x2d.shape
    tm = ROW_TILE
    d_ff = lw["wdn"].shape[0]
    has_hist = hist is not None
    row = lambda w: pl.BlockSpec((tm, w), lambda i: (i, 0))
    in_specs = [row(d), _const_spec(lw["gffn"].shape), _const_spec(lw["wup"].shape),
                _const_spec(lw["cw"].shape), _const_spec(lw["cb"].shape)]
    args = [x2d, lw["gffn"], lw["wup"], lw["cw"], lw["cb"]]
    if has_hist:
        assert tm % seg == 0 and seg % 8 == 0
        nseq = tm // seg
        in_specs.append(pl.BlockSpec((1, nseq, 2, d_ff), lambda i: (layer, i, 0, 0)))
        args.append(hist)
        conv_spec = pl.BlockSpec((nseq, 2, d_ff), lambda i: (i, 0, 0))
        scratch = [pltpu.VMEM((tm, d_ff), BF16)]
    else:
        assert seg % tm == 0
        tps = seg // tm
        conv_spec = pl.BlockSpec((1, 2, d_ff), lambda i: (i // tps, 0, 0))
        scratch = [pltpu.VMEM((8, d_ff), F32)]
    in_specs.append(_const_spec(lw["wdn"].shape))
    args.append(lw["wdn"])
    return pl.pallas_call(
        functools.partial(_ffn_kernel, seg=seg, d_ff=d_ff, has_hist=has_hist),
        grid=(t // tm,),
        in_specs=in_specs,
        out_specs=[row(d), conv_spec],
        out_shape=[jax.ShapeDtypeStruct((t, d), F32),
                   jax.ShapeDtypeStruct((t // seg, 2, d_ff), F32)],
        scratch_shapes=scratch,
        compiler_params=_cparams(1),
        name="ffn",
    )(*args)


def _group_matrix(groups, width):
    m = np.zeros((width, width), np.float32)
    for g in groups:
        for i in g:
            m[i, g] = 1.0 / len(g)
    return jnp.asarray(m, dtype=BF16)


def _pack_layer(w, l, dims):
    nope, rope, heads = dims["nope"], dims["rope"], dims["mla_heads"]
    q_rank, kv_rank = dims["q_rank"], dims["kv_rank"]
    sb_w, df_w = dims["sb_w"], dims["df_w"]
    half = rope // 2
    assert nope == HEAD and rope == 32 and heads % 2 == 0 and q_rank + kv_rank + rope <= 4 * LANES
    assert (q_rank + kv_rank) % LANES == 0 and sb_w % (2 * LANES) == 0 and df_w % (2 * LANES) == 0
    w_in = w["w_in"][l]
    o_kr = q_rank + kv_rank
    o_sb = o_kr + rope
    o_df = o_sb + 3 * sb_w
    o_g = o_df + 3 * df_w
    col = lambda a, b: w_in[:, a:b]
    wa = jnp.pad(col(0, o_sb), ((0, 0), (0, 4 * LANES - o_sb)))
    w1 = jnp.concatenate([wa, col(o_sb, o_g)], axis=1).astype(BF16)
    wn = jnp.concatenate([col(0, o_kr), col(o_sb, o_sb + sb_w), col(o_df, o_df + df_w),
                          col(o_df + 2 * df_w, o_g)], axis=1).astype(BF16)
    wt = jnp.concatenate([col(o_sb + sb_w, o_df), col(o_df + df_w, o_df + 2 * df_w),
                          col(o_kr, o_sb)], axis=1).T.astype(BF16)

    cols = np.full((heads // 2) * 2 * LANES, -1, np.int64)
    gain_src = np.full(cols.shape, -1, np.int64)
    for p in range(heads // 2):
        base = 2 * LANES * p
        for j, hd in enumerate((2 * p, 2 * p + 1)):
            hb = hd * (nope + rope)
            cols[base + nope * j:base + nope * (j + 1)] = hb + np.arange(nope)
            gain_src[base + nope * j:base + nope * (j + 1)] = np.arange(nope)
            x1 = base + 2 * nope + half * j
            x2 = base + 2 * nope + 2 * half + half * j
            cols[x1:x1 + half] = hb + nope + np.arange(half)
            cols[x2:x2 + half] = hb + nope + half + np.arange(half)
            gain_src[x1:x1 + half] = nope + np.arange(half)
            gain_src[x2:x2 + half] = nope + half + np.arange(half)
    valid = jnp.asarray(cols >= 0)
    wuq = jnp.where(valid[None, :], w["mla_w_uq"][l][:, np.maximum(cols, 0)], 0.0).astype(BF16)
    qg = jnp.concatenate([w["mla_qn_g"][l], w["mla_qr_g"][l]])
    gq = jnp.where(valid, qg[np.maximum(gain_src, 0)], 0.0) * ((nope + rope) ** -0.5 * LOG2E)

    groups = [list(range(0, nope)), list(range(nope, 2 * nope))]
    for j in range(2):
        groups.append(list(range(2 * nope + half * j, 2 * nope + half * (j + 1)))
                      + list(range(2 * nope + 2 * half + half * j, 2 * nope + 2 * half + half * (j + 1))))
    gm_mla = _group_matrix(groups, 2 * LANES)
    gm64 = _group_matrix([list(range(HEAD * j, HEAD * (j + 1))) for j in range(4)], 2 * LANES)

    rep = np.zeros((rope, LANES), np.float32)
    for j in range(half):
        rep[j, [j, half + j]] = 1.0
        rep[half + j, [2 * half + j, 3 * half + j]] = 1.0

    row = lambda v: v.reshape(1, -1).astype(F32)
    colv = lambda v: v.reshape(-1, 1).astype(F32)
    gdk = jnp.tile(w["diff_kn_g"][l], df_w // HEAD)
    gkn = jnp.tile(w["mla_kn_g"][l], heads)
    return dict(
        gmix=row(w["mix_norm_g"][l]), w1=w1, wn=wn, wt=wt, gcq=row(w["mla_q_norm_g"][l]), wuq=wuq,
        gm_mla=gm_mla, gq=row(gq), gckv=row(w["mla_kv_norm_g"][l]),
        gkr=row(jnp.pad(w["mla_kr_g"][l], (0, LANES - rope))), gkr_col=colv(w["mla_kr_g"][l]),
        gm64=gm64,
        gdq=row(jnp.tile(w["diff_qn_g"][l], df_w // HEAD) * (LOG2E / 8.0)),
        gdk=row(gdk), gdk_col=colv(gdk),
        wukv=jnp.concatenate([w["mla_w_uk"][l], w["mla_w_uv"][l]], axis=1).astype(BF16),
        wukt=w["mla_w_uk"][l].T.astype(BF16), wuvt=w["mla_w_uv"][l].T.astype(BF16),
        gkn=row(gkn), gkn_col=colv(gkn),
        rep=jnp.asarray(rep, dtype=BF16),
        lam=w["diff_lambda"][l].astype(F32), gsub=row(w["diff_subln_g"][l]),
        wg=w_in[:, o_g:].astype(BF16),
        wbr=jnp.stack([w["w_br_mla"][l], w["w_br_sb"][l], w["w_br_diff"][l]]).astype(BF16),
        wout=w["w_out"][l].astype(BF16),
        gffn=row(w["ffn_norm_g"][l]), wup=w["ffn_w_up"][l].astype(BF16),
        cw=w["ffn_conv_w"][l].astype(F32), cb=row(w["ffn_conv_b"][l]),
        wdn=w["ffn_w_down"][l].astype(BF16),
    )


def _rope_tables(pos, rope, n_rows):
    half = rope // 2
    inv = ROPE_BASE ** (-jnp.arange(half, dtype=F32) / half)
    ang = pos.astype(F32)[:, None] * inv[None, :]
    cos, sin = jnp.cos(ang), jnp.sin(ang)
    z = lambda n: jnp.zeros((pos.shape[0], n), F32)
    tab = jnp.concatenate([
        cos, cos, cos, cos, z(LANES - 4 * half),
        -sin, -sin, sin, sin, z(LANES - 4 * half),
        cos, cos, z(LANES - 2 * half),
        -sin, sin, z(LANES - 2 * half)], axis=1)
    return jnp.tile(tab, (n_rows // pos.shape[0], 1)), jnp.concatenate([cos.T, sin.T], axis=0)


def _tail(x2d, o_mla, o_sb, o_df, lw, *, seq, hist=None, layer=0):
    f2 = lambda a: a.reshape(x2d.shape[0], a.shape[-1])
    x1 = _merge(x2d, f2(o_mla), f2(o_sb), f2(o_df), lw)
    return _ffn(x1, lw, seg=seq, hist=hist, layer=layer)


def _layer_prompt(x2d, lw, dims, *, batch, seq, tabs, layer, depth, bufs, slopes):
    q, sq, dq, dv, *bufs = _in_proj_t(x2d, lw, tabs[0], tabs[1], dims, batch=batch, seq=seq,
                                      layer=layer, depth=depth, prev=bufs)
    ckv, krt, skt, svt, dkt = bufs
    r3 = lambda a: a.reshape(batch, seq, a.shape[-1])
    kpt, mvt = _kv_up_t(ckv, krt, layer, lw, dims["rope"])
    src = lambda k, v, lyr, vlyr, vt=True: [dict(k=k, v=v, layer=lyr, vlayer=vlyr, kt=True, vt=vt,
                                                 pos0=0, mode="self")]
    lam_init = 0.8 - 0.6 * math.exp(-0.3 * layer)
    o_mla = _attention("mla", r3(q), src(kpt[None], mvt[None], 0, 0), ns=2, q_pos0=0)
    o_sb = _attention("sb", r3(sq), src(skt, svt, layer, layer), ns=2, q_pos0=0)
    o_df = _attention("diff", r3(dq), src(dkt, r3(dv)[None], layer, 0, vt=False), ns=2, q_pos0=0,
                      extras=(slopes, lw["lam"], lw["gsub"]), lam_init=lam_init)
    x2, conv = _tail(x2d, o_mla, o_sb, o_df, lw, seq=seq)
    return x2, dv, conv, tuple(bufs)


def _layer_sample(x2d, lw, dims, *, batch, seq, tab, layer, caches, conv_state, slopes):
    q, ckv, kr, sq, sk, sv, dq, dk, dv = _in_proj(x2d, lw, tab, dims)
    r3 = lambda a: a.reshape(batch, seq, a.shape[-1])
    r4 = lambda a: a.reshape(1, batch, seq, a.shape[-1])
    kpk, mv = _kv_up(ckv, kr, lw)
    c_ckv, c_krt, c_skt, c_svt, c_dkt, c_dv = caches
    plen = c_ckv.shape[2]
    ckpt, cmvt = _kv_up_t(c_ckv, c_krt, layer, lw, dims["rope"])

    def srcs(ck, cv, lyr, cvt, nk, nv):
        return [dict(k=ck, v=cv, layer=lyr, kt=True, vt=cvt, pos0=0, mode="full", tk=plen),
                dict(k=r4(nk), v=r4(nv), layer=0, kt=False, vt=False, pos0=plen, mode="masked")]

    lam_init = 0.8 - 0.6 * math.exp(-0.3 * layer)
    o_mla = _attention("mla", r3(q), srcs(ckpt[None], cmvt[None], 0, True, kpk, mv), ns=4, q_pos0=plen)
    o_sb = _attention("sb", r3(sq), srcs(c_skt, c_svt, layer, True, sk, sv), ns=4, q_pos0=plen)
    o_df = _attention("diff", r3(dq), srcs(c_dkt, c_dv, layer, False, dk, dv), ns=4, q_pos0=plen,
                      extras=(slopes, lw["lam"], lw["gsub"]), lam_init=lam_init)
    x2, conv = _tail(x2d, o_mla, o_sb, o_df, lw, seq=seq, hist=conv_state, layer=layer)
    return x2, (ckv, kr, sk, sv, dk, dv), conv


def kernel(x_prompt, x_sample, cache_mla_ckv, cache_mla_krope, cache_sb_k, cache_sb_v, cache_diff_k, cache_diff_v, state_ffn_conv, mix_norm_g, w_in, mla_q_norm_g, mla_w_uq, mla_kv_norm_g, mla_w_uk, mla_w_uv, mla_qn_g, mla_kn_g, mla_qr_g, mla_kr_g, diff_qn_g, diff_kn_g, diff_lambda, diff_subln_g, w_br_mla, w_br_sb, w_br_diff, w_out, ffn_norm_g, ffn_w_up, ffn_conv_w, ffn_conv_b, ffn_w_down):
    w = dict(mix_norm_g=mix_norm_g, w_in=w_in, mla_q_norm_g=mla_q_norm_g, mla_w_uq=mla_w_uq,
             mla_kv_norm_g=mla_kv_norm_g, mla_w_uk=mla_w_uk, mla_w_uv=mla_w_uv, mla_qn_g=mla_qn_g,
             mla_kn_g=mla_kn_g, mla_qr_g=mla_qr_g, mla_kr_g=mla_kr_g, diff_qn_g=diff_qn_g,
             diff_kn_g=diff_kn_g, diff_lambda=diff_lambda, diff_subln_g=diff_subln_g,
             w_br_mla=w_br_mla, w_br_sb=w_br_sb, w_br_diff=w_br_diff, w_out=w_out,
             ffn_norm_g=ffn_norm_g, ffn_w_up=ffn_w_up, ffn_conv_w=ffn_conv_w,
             ffn_conv_b=ffn_conv_b, ffn_w_down=ffn_w_down)
    depth = w_in.shape[0]
    bp, sp, d = x_prompt.shape
    bs, ss, _ = x_sample.shape
    past_len = cache_mla_ckv.shape[2]
    sb_heads, sb_dim = cache_sb_k.shape[3], cache_sb_k.shape[4]
    df_heads, df_dim = cache_diff_k.shape[3], cache_diff_k.shape[5]
    assert sb_dim == HEAD and df_dim == HEAD and ffn_conv_w.shape[1] == 3
    nope, rope = mla_qn_g.shape[1], mla_qr_g.shape[1]
    dims = dict(d_model=d, q_rank=mla_q_norm_g.shape[1], kv_rank=mla_kv_norm_g.shape[1],
                nope=nope, rope=rope, mla_heads=mla_w_uk.shape[2] // nope,
                sb_w=sb_heads * sb_dim, df_w=2 * df_heads * df_dim)
    sb_w, df_w = dims["sb_w"], dims["df_w"]

    tabs_p = _rope_tables(jnp.arange(sp, dtype=jnp.int32), rope, max(sp, ROW_TILE))
    tab_s, _ = _rope_tables(past_len + jnp.arange(ss, dtype=jnp.int32), rope, max(ss, ROW_TILE))
    slopes = 2.0 ** (-8.0 * jnp.arange(1, df_heads + 1, dtype=F32) / df_heads) * LOG2E

    feat = lambda c: jnp.moveaxis(c, 2, -1).reshape(depth, bs, -1, past_len)
    caches = (cache_mla_ckv, feat(cache_mla_krope), feat(cache_sb_k), feat(cache_sb_v),
              feat(cache_diff_k), cache_diff_v.reshape(depth, bs, past_len, df_w))

    xp = x_prompt.reshape(bp * sp, d)
    xs = x_sample.reshape(bs * ss, d)
    bufs, dv_p, rows_s, conv_p, conv_s = None, [], [], [], []
    for l in range(depth):
        lw = _pack_layer(w, l, dims)
        xp, dv, cp, bufs = _layer_prompt(xp, lw, dims, batch=bp, seq=sp, tabs=tabs_p, layer=l,
                                         depth=depth, bufs=bufs, slopes=slopes)
        xs, rs, cs = _layer_sample(xs, lw, dims, batch=bs, seq=ss, tab=tab_s, layer=l,
                                   caches=caches, conv_state=state_ffn_conv, slopes=slopes)
        dv_p.append(dv)
        rows_s.append(rs)
        conv_p.append(cp)
        conv_s.append(cs)

    p_ckv, p_krt, p_skt, p_svt, p_dkt = bufs
    tok = lambda t, tail: jnp.moveaxis(t.reshape((depth, bp) + tail + (sp,)), -1, 2)
    prompt_rows = (p_ckv, tok(p_krt, (rope,)), tok(p_skt, (sb_heads, sb_dim)),
                   tok(p_svt, (sb_heads, sb_dim)), tok(p_dkt, (df_heads, 2, df_dim)),
                   jnp.stack(dv_p).reshape(depth, bp, sp, df_heads, 2 * df_dim))

    def stack(i, tail):
        return jnp.stack([r[i] for r in rows_s]).reshape((depth, bs, ss) + tail)

    sample_rows = (stack(0, (dims["kv_rank"],)), stack(1, (rope,)), stack(2, (sb_heads, sb_dim)),
                   stack(3, (sb_heads, sb_dim)), stack(4, (df_heads, 2, df_dim)),
                   stack(5, (df_heads, 2 * df_dim)))
    return ((xp.reshape(bp, sp, d), xs.reshape(bs, ss, d)) + prompt_rows + (jnp.stack(conv_p),)
            + sample_rows + (jnp.stack(conv_s),))
```

```python
import functools
import math

import numpy as np
import jax
import jax.numpy as jnp
from jax import lax
from jax.experimental import pallas as pl
from jax.experimental.pallas import tpu as pltpu

F32 = jnp.float32
BF16 = jnp.bfloat16

EPS = 1e-6
CHUNK = 64
CHUNK_SHIFT = CHUNK.bit_length() - 1
assert 1 << CHUNK_SHIFT == CHUNK
ROPE_BASE = 10000.0
NEG_BIG = -1e30
HEAD = 64

LANES = 128
ROW_TILE = 512
KV_ROW_TILE = 1024
ATTN_TILE = 512
SB_BLOCK = 256
FF_CHUNK = 512
SB_SKIP_LOG2 = -160.0
LOG2E = math.log2(math.e)
VMEM_LIMIT = 52 * 1024 * 1024


def _cparams(n_axes):
    return pltpu.CompilerParams(dimension_semantics=("arbitrary",) * n_axes,
                                vmem_limit_bytes=VMEM_LIMIT)


def _const_spec(shape):
    nd = len(shape)
    return pl.BlockSpec(shape, lambda *_: (0,) * nd, pipeline_mode=pl.Buffered(1))


def _dot(a, b):
    return jnp.dot(a, b, preferred_element_type=F32)


def _dot_nt(a, b):
    return lax.dot_general(a, b, (((1,), (1,)), ((), ())), preferred_element_type=F32)


def _rmsnorm(x, g):
    return x * lax.rsqrt(jnp.mean(x * x, axis=-1, keepdims=True) + EPS) * g


def _group_mean_sq(x, gmat):
    sq = x * x
    hi = sq.astype(BF16)
    lo = (sq - hi.astype(F32)).astype(BF16)
    return _dot(hi, gmat) + _dot(lo, gmat)


def _row_group_rmsnorm(xt, gcol):
    r, n = xt.shape
    x3 = xt.reshape(r // HEAD, HEAD, n)
    y3 = x3 * lax.rsqrt(jnp.mean(x3 * x3, axis=1, keepdims=True) + EPS)
    return y3.reshape(r, n) * gcol


def _sigmoid(x):
    return 1.0 / (1.0 + jnp.exp(-x))


def _mla_query(cq, gcq_ref, wuq_ref, gm_mla_ref, gq_ref, cos_q, sin_q, q_ref, half):
    tm = cq.shape[0]
    cqn = _rmsnorm(cq, gcq_ref[...]).astype(BF16)
    q = _dot(cqn, wuq_ref[...])
    lane = lax.broadcasted_iota(jnp.int32, (tm, LANES), 1)
    for p in range(q.shape[1] // (2 * LANES)):
        lo = 2 * LANES * p
        blk = q[:, lo:lo + 2 * LANES]
        y = blk * lax.rsqrt(_group_mean_sq(blk, gm_mla_ref[...]) + EPS) * gq_ref[:, lo:lo + 2 * LANES]
        rp = y[:, LANES:]
        partner = jnp.where(lane < 2 * half, pltpu.roll(rp, LANES - 2 * half, 1),
                            pltpu.roll(rp, 2 * half, 1))
        q_ref[:, lo:lo + LANES] = y[:, :LANES].astype(BF16)
        q_ref[:, lo + LANES:lo + 2 * LANES] = (rp * cos_q + partner * sin_q).astype(BF16)


def _diff_query(zq, gm64_ref, gdq_ref, dq_ref):
    for s in range(zq.shape[1] // (2 * LANES)):
        lo = 2 * LANES * s
        blk = zq[:, lo:lo + 2 * LANES]
        r = lax.rsqrt(_group_mean_sq(blk, gm64_ref[...]) + EPS)
        dq_ref[:, lo:lo + 2 * LANES] = (blk * r * gdq_ref[:, lo:lo + 2 * LANES]).astype(BF16)


def _in_proj_kernel(x_ref, gmix_ref, w_ref, gcq_ref, wuq_ref, gm_mla_ref, gq_ref, gckv_ref,
                    gkr_ref, tab_ref, gm64_ref, gdq_ref, gdk_ref,
                    q_ref, ckv_ref, kr_ref, sq_ref, sk_ref, sv_ref, dq_ref, dk_ref, dv_ref,
                    *, q_rank, kv_rank, rope, sb_w, df_w):
    tm = x_ref.shape[0]
    half = rope // 2
    h = _rmsnorm(x_ref[...], gmix_ref[...]).astype(BF16)
    tab = tab_ref[...]
    cos_k, sin_k = tab[:, 2 * LANES:3 * LANES], tab[:, 3 * LANES:4 * LANES]
    lane = lax.broadcasted_iota(jnp.int32, (tm, LANES), 1)

    za = _dot(h, w_ref[:, 0:4 * LANES])
    _mla_query(za[:, 0:q_rank], gcq_ref, wuq_ref, gm_mla_ref, gq_ref, tab[:, 0:LANES],
               tab[:, LANES:2 * LANES], q_ref, half)
    ckv_ref[...] = _rmsnorm(za[:, q_rank:q_rank + kv_rank], gckv_ref[...])

    kr = za[:, q_rank + kv_rank:4 * LANES]
    ms = jnp.sum(kr * kr, axis=-1, keepdims=True) * (1.0 / rope)
    krn = kr * lax.rsqrt(ms + EPS) * gkr_ref[...]
    partner = jnp.where(lane < half, pltpu.roll(krn, LANES - half, 1), pltpu.roll(krn, half, 1))
    kr_ref[...] = (krn * cos_k + partner * sin_k)[:, :rope]

    o = 4 * LANES
    zs = _dot(h, w_ref[:, o:o + 3 * sb_w])
    sq_ref[...] = (zs[:, 0:sb_w] * (LOG2E / 8.0)).astype(BF16)
    sk_ref[...] = zs[:, sb_w:2 * sb_w]
    sv_ref[...] = zs[:, 2 * sb_w:3 * sb_w]

    o = o + 3 * sb_w
    zd = _dot(h, w_ref[:, o:o + 3 * df_w])
    _diff_query(zd[:, 0:df_w], gm64_ref, gdq_ref, dq_ref)
    for s in range(df_w // (2 * LANES)):
        lo = 2 * LANES * s
        blk = zd[:, df_w + lo:df_w + lo + 2 * LANES]
        r = lax.rsqrt(_group_mean_sq(blk, gm64_ref[...]) + EPS)
        dk_ref[:, lo:lo + 2 * LANES] = blk * r * gdk_ref[:, lo:lo + 2 * LANES]
    dv_ref[...] = zd[:, 2 * df_w:3 * df_w]


def _in_proj(x2d, lw, tab, dims):
    t, d = x2d.shape
    tm = ROW_TILE
    assert t % tm == 0 and tab.shape[0] % tm == 0
    n_tab = tab.shape[0] // tm
    row = lambda w: pl.BlockSpec((tm, w), lambda i: (i, 0))
    consts = [lw["gmix"], lw["w1"], lw["gcq"], lw["wuq"], lw["gm_mla"], lw["gq"], lw["gckv"], lw["gkr"]]
    consts2 = [lw["gm64"], lw["gdq"], lw["gdk"]]
    sb_w, df_w = dims["sb_w"], dims["df_w"]
    out_shape = [
        jax.ShapeDtypeStruct((t, lw["wuq"].shape[1]), BF16),
        jax.ShapeDtypeStruct((t, dims["kv_rank"]), F32),
        jax.ShapeDtypeStruct((t, dims["rope"]), F32),
        jax.ShapeDtypeStruct((t, sb_w), BF16),
        jax.ShapeDtypeStruct((t, sb_w), F32),
        jax.ShapeDtypeStruct((t, sb_w), F32),
        jax.ShapeDtypeStruct((t, df_w), BF16),
        jax.ShapeDtypeStruct((t, df_w), F32),
        jax.ShapeDtypeStruct((t, df_w), F32),
    ]
    kern = functools.partial(_in_proj_kernel, q_rank=dims["q_rank"], kv_rank=dims["kv_rank"],
                             rope=dims["rope"], sb_w=sb_w, df_w=df_w)
    return pl.pallas_call(
        kern,
        grid=(t // tm,),
        in_specs=([row(d)] + [_const_spec(c.shape) for c in consts]
                  + [pl.BlockSpec((tm, tab.shape[1]), lambda i: (i % n_tab, 0))]
                  + [_const_spec(c.shape) for c in consts2]),
        out_specs=[row(s.shape[1]) for s in out_shape],
        out_shape=out_shape,
        compiler_params=_cparams(1),
        name="in_proj",
    )(x2d, *consts, tab, *consts2)


def _in_proj_t_kernel(*refs, q_rank, kv_rank, rope, sb_w, df_w, n_alias):
    (x_ref, gmix_ref, wn_ref, wt_ref, gcq_ref, wuq_ref, gm_mla_ref, gq_ref, gckv_ref, gkr_ref,
     tab_ref, tabt_ref, gm64_ref, gdq_ref, gdk_ref) = refs[:15]
    (q_ref, sq_ref, dq_ref, ckv_ref, krt_ref, skt_ref, svt_ref, dkt_ref, dvi_ref) = refs[15 + n_alias:]
    half = rope // 2
    tm = x_ref.shape[0]
    h = _rmsnorm(x_ref[...], gmix_ref[...]).astype(BF16)
    tab = tab_ref[...]

    o = q_rank + kv_rank
    za = _dot(h, wn_ref[:, 0:o])
    _mla_query(za[:, 0:q_rank], gcq_ref, wuq_ref, gm_mla_ref, gq_ref, tab[:, 0:LANES],
               tab[:, LANES:2 * LANES], q_ref, half)
    ckv_ref[0, 0] = _rmsnorm(za[:, q_rank:o], gckv_ref[...])

    sq_ref[...] = (_dot(h, wn_ref[:, o:o + sb_w]) * (LOG2E / 8.0)).astype(BF16)
    o = o + sb_w
    zd = _dot(h, wn_ref[:, o:o + 2 * df_w])
    _diff_query(zd[:, 0:df_w], gm64_ref, gdq_ref, dq_ref)
    n_dv = df_w // LANES
    for hh in range(n_dv):
        dvi_ref[0, 0, pl.ds(hh, tm, stride=n_dv), :] = zd[:, df_w + LANES * hh:df_w + LANES * (hh + 1)]

    skt_ref[0, 0] = _dot_nt(wt_ref[0:sb_w, :], h)
    svt_ref[0, 0] = _dot_nt(wt_ref[sb_w:2 * sb_w, :], h)
    o = 2 * sb_w
    dkt_ref[0, 0] = _row_group_rmsnorm(_dot_nt(wt_ref[o:o + df_w, :], h), gdk_ref[...])
    o = o + df_w
    krt = _dot_nt(wt_ref[o:o + rope, :], h)
    krn = krt * lax.rsqrt(jnp.mean(krt * krt, axis=0, keepdims=True) + EPS) * gkr_ref[...]
    x1, x2 = krn[:half], krn[half:]
    cos_t, sin_t = tabt_ref[0:half, :], tabt_ref[half:rope, :]
    krt_ref[0, 0] = jnp.concatenate([x1 * cos_t - x2 * sin_t, x1 * sin_t + x2 * cos_t], axis=0)


def _in_proj_t(x2d, lw, tab, tabt, dims, *, batch, seq, layer, depth, prev):
    t, d = x2d.shape
    tm = ROW_TILE
    assert seq % tm == 0 and t == batch * seq
    nst = seq // tm
    row = lambda w: pl.BlockSpec((tm, w), lambda i: (i, 0))
    consts = [lw["gmix"], lw["wn"], lw["wt"], lw["gcq"], lw["wuq"], lw["gm_mla"], lw["gq"], lw["gckv"],
              lw["gkr_col"]]
    consts2 = [lw["gm64"], lw["gdq"], lw["gdk_col"]]
    sb_w, df_w, rope, kvr = dims["sb_w"], dims["df_w"], dims["rope"], dims["kv_rank"]
    n_alias = 0 if prev is None else len(prev)
    in_specs = ([row(d)] + [_const_spec(c.shape) for c in consts]
                + [pl.BlockSpec((tm, tab.shape[1]), lambda i: (i % nst, 0)),
                   pl.BlockSpec((rope, tm), lambda i: (0, i % nst))]
                + [_const_spec(c.shape) for c in consts2]
                + [pl.BlockSpec(memory_space=pl.ANY)] * n_alias)
    n_plain = len(in_specs) - n_alias
    tok = lambda w: pl.BlockSpec((1, 1, tm, w), lambda i: (layer, i // nst, i % nst, 0))
    feat = lambda r: pl.BlockSpec((1, 1, r, tm), lambda i: (layer, i // nst, 0, i % nst))
    n_dv = df_w // LANES
    out_shape = [
        jax.ShapeDtypeStruct((t, lw["wuq"].shape[1]), BF16),
        jax.ShapeDtypeStruct((t, sb_w), BF16),
        jax.ShapeDtypeStruct((t, df_w), BF16),
        jax.ShapeDtypeStruct((depth, batch, seq, kvr), F32),
        jax.ShapeDtypeStruct((depth, batch, rope, seq), F32),
        jax.ShapeDtypeStruct((depth, batch, sb_w, seq), F32),
        jax.ShapeDtypeStruct((depth, batch, sb_w, seq), F32),
        jax.ShapeDtypeStruct((depth, batch, df_w, seq), F32),
        jax.ShapeDtypeStruct((depth, batch, seq * n_dv, LANES), F32),
    ]
    out_specs = [row(out_shape[0].shape[1]), row(sb_w), row(df_w),
                 tok(kvr), feat(rope), feat(sb_w), feat(sb_w), feat(df_w),
                 pl.BlockSpec((1, 1, tm * n_dv, LANES), lambda i: (layer, i // nst, i % nst, 0))]
    kern = functools.partial(_in_proj_t_kernel, q_rank=dims["q_rank"], kv_rank=kvr, rope=rope,
                             sb_w=sb_w, df_w=df_w, n_alias=n_alias)
    return pl.pallas_call(
        kern,
        grid=(t // tm,),
        in_specs=in_specs,
        out_specs=out_specs,
        out_shape=out_shape,
        input_output_aliases={n_plain + j: 3 + j for j in range(n_alias)},
        compiler_params=_cparams(1),
        name="in_proj_t",
    )(x2d, *consts, tab, tabt, *consts2, *(prev or ()))


def _kv_up_kernel(ckv_ref, kr_ref, wukv_ref, gm64_ref, gkn_ref, rep_ref, k_ref, v_ref, *, kn_w):
    c = ckv_ref[...].astype(BF16)
    kv = _dot(c, wukv_ref[...])
    krrep = _dot(kr_ref[...].astype(BF16), rep_ref[...]).astype(BF16)
    for b in range(kn_w // (2 * LANES)):
        s = 2 * LANES * b
        blk = kv[:, s:s + 2 * LANES]
        r = lax.rsqrt(_group_mean_sq(blk, gm64_ref[...]) + EPS)
        kn = (blk * r * gkn_ref[:, s:s + 2 * LANES]).astype(BF16)
        for j in range(2):
            p = 2 * b + j
            k_ref[:, 2 * LANES * p:2 * LANES * p + LANES] = kn[:, LANES * j:LANES * (j + 1)]
            k_ref[:, 2 * LANES * p + LANES:2 * LANES * (p + 1)] = krrep
    v_ref[...] = kv[:, kn_w:].astype(BF16)


def _kv_up(ckv2d, kr2d, lw):
    r, kvr = ckv2d.shape
    tr = min(KV_ROW_TILE, r)
    assert r % tr == 0
    kn_w = lw["gkn"].shape[1]
    consts = [lw["wukv"], lw["gm64"], lw["gkn"], lw["rep"]]
    out_shape = [jax.ShapeDtypeStruct((r, 2 * kn_w), BF16),
                 jax.ShapeDtypeStruct((r, lw["wukv"].shape[1] - kn_w), BF16)]
    return pl.pallas_call(
        functools.partial(_kv_up_kernel, kn_w=kn_w),
        grid=(r // tr,),
        in_specs=[pl.BlockSpec((tr, kvr), lambda i: (i, 0)),
                  pl.BlockSpec((tr, kr2d.shape[1]), lambda i: (i, 0))]
                 + [_const_spec(c.shape) for c in consts],
        out_specs=[pl.BlockSpec((tr, s.shape[1]), lambda i: (i, 0)) for s in out_shape],
        out_shape=out_shape,
        compiler_params=_cparams(1),
        name="mla_kv_up",
    )(ckv2d, kr2d, *consts)


def _kv_up_t_kernel(ckv_ref, krt_ref, wukt_ref, wuvt_ref, gkn_ref, kt_ref, vt_ref, *, rope):
    half = rope // 2
    c = ckv_ref[0, 0].astype(BF16)
    n = c.shape[0]
    knt = _row_group_rmsnorm(_dot_nt(wukt_ref[...], c), gkn_ref[...]).astype(BF16)
    kr = krt_ref[0, 0].astype(BF16)
    x1, x2 = kr[:half], kr[half:]
    krrep = jnp.concatenate([x1, x1, x2, x2, jnp.zeros((LANES - 4 * half, n), BF16)], axis=0)
    for p in range(knt.shape[0] // LANES):
        kt_ref[0, 2 * LANES * p:2 * LANES * p + LANES, :] = knt[LANES * p:LANES * (p + 1)]
        kt_ref[0, 2 * LANES * p + LANES:2 * LANES * (p + 1), :] = krrep
    vt_ref[0] = _dot_nt(wuvt_ref[...], c).astype(BF16)


def _kv_up_t(ckv4, krt4, layer, lw, rope):
    _, b, s, kvr = ckv4.shape
    tr = min(KV_ROW_TILE, s)
    assert s % tr == 0
    consts = [lw["wukt"], lw["wuvt"], lw["gkn_col"]]
    kn_w, v_w = lw["wukt"].shape[0], lw["wuvt"].shape[0]
    out_shape = [jax.ShapeDtypeStruct((b, 2 * kn_w, s), BF16), jax.ShapeDtypeStruct((b, v_w, s), BF16)]
    return pl.pallas_call(
        functools.partial(_kv_up_t_kernel, rope=rope),
        grid=(b, s // tr),
        in_specs=[pl.BlockSpec((1, 1, tr, kvr), lambda bi, i: (layer, bi, i, 0)),
                  pl.BlockSpec((1, 1, rope, tr), lambda bi, i: (layer, bi, 0, i))]
                 + [_const_spec(c.shape) for c in consts],
        out_specs=[pl.BlockSpec((1, o.shape[1], tr), lambda bi, i: (bi, 0, i)) for o in out_shape],
        out_shape=out_shape,
        compiler_params=_cparams(2),
        name="mla_kv_up_t",
    )(ckv4, krt4, *consts)


def _stream_lane_masks(kind, ns, width):
    lane = lax.broadcasted_iota(jnp.int32, (1, width), 1)
    masks = []
    for i in range(ns):
        if kind == "mla":
            r = lane - 2 * LANES * (i // 2)
            j = i % 2
            masks.append(((r >= HEAD * j) & (r < HEAD * (j + 1)))
                         | ((r >= 128 + 16 * j) & (r < 128 + 16 * (j + 1)))
                         | ((r >= 160 + 16 * j) & (r < 160 + 16 * (j + 1))))
        else:
            masks.append((lane >= HEAD * i) & (lane < HEAD * (i + 1)))
    return masks


def _attn_kernel(*refs, kind, srcs, ns, tq, q_pos0, lam_init):
    it = iter(refs)
    slopes_ref = next(it) if kind == "diff" else None
    q_ref = next(it)
    kv_refs = [(next(it), next(it)) for _ in srcs]
    if kind == "diff":
        lam_ref, gsub_ref = next(it), next(it)
    o_ref = next(it)
    scr = [(next(it), next(it)) if s["cast"] else None for s in srcs]

    grp = pl.program_id(1)
    qi = pl.program_id(2)
    rows = ns * tq
    vw = HEAD * ns

    for s, sc, (k_ref, v_ref) in zip(srcs, scr, kv_refs):
        if sc is not None:
            @pl.when(qi == 0)
            def _(s=s, sc=sc, k_ref=k_ref, v_ref=v_ref):
                sc[0][...] = k_ref[0, 0].astype(BF16)
                if s["vil"]:
                    for j in range(vw // LANES):
                        rows_j = pl.ds(grp * (vw // LANES) + j, s["sk"], stride=s["vil"])
                        sc[1][:, LANES * j:LANES * (j + 1)] = v_ref[0, 0, rows_j, :].astype(BF16)
                else:
                    sc[1][...] = v_ref[0, 0].astype(BF16)

    q = q_ref[0]
    zero = jnp.zeros_like(q)
    qs = jnp.concatenate([jnp.where(m, q, zero) for m in _stream_lane_masks(kind, ns, q.shape[1])],
                         axis=0)
    q_start = q_pos0 + qi * tq
    qpos = jnp.concatenate([q_start + lax.broadcasted_iota(jnp.int32, (tq, 1), 0)] * ns, axis=0)
    if kind == "diff":
        slope = jnp.concatenate(
            [jnp.full((tq, 1), slopes_ref[grp * (ns // 2) + i // 2], F32) for i in range(ns)], axis=0)

    if kind == "sb":
        init = (jnp.zeros((rows, 1), F32), jnp.zeros((rows, vw), F32))
    else:
        init = (jnp.full((rows, 1), NEG_BIG, F32), jnp.zeros((rows, 1), F32),
                jnp.zeros((rows, vw), F32))

    def tile_step(st, k_t, v_t, k_start, src, masked, upper):
        tk = src["tk"]
        kpos = k_start + lax.broadcasted_iota(jnp.int32, (1, tk), 1)
        s = _dot(qs, k_t) if src["kt"] else _dot_nt(qs, k_t)
        pv = (lambda p: _dot_nt(p, v_t)) if src["vt"] else (lambda p: _dot(p, v_t))
        if masked:
            valid = (kpos < qpos) if kind == "sb" else ((kpos >> CHUNK_SHIFT) <= (qpos >> CHUNK_SHIFT))
        if kind == "sb":
            c, acc = st
            lp = jnp.log(1.0 + jnp.exp2(-jnp.abs(s))) * LOG2E
            ls = jnp.minimum(s, 0.0) - lp
            l1 = ls - s
            if masked:
                l1 = jnp.where(valid, l1, 0.0)
            hi = l1.astype(BF16)
            lo = (l1 - hi.astype(F32)).astype(BF16)
            cb = upper.shape[0]
            nb = tk // cb
            if nb > 1:
                hi = jnp.concatenate([hi[:, cb * j:cb * (j + 1)] for j in range(nb)], axis=0)
                lo = jnp.concatenate([lo[:, cb * j:cb * (j + 1)] for j in range(nb)], axis=0)
            suffix = _dot(hi, upper) + _dot(lo, upper)
            parts = [None] * nb
            for j in reversed(range(nb)):
                parts[j] = jnp.exp2(ls[:, cb * j:cb * (j + 1)] + suffix[rows * j:rows * (j + 1)] + c)
                c = c + jnp.sum(l1[:, cb * j:cb * (j + 1)], axis=-1, keepdims=True)
            a = parts[0] if nb == 1 else jnp.concatenate(parts, axis=1)
            if masked:
                a = jnp.where(valid, a, 0.0)
            return (c, acc + pv(a.astype(BF16)))
        m, l, acc = st
        if kind == "diff":
            dist = qpos - kpos
            s = s - slope * (jnp.abs(dist) if masked else dist).astype(F32)
        if masked:
            s = jnp.where(valid, s, -jnp.inf)
        m_new = jnp.maximum(m, jnp.max(s, axis=-1, keepdims=True))
        alpha = jnp.exp2(m - m_new)
        p = jnp.exp2(s - m_new)
        l = alpha * l + jnp.sum(p, axis=-1, keepdims=True)
        return (m_new, l, alpha * acc + pv(p.astype(BF16)))

    def upper_tri(tk):
        if kind != "sb":
            return None
        cb = min(tk, SB_BLOCK)
        r = lax.broadcasted_iota(jnp.int32, (cb, cb), 0)
        c = lax.broadcasted_iota(jnp.int32, (cb, cb), 1)
        return jnp.where(r > c, 1.0, 0.0).astype(BF16)

    st = init
    for s, sc, (k_ref, v_ref) in reversed(list(zip(srcs, scr, kv_refs))):
        tk = s["tk"]
        upper = upper_tri(tk)

        def load(k_off, _s=s, _sc=sc, _k=k_ref, _v=v_ref):
            def one(ref4, ref2, transposed):
                win = pl.ds(k_off, _s["tk"])
                if ref2 is not None:
                    return ref2[:, win] if transposed else ref2[win, :]
                return ref4[0, 0, :, win] if transposed else ref4[0, 0, win, :]
            return (one(_k, _sc and _sc[0], _s["kt"]), one(_v, _sc and _sc[1], _s["vt"]))

        if s["mode"] == "self":
            k_t, v_t = load(pl.multiple_of(qi * tk, tk))
            st = tile_step(st, k_t, v_t, s["pos0"] + qi * tk, s, True, upper)
            n_full = qi
        elif s["mode"] == "masked":
            k_t, v_t = load(0)
            st = tile_step(st, k_t, v_t, s["pos0"], s, True, upper)
            continue
        else:
            n_full = s["sk"] // tk

        def body(i, st_, _s=s, _load=load, _n=n_full, _upper=upper):
            kj = _n - 1 - i
            k_t, v_t = _load(pl.multiple_of(kj * _s["tk"], _s["tk"]))
            return tile_step(st_, k_t, v_t, _s["pos0"] + kj * _s["tk"], _s, False, _upper)

        if isinstance(n_full, int) and n_full <= 2:
            for i in range(n_full):
                st = body(i, st)
        elif kind == "sb":
            def live(st_):
                return (jnp.max(st_[0]) > SB_SKIP_LOG2).astype(jnp.int32)

            def wbody(carry, _body=body):
                i, _, st_ = carry
                st_ = _body(i, st_)
                return i + 1, live(st_), st_

            _, _, st = lax.while_loop(lambda carry, _n=n_full: (carry[0] < _n) & (carry[1] > 0),
                                      wbody, (jnp.int32(0), live(st), st))
        else:
            st = lax.fori_loop(0, n_full, body, st)

    acc = st[-1]
    if kind == "diff":
        l = st[1]
        lv = lam_ref[...]
        lam = (jnp.exp(jnp.sum(lv[0:1] * lv[1:2], axis=-1, keepdims=True))
               - jnp.exp(jnp.sum(lv[2:3] * lv[3:4], axis=-1, keepdims=True)) + lam_init)
        for hh in range(ns // 2):
            lanes = slice(2 * HEAD * hh, 2 * HEAD * (hh + 1))
            r0, r1 = slice(tq * 2 * hh, tq * (2 * hh + 1)), slice(tq * (2 * hh + 1), tq * (2 * hh + 2))
            o = acc[r0, lanes] / l[r0] - lam * (acc[r1, lanes] / l[r1])
            o_ref[0, :, lanes] = (_rmsnorm(o, gsub_ref[...]) * (1.0 - lam_init)).astype(BF16)
    else:
        lane = lax.broadcasted_iota(jnp.int32, (tq, vw), 1)
        o = jnp.zeros((tq, vw), F32)
        for i in range(ns):
            blk = acc[tq * i:tq * (i + 1)]
            if kind == "mla":
                blk = blk / st[1][tq * i:tq * (i + 1)]
            o = jnp.where((lane >= HEAD * i) & (lane < HEAD * (i + 1)), blk, o)
        o_ref[0] = o.astype(BF16)


def _attention(kind, q3, sources, *, ns, q_pos0, extras=(), lam_init=0.0):
    b, sq, qtot = q3.shape
    w = (2 * HEAD if kind == "mla" else HEAD) * ns
    vw = HEAD * ns
    n_groups = qtot // w
    tq = min(ATTN_TILE, sq)
    assert sq % tq == 0 and qtot % w == 0
    srcs, in_specs, args, scratch = [], [], [], []
    if kind == "diff":
        in_specs.append(pl.BlockSpec(memory_space=pltpu.SMEM))
        args.append(extras[0])
    in_specs.append(pl.BlockSpec((1, tq, w), lambda bi, g, qi: (bi, qi, g)))
    args.append(q3)

    def kv_spec(arr, width, transposed, layer):
        if transposed:
            assert arr.shape[2] == n_groups * width
            return pl.BlockSpec((1, 1, width, arr.shape[3]), lambda bi, g, qi: (layer, bi, g, 0))
        assert arr.shape[3] == n_groups * width
        return pl.BlockSpec((1, 1, arr.shape[2], width), lambda bi, g, qi: (layer, bi, 0, g))

    for s in sources:
        k, v, kt, vt = s["k"], s["v"], s["kt"], s["vt"]
        sk = k.shape[3] if kt else k.shape[2]
        tk = tq if s["mode"] == "self" else (sk if s["mode"] == "masked" else s.get("tk", min(ATTN_TILE, sk)))
        assert sk % tk == 0 and (tk <= SB_BLOCK or tk % SB_BLOCK == 0)
        cast = k.dtype != BF16
        assert cast == (v.dtype != BF16)
        vil = s.get("vil", 0)
        vlayer = s.get("vlayer", s["layer"])
        srcs.append(dict(mode=s["mode"], pos0=s["pos0"], sk=sk, tk=tk, cast=cast, kt=kt, vt=vt, vil=vil))
        if vil:
            assert cast and not vt and v.shape[2:] == (sk * vil, LANES) and vw % LANES == 0
            v_spec = pl.BlockSpec((1, 1, sk * vil, LANES), lambda bi, g, qi, _l=vlayer: (_l, bi, 0, 0))
        else:
            v_spec = kv_spec(v, vw, vt, vlayer)
        in_specs += [kv_spec(k, w, kt, s["layer"]), v_spec]
        args += [k, v]
        if cast:
            scratch += [pltpu.VMEM((w, sk) if kt else (sk, w), BF16),
                        pltpu.VMEM((vw, sk) if vt else (sk, vw), BF16)]
    if kind == "diff":
        in_specs += [_const_spec(extras[1].shape), _const_spec(extras[2].shape)]
        args += [extras[1], extras[2]]
    kern = functools.partial(_attn_kernel, kind=kind, srcs=srcs, ns=ns, tq=tq, q_pos0=q_pos0,
                             lam_init=lam_init)
    return pl.pallas_call(
        kern,
        grid=(b, n_groups, sq // tq),
        in_specs=in_specs,
        out_specs=pl.BlockSpec((1, tq, vw), lambda bi, g, qi: (bi, qi, g)),
        out_shape=jax.ShapeDtypeStruct((b, sq, n_groups * vw), BF16),
        scratch_shapes=scratch,
        compiler_params=_cparams(3),
        name="attn_" + kind,
    )(*args)


def _merge_kernel(x_ref, gmix_ref, wg_ref, om_ref, os_ref, od_ref, wbr_ref, wout_ref, o_ref):
    x = x_ref[...]
    d = x.shape[1]
    h = _rmsnorm(x, gmix_ref[...]).astype(BF16)
    merged = None
    for i, br_ref in enumerate((om_ref, os_ref, od_ref)):
        gate = _sigmoid(_dot(h, wg_ref[:, i * d:(i + 1) * d]))
        term = gate * _dot(br_ref[...], wbr_ref[i])
        merged = term if merged is None else merged + term
    o_ref[...] = x + _dot(merged.astype(BF16), wout_ref[...])


def _merge(x2d, o_mla, o_sb, o_diff, lw):
    t, d = x2d.shape
    tm = ROW_TILE
    row = lambda w: pl.BlockSpec((tm, w), lambda i: (i, 0))
    return pl.pallas_call(
        _merge_kernel,
        grid=(t // tm,),
        in_specs=[row(d), _const_spec(lw["gmix"].shape), _const_spec(lw["wg"].shape),
                  row(o_mla.shape[1]), row(o_sb.shape[1]), row(o_diff.shape[1]),
                  _const_spec(lw["wbr"].shape), _const_spec(lw["wout"].shape)],
        out_specs=row(d),
        out_shape=jax.ShapeDtypeStruct((t, d), F32),
        compiler_params=_cparams(1),
        name="merge",
    )(x2d, lw["gmix"], lw["wg"], o_mla, o_sb, o_diff, lw["wbr"], lw["wout"])


def _conv_rows(a, h0, h1, cw, cb):
    n = a.shape[0]
    row = lax.broadcasted_iota(jnp.int32, (n, 1), 0)
    p1 = jnp.where(row == 0, h1, pltpu.roll(a, 1, 0))
    p2 = jnp.where(row == 0, h0, jnp.where(row == 1, h1, pltpu.roll(a, 2, 0)))
    return p2 * cw[0:1] + p1 * cw[1:2] + a * cw[2:3] + cb


def _ffn_kernel(*refs, seg, d_ff, has_hist):
    it = iter(refs)
    x_ref, g_ref, wup_ref, cw_ref, cb_ref = next(it), next(it), next(it), next(it), next(it)
    hist_ref = next(it) if has_hist else None
    wdn_ref, o_ref, conv_ref = next(it), next(it), next(it)
    carry_ref = next(it) if not has_hist else None
    act_ref = next(it) if has_hist else None

    x = x_ref[...]
    tm = x.shape[0]
    h = _rmsnorm(x, g_ref[...]).astype(BF16)
    cw, cb = cw_ref[...], cb_ref[...]

    if not has_hist:
        i = pl.program_id(0)
        tiles_per_seq = seg // tm

        @pl.when((i % tiles_per_seq) == 0)
        def _():
            carry_ref[...] = jnp.zeros_like(carry_ref)

        tail = carry_ref[...]
        y, tails = x, []
        for c0 in range(0, d_ff, FF_CHUNK):
            cs = min(FF_CHUNK, d_ff - c0)
            a = _dot(h, wup_ref[:, c0:c0 + cs])
            u = _dot(h, wup_ref[:, d_ff + c0:d_ff + c0 + cs])
            c = _conv_rows(a, tail[6:7, c0:c0 + cs], tail[7:8, c0:c0 + cs], cw[:, c0:c0 + cs],
                           cb[:, c0:c0 + cs])
            tails.append(a[tm - 8:tm])
            y = y + _dot((c * _sigmoid(c) * u).astype(BF16), wdn_ref[c0:c0 + cs, :])
        new_tail = jnp.concatenate(tails, axis=1)
        carry_ref[...] = new_tail

        @pl.when((i % tiles_per_seq) == tiles_per_seq - 1)
        def _():
            conv_ref[0] = new_tail[6:8]

        o_ref[...] = y
    else:
        a = _dot(h, wup_ref[:, 0:d_ff])
        u = _dot(h, wup_ref[:, d_ff:2 * d_ff])
        for s in range(tm // seg):
            a_s = a[s * seg:(s + 1) * seg]
            hs = hist_ref[0, s]
            c = _conv_rows(a_s, hs[0:1], hs[1:2], cw, cb)
            act_ref[s * seg:(s + 1) * seg, :] = (c * _sigmoid(c) * u[s * seg:(s + 1) * seg]).astype(BF16)
            conv_ref[s] = a_s[seg - 2:seg]
        o_ref[...] = x + _dot(act_ref[...], wdn_ref[...])


def _ffn(x2d, lw, *, seg, hist=None, layer=0):
    t, d = x2d.shape
    tm = ROW_TILE
    d_ff = lw["wdn"].shape[0]
    has_hist = hist is not None
    row = lambda w: pl.BlockSpec((tm, w), lambda i: (i, 0))
    in_specs = [row(d), _const_spec(lw["gffn"].shape), _const_spec(lw["wup"].shape),
                _const_spec(lw["cw"].shape), _const_spec(lw["cb"].shape)]
    args = [x2d, lw["gffn"], lw["wup"], lw["cw"], lw["cb"]]
    if has_hist:
        assert tm % seg == 0 and seg % 8 == 0
        nseq = tm // seg
        in_specs.append(pl.BlockSpec((1, nseq, 2, d_ff), lambda i: (layer, i, 0, 0)))
        args.append(hist)
        conv_spec = pl.BlockSpec((nseq, 2, d_ff), lambda i: (i, 0, 0))
        scratch = [pltpu.VMEM((tm, d_ff), BF16)]
    else:
        assert seg % tm == 0
        tps = seg // tm
        conv_spec = pl.BlockSpec((1, 2, d_ff), lambda i: (i // tps, 0, 0))
        scratch = [pltpu.VMEM((8, d_ff), F32)]
    in_specs.append(_const_spec(lw["wdn"].shape))
    args.append(lw["wdn"])
    return pl.pallas_call(
        functools.partial(_ffn_kernel, seg=seg, d_ff=d_ff, has_hist=has_hist),
        grid=(t // tm,),
        in_specs=in_specs,
        out_specs=[row(d), conv_spec],
        out_shape=[jax.ShapeDtypeStruct((t, d), F32),
                   jax.ShapeDtypeStruct((t // seg, 2, d_ff), F32)],
        scratch_shapes=scratch,
        compiler_params=_cparams(1),
        name="ffn",
    )(*args)


def _group_matrix(groups, width):
    m = np.zeros((width, width), np.float32)
    for g in groups:
        for i in g:
            m[i, g] = 1.0 / len(g)
    return jnp.asarray(m, dtype=BF16)


def _pack_layer(w, l, dims):
    nope, rope, heads = dims["nope"], dims["rope"], dims["mla_heads"]
    q_rank, kv_rank = dims["q_rank"], dims["kv_rank"]
    sb_w, df_w = dims["sb_w"], dims["df_w"]
    half = rope // 2
    assert nope == HEAD and rope == 32 and heads % 2 == 0 and q_rank + kv_rank + rope <= 4 * LANES
    assert (q_rank + kv_rank) % LANES == 0 and sb_w % (2 * LANES) == 0 and df_w % (2 * LANES) == 0
    w_in = w["w_in"][l]
    o_kr = q_rank + kv_rank
    o_sb = o_kr + rope
    o_df = o_sb + 3 * sb_w
    o_g = o_df + 3 * df_w
    col = lambda a, b: w_in[:, a:b]
    wa = jnp.pad(col(0, o_sb), ((0, 0), (0, 4 * LANES - o_sb)))
    w1 = jnp.concatenate([wa, col(o_sb, o_g)], axis=1).astype(BF16)
    wn = jnp.concatenate([col(0, o_kr), col(o_sb, o_sb + sb_w), col(o_df, o_df + df_w),
                          col(o_df + 2 * df_w, o_g)], axis=1).astype(BF16)
    wt = jnp.concatenate([col(o_sb + sb_w, o_df), col(o_df + df_w, o_df + 2 * df_w),
                          col(o_kr, o_sb)], axis=1).T.astype(BF16)

    cols = np.full((heads // 2) * 2 * LANES, -1, np.int64)
    gain_src = np.full(cols.shape, -1, np.int64)
    for p in range(heads // 2):
        base = 2 * LANES * p
        for j, hd in enumerate((2 * p, 2 * p + 1)):
            hb = hd * (nope + rope)
            cols[base + nope * j:base + nope * (j + 1)] = hb + np.arange(nope)
            gain_src[base + nope * j:base + nope * (j + 1)] = np.arange(nope)
            x1 = base + 2 * nope + half * j
            x2 = base + 2 * nope + 2 * half + half * j
            cols[x1:x1 + half] = hb + nope + np.arange(half)
            cols[x2:x2 + half] = hb + nope + half + np.arange(half)
            gain_src[x1:x1 + half] = nope + np.arange(half)
            gain_src[x2:x2 + half] = nope + half + np.arange(half)
    valid = jnp.asarray(cols >= 0)
    wuq = jnp.where(valid[None, :], w["mla_w_uq"][l][:, np.maximum(cols, 0)], 0.0).astype(BF16)
    qg = jnp.concatenate([w["mla_qn_g"][l], w["mla_qr_g"][l]])
    gq = jnp.where(valid, qg[np.maximum(gain_src, 0)], 0.0) * ((nope + rope) ** -0.5 * LOG2E)

    groups = [list(range(0, nope)), list(range(nope, 2 * nope))]
    for j in range(2):
        groups.append(list(range(2 * nope + half * j, 2 * nope + half * (j + 1)))
                      + list(range(2 * nope + 2 * half + half * j, 2 * nope + 2 * half + half * (j + 1))))
    gm_mla = _group_matrix(groups, 2 * LANES)
    gm64 = _group_matrix([list(range(HEAD * j, HEAD * (j + 1))) for j in range(4)], 2 * LANES)

    rep = np.zeros((rope, LANES), np.float32)
    for j in range(half):
        rep[j, [j, half + j]] = 1.0
        rep[half + j, [2 * half + j, 3 * half + j]] = 1.0

    row = lambda v: v.reshape(1, -1).astype(F32)
    colv = lambda v: v.reshape(-1, 1).astype(F32)
    gdk = jnp.tile(w["diff_kn_g"][l], df_w // HEAD)
    gkn = jnp.tile(w["mla_kn_g"][l], heads)
    return dict(
        gmix=row(w["mix_norm_g"][l]), w1=w1, wn=wn, wt=wt, gcq=row(w["mla_q_norm_g"][l]), wuq=wuq,
        gm_mla=gm_mla, gq=row(gq), gckv=row(w["mla_kv_norm_g"][l]),
        gkr=row(jnp.pad(w["mla_kr_g"][l], (0, LANES - rope))), gkr_col=colv(w["mla_kr_g"][l]),
        gm64=gm64,
        gdq=row(jnp.tile(w["diff_qn_g"][l], df_w // HEAD) * (LOG2E / 8.0)),
        gdk=row(gdk), gdk_col=colv(gdk),
        wukv=jnp.concatenate([w["mla_w_uk"][l], w["mla_w_uv"][l]], axis=1).astype(BF16),
        wukt=w["mla_w_uk"][l].T.astype(BF16), wuvt=w["mla_w_uv"][l].T.astype(BF16),
        gkn=row(gkn), gkn_col=colv(gkn),
        rep=jnp.asarray(rep, dtype=BF16),
        lam=w["diff_lambda"][l].astype(F32), gsub=row(w["diff_subln_g"][l]),
        wg=w_in[:, o_g:].astype(BF16),
        wbr=jnp.stack([w["w_br_mla"][l], w["w_br_sb"][l], w["w_br_diff"][l]]).astype(BF16),
        wout=w["w_out"][l].astype(BF16),
        gffn=row(w["ffn_norm_g"][l]), wup=w["ffn_w_up"][l].astype(BF16),
        cw=w["ffn_conv_w"][l].astype(F32), cb=row(w["ffn_conv_b"][l]),
        wdn=w["ffn_w_down"][l].astype(BF16),
    )


def _rope_tables(pos, rope, n_rows):
    half = rope // 2
    inv = ROPE_BASE ** (-jnp.arange(half, dtype=F32) / half)
    ang = pos.astype(F32)[:, None] * inv[None, :]
    cos, sin = jnp.cos(ang), jnp.sin(ang)
    z = lambda n: jnp.zeros((pos.shape[0], n), F32)
    tab = jnp.concatenate([
        cos, cos, cos, cos, z(LANES - 4 * half),
        -sin, -sin, sin, sin, z(LANES - 4 * half),
        cos, cos, z(LANES - 2 * half),
        -sin, sin, z(LANES - 2 * half)], axis=1)
    return jnp.tile(tab, (n_rows // pos.shape[0], 1)), jnp.concatenate([cos.T, sin.T], axis=0)


def _tail(x2d, o_mla, o_sb, o_df, lw, *, seq, hist=None, layer=0):
    f2 = lambda a: a.reshape(x2d.shape[0], a.shape[-1])
    x1 = _merge(x2d, f2(o_mla), f2(o_sb), f2(o_df), lw)
    return _ffn(x1, lw, seg=seq, hist=hist, layer=layer)


def _layer_prompt(x2d, lw, dims, *, batch, seq, tabs, layer, depth, bufs, slopes):
    q, sq, dq, *bufs = _in_proj_t(x2d, lw, tabs[0], tabs[1], dims, batch=batch, seq=seq,
                                  layer=layer, depth=depth, prev=bufs)
    ckv, krt, skt, svt, dkt, dvi = bufs
    r3 = lambda a: a.reshape(batch, seq, a.shape[-1])
    kpt, mvt = _kv_up_t(ckv, krt, layer, lw, dims["rope"])
    src = lambda k, v, lyr, **kw: [dict(k=k, v=v, layer=lyr, kt=True, pos0=0, mode="self", **kw)]
    lam_init = 0.8 - 0.6 * math.exp(-0.3 * layer)
    o_mla = _attention("mla", r3(q), src(kpt[None], mvt[None], 0, vt=True), ns=2, q_pos0=0)
    o_sb = _attention("sb", r3(sq), src(skt, svt, layer, vt=True), ns=2, q_pos0=0)
    o_df = _attention("diff", r3(dq), src(dkt, dvi, layer, vt=False, vil=dims["df_w"] // LANES),
                      ns=2, q_pos0=0, extras=(slopes, lw["lam"], lw["gsub"]), lam_init=lam_init)
    x2, conv = _tail(x2d, o_mla, o_sb, o_df, lw, seq=seq)
    return x2, conv, tuple(bufs)


def _layer_sample(x2d, lw, dims, *, batch, seq, tab, layer, caches, conv_state, slopes):
    q, ckv, kr, sq, sk, sv, dq, dk, dv = _in_proj(x2d, lw, tab, dims)
    r3 = lambda a: a.reshape(batch, seq, a.shape[-1])
    r4 = lambda a: a.reshape(1, batch, seq, a.shape[-1])
    kpk, mv = _kv_up(ckv, kr, lw)
    c_ckv, c_krt, c_skt, c_svt, c_dkt, c_dv = caches
    plen = c_ckv.shape[2]
    ckpt, cmvt = _kv_up_t(c_ckv, c_krt, layer, lw, dims["rope"])

    def srcs(ck, cv, lyr, nk, nv, **kw):
        return [dict(k=ck, v=cv, layer=lyr, kt=True, pos0=0, mode="full", tk=plen, **kw),
                dict(k=r4(nk), v=r4(nv), layer=0, kt=False, vt=False, pos0=plen, mode="masked")]

    lam_init = 0.8 - 0.6 * math.exp(-0.3 * layer)
    o_mla = _attention("mla", r3(q), srcs(ckpt[None], cmvt[None], 0, kpk, mv, vt=True), ns=4, q_pos0=plen)
    o_sb = _attention("sb", r3(sq), srcs(c_skt, c_svt, layer, sk, sv, vt=True), ns=4, q_pos0=plen)
    o_df = _attention("diff", r3(dq),
                      srcs(c_dkt, c_dv, layer, dk, dv, vt=False, vil=dims["df_w"] // LANES),
                      ns=4, q_pos0=plen, extras=(slopes, lw["lam"], lw["gsub"]), lam_init=lam_init)
    x2, conv = _tail(x2d, o_mla, o_sb, o_df, lw, seq=seq, hist=conv_state, layer=layer)
    return x2, (ckv, kr, sk, sv, dk, dv), conv


def kernel(x_prompt, x_sample, cache_mla_ckv, cache_mla_krope, cache_sb_k, cache_sb_v, cache_diff_k, cache_diff_v, state_ffn_conv, mix_norm_g, w_in, mla_q_norm_g, mla_w_uq, mla_kv_norm_g, mla_w_uk, mla_w_uv, mla_qn_g, mla_kn_g, mla_qr_g, mla_kr_g, diff_qn_g, diff_kn_g, diff_lambda, diff_subln_g, w_br_mla, w_br_sb, w_br_diff, w_out, ffn_norm_g, ffn_w_up, ffn_conv_w, ffn_conv_b, ffn_w_down):
    w = dict(mix_norm_g=mix_norm_g, w_in=w_in, mla_q_norm_g=mla_q_norm_g, mla_w_uq=mla_w_uq,
             mla_kv_norm_g=mla_kv_norm_g, mla_w_uk=mla_w_uk, mla_w_uv=mla_w_uv, mla_qn_g=mla_qn_g,
             mla_kn_g=mla_kn_g, mla_qr_g=mla_qr_g, mla_kr_g=mla_kr_g, diff_qn_g=diff_qn_g,
             diff_kn_g=diff_kn_g, diff_lambda=diff_lambda, diff_subln_g=diff_subln_g,
             w_br_mla=w_br_mla, w_br_sb=w_br_sb, w_br_diff=w_br_diff, w_out=w_out,
             ffn_norm_g=ffn_norm_g, ffn_w_up=ffn_w_up, ffn_conv_w=ffn_conv_w,
             ffn_conv_b=ffn_conv_b, ffn_w_down=ffn_w_down)
    depth = w_in.shape[0]
    bp, sp, d = x_prompt.shape
    bs, ss, _ = x_sample.shape
    past_len = cache_mla_ckv.shape[2]
    sb_heads, sb_dim = cache_sb_k.shape[3], cache_sb_k.shape[4]
    df_heads, df_dim = cache_diff_k.shape[3], cache_diff_k.shape[5]
    assert sb_dim == HEAD and df_dim == HEAD and ffn_conv_w.shape[1] == 3
    nope, rope = mla_qn_g.shape[1], mla_qr_g.shape[1]
    dims = dict(d_model=d, q_rank=mla_q_norm_g.shape[1], kv_rank=mla_kv_norm_g.shape[1],
                nope=nope, rope=rope, mla_heads=mla_w_uk.shape[2] // nope,
                sb_w=sb_heads * sb_dim, df_w=2 * df_heads * df_dim)
    sb_w, df_w = dims["sb_w"], dims["df_w"]

    tabs_p = _rope_tables(jnp.arange(sp, dtype=jnp.int32), rope, max(sp, ROW_TILE))
    tab_s, _ = _rope_tables(past_len + jnp.arange(ss, dtype=jnp.int32), rope, max(ss, ROW_TILE))
    slopes = 2.0 ** (-8.0 * jnp.arange(1, df_heads + 1, dtype=F32) / df_heads) * LOG2E

    feat = lambda c: jnp.moveaxis(c, 2, -1).reshape(depth, bs, -1, past_len)
    assert 2 * df_dim == LANES
    caches = (cache_mla_ckv, feat(cache_mla_krope), feat(cache_sb_k), feat(cache_sb_v),
              feat(cache_diff_k), cache_diff_v.reshape(depth, bs, past_len * df_heads, 2 * df_dim))

    xp = x_prompt.reshape(bp * sp, d)
    xs = x_sample.reshape(bs * ss, d)
    bufs, rows_s, conv_p, conv_s = None, [], [], []
    for l in range(depth):
        lw = _pack_layer(w, l, dims)
        xp, cp, bufs = _layer_prompt(xp, lw, dims, batch=bp, seq=sp, tabs=tabs_p, layer=l,
                                     depth=depth, bufs=bufs, slopes=slopes)
        xs, rs, cs = _layer_sample(xs, lw, dims, batch=bs, seq=ss, tab=tab_s, layer=l,
                                   caches=caches, conv_state=state_ffn_conv, slopes=slopes)
        rows_s.append(rs)
        conv_p.append(cp)
        conv_s.append(cs)

    p_ckv, p_krt, p_skt, p_svt, p_dkt, p_dvi = bufs
    tok = lambda t, tail: jnp.moveaxis(t.reshape((depth, bp) + tail + (sp,)), -1, 2)
    prompt_rows = (p_ckv, tok(p_krt, (rope,)), tok(p_skt, (sb_heads, sb_dim)),
                   tok(p_svt, (sb_heads, sb_dim)), tok(p_dkt, (df_heads, 2, df_dim)),
                   p_dvi.reshape(depth, bp, sp, df_heads, 2 * df_dim))

    def stack(i, tail):
        return jnp.stack([r[i] for r in rows_s]).reshape((depth, bs, ss) + tail)

    sample_rows = (stack(0, (dims["kv_rank"],)), stack(1, (rope,)), stack(2, (sb_heads, sb_dim)),
                   stack(3, (sb_heads, sb_dim)), stack(4, (df_heads, 2, df_dim)),
                   stack(5, (df_heads, 2 * df_dim)))
    return ((xp.reshape(bp, sp, d), xs.reshape(bs, ss, d)) + prompt_rows + (jnp.stack(conv_p),)
            + sample_rows + (jnp.stack(conv_s),))
```

```python
import functools
import math

import numpy as np
import jax
import jax.numpy as jnp
from jax import lax
from jax.experimental import pallas as pl
from jax.experimental.pallas import tpu as pltpu

F32 = jnp.float32
BF16 = jnp.bfloat16

EPS = 1e-6
CHUNK = 64
CHUNK_SHIFT = CHUNK.bit_length() - 1
assert 1 << CHUNK_SHIFT == CHUNK
ROPE_BASE = 10000.0
NEG_BIG = -1e30
HEAD = 64

LANES = 128
ROW_TILE = 512
KV_ROW_TILE = 1024
ATTN_TILE = 512
SB_BLOCK = 256
FF_CHUNK = 512
SB_SKIP_LOG2 = -160.0
LOG2E = math.log2(math.e)
VMEM_LIMIT = 52 * 1024 * 1024


def _cparams(n_axes):
    return pltpu.CompilerParams(dimension_semantics=("arbitrary",) * n_axes,
                                vmem_limit_bytes=VMEM_LIMIT)


def _const_spec(shape):
    nd = len(shape)
    return pl.BlockSpec(shape, lambda *_: (0,) * nd, pipeline_mode=pl.Buffered(1))


def _dot(a, b):
    return jnp.dot(a, b, preferred_element_type=F32)


def _dot_nt(a, b):
    return lax.dot_general(a, b, (((1,), (1,)), ((), ())), preferred_element_type=F32)


def _rmsnorm(x, g):
    return x * lax.rsqrt(jnp.mean(x * x, axis=-1, keepdims=True) + EPS) * g


def _group_mean_sq(x, gmat):
    sq = x * x
    hi = sq.astype(BF16)
    lo = (sq - hi.astype(F32)).astype(BF16)
    return _dot(hi, gmat) + _dot(lo, gmat)


def _row_group_rmsnorm(xt, gcol):
    r, n = xt.shape
    x3 = xt.reshape(r // HEAD, HEAD, n)
    y3 = x3 * lax.rsqrt(jnp.mean(x3 * x3, axis=1, keepdims=True) + EPS)
    return y3.reshape(r, n) * gcol


def _sigmoid(x):
    return 1.0 / (1.0 + jnp.exp(-x))


def _mla_query(cq, gcq_ref, wuq_ref, gm_mla_ref, gq_ref, cos_q, sin_q, q_ref, half):
    tm = cq.shape[0]
    cqn = _rmsnorm(cq, gcq_ref[...]).astype(BF16)
    q = _dot(cqn, wuq_ref[...])
    lane = lax.broadcasted_iota(jnp.int32, (tm, LANES), 1)
    for p in range(q.shape[1] // (2 * LANES)):
        lo = 2 * LANES * p
        blk = q[:, lo:lo + 2 * LANES]
        y = blk * lax.rsqrt(_group_mean_sq(blk, gm_mla_ref[...]) + EPS) * gq_ref[:, lo:lo + 2 * LANES]
        rp = y[:, LANES:]
        partner = jnp.where(lane < 2 * half, pltpu.roll(rp, LANES - 2 * half, 1),
                            pltpu.roll(rp, 2 * half, 1))
        q_ref[:, lo:lo + LANES] = y[:, :LANES].astype(BF16)
        q_ref[:, lo + LANES:lo + 2 * LANES] = (rp * cos_q + partner * sin_q).astype(BF16)


def _diff_query(zq, gm64_ref, gdq_ref, dq_ref):
    for s in range(zq.shape[1] // (2 * LANES)):
        lo = 2 * LANES * s
        blk = zq[:, lo:lo + 2 * LANES]
        r = lax.rsqrt(_group_mean_sq(blk, gm64_ref[...]) + EPS)
        dq_ref[:, lo:lo + 2 * LANES] = (blk * r * gdq_ref[:, lo:lo + 2 * LANES]).astype(BF16)


def _in_proj_kernel(x_ref, gmix_ref, w_ref, gcq_ref, wuq_ref, gm_mla_ref, gq_ref, gckv_ref,
                    gkr_ref, tab_ref, gm64_ref, gdq_ref, gdk_ref,
                    q_ref, ckv_ref, kr_ref, sq_ref, sk_ref, sv_ref, dq_ref, dk_ref, dv_ref,
                    *, q_rank, kv_rank, rope, sb_w, df_w):
    tm = x_ref.shape[0]
    half = rope // 2
    h = _rmsnorm(x_ref[...], gmix_ref[...]).astype(BF16)
    tab = tab_ref[...]
    cos_k, sin_k = tab[:, 2 * LANES:3 * LANES], tab[:, 3 * LANES:4 * LANES]
    lane = lax.broadcasted_iota(jnp.int32, (tm, LANES), 1)

    za = _dot(h, w_ref[:, 0:4 * LANES])
    _mla_query(za[:, 0:q_rank], gcq_ref, wuq_ref, gm_mla_ref, gq_ref, tab[:, 0:LANES],
               tab[:, LANES:2 * LANES], q_ref, half)
    ckv_ref[...] = _rmsnorm(za[:, q_rank:q_rank + kv_rank], gckv_ref[...])

    kr = za[:, q_rank + kv_rank:4 * LANES]
    ms = jnp.sum(kr * kr, axis=-1, keepdims=True) * (1.0 / rope)
    krn = kr * lax.rsqrt(ms + EPS) * gkr_ref[...]
    partner = jnp.where(lane < half, pltpu.roll(krn, LANES - half, 1), pltpu.roll(krn, half, 1))
    kr_ref[...] = (krn * cos_k + partner * sin_k)[:, :rope]

    o = 4 * LANES
    zs = _dot(h, w_ref[:, o:o + 3 * sb_w])
    sq_ref[...] = (zs[:, 0:sb_w] * (LOG2E / 8.0)).astype(BF16)
    sk_ref[...] = zs[:, sb_w:2 * sb_w]
    sv_ref[...] = zs[:, 2 * sb_w:3 * sb_w]

    o = o + 3 * sb_w
    zd = _dot(h, w_ref[:, o:o + 3 * df_w])
    _diff_query(zd[:, 0:df_w], gm64_ref, gdq_ref, dq_ref)
    for s in range(df_w // (2 * LANES)):
        lo = 2 * LANES * s
        blk = zd[:, df_w + lo:df_w + lo + 2 * LANES]
        r = lax.rsqrt(_group_mean_sq(blk, gm64_ref[...]) + EPS)
        dk_ref[:, lo:lo + 2 * LANES] = blk * r * gdk_ref[:, lo:lo + 2 * LANES]
    dv_ref[...] = zd[:, 2 * df_w:3 * df_w]


def _in_proj(x2d, lw, tab, dims):
    t, d = x2d.shape
    tm = ROW_TILE
    assert t % tm == 0 and tab.shape[0] % tm == 0
    n_tab = tab.shape[0] // tm
    row = lambda w: pl.BlockSpec((tm, w), lambda i: (i, 0))
    consts = [lw["gmix"], lw["w1"], lw["gcq"], lw["wuq"], lw["gm_mla"], lw["gq"], lw["gckv"], lw["gkr"]]
    consts2 = [lw["gm64"], lw["gdq"], lw["gdk"]]
    sb_w, df_w = dims["sb_w"], dims["df_w"]
    out_shape = [
        jax.ShapeDtypeStruct((t, lw["wuq"].shape[1]), BF16),
        jax.ShapeDtypeStruct((t, dims["kv_rank"]), F32),
        jax.ShapeDtypeStruct((t, dims["rope"]), F32),
        jax.ShapeDtypeStruct((t, sb_w), BF16),
        jax.ShapeDtypeStruct((t, sb_w), F32),
        jax.ShapeDtypeStruct((t, sb_w), F32),
        jax.ShapeDtypeStruct((t, df_w), BF16),
        jax.ShapeDtypeStruct((t, df_w), F32),
        jax.ShapeDtypeStruct((t, df_w), F32),
    ]
    kern = functools.partial(_in_proj_kernel, q_rank=dims["q_rank"], kv_rank=dims["kv_rank"],
                             rope=dims["rope"], sb_w=sb_w, df_w=df_w)
    return pl.pallas_call(
        kern,
        grid=(t // tm,),
        in_specs=([row(d)] + [_const_spec(c.shape) for c in consts]
                  + [pl.BlockSpec((tm, tab.shape[1]), lambda i: (i % n_tab, 0))]
                  + [_const_spec(c.shape) for c in consts2]),
        out_specs=[row(s.shape[1]) for s in out_shape],
        out_shape=out_shape,
        compiler_params=_cparams(1),
        name="in_proj",
    )(x2d, *consts, tab, *consts2)


def _in_proj_t_kernel(*refs, q_rank, kv_rank, rope, sb_w, df_w, n_alias):
    (x_ref, gmix_ref, wn_ref, wt_ref, gcq_ref, wuq_ref, gm_mla_ref, gq_ref, gckv_ref, gkr_ref,
     tab_ref, tabt_ref, gm64_ref, gdq_ref, gdk_ref) = refs[:15]
    (q_ref, sq_ref, dq_ref, ckv_ref, krt_ref, skt_ref, svt_ref, dkt_ref, dvi_ref) = refs[15 + n_alias:]
    half = rope // 2
    tm = x_ref.shape[0]
    h = _rmsnorm(x_ref[...], gmix_ref[...]).astype(BF16)
    tab = tab_ref[...]

    o = q_rank + kv_rank
    za = _dot(h, wn_ref[:, 0:o])
    _mla_query(za[:, 0:q_rank], gcq_ref, wuq_ref, gm_mla_ref, gq_ref, tab[:, 0:LANES],
               tab[:, LANES:2 * LANES], q_ref, half)
    ckv_ref[0, 0] = _rmsnorm(za[:, q_rank:o], gckv_ref[...])

    sq_ref[...] = (_dot(h, wn_ref[:, o:o + sb_w]) * (LOG2E / 8.0)).astype(BF16)
    o = o + sb_w
    zd = _dot(h, wn_ref[:, o:o + 2 * df_w])
    _diff_query(zd[:, 0:df_w], gm64_ref, gdq_ref, dq_ref)
    n_dv = df_w // LANES
    for hh in range(n_dv):
        dvi_ref[0, 0, pl.ds(hh, tm, stride=n_dv), :] = zd[:, df_w + LANES * hh:df_w + LANES * (hh + 1)]

    skt_ref[0, 0] = _dot_nt(wt_ref[0:sb_w, :], h)
    svt_ref[0, 0] = _dot_nt(wt_ref[sb_w:2 * sb_w, :], h)
    o = 2 * sb_w
    dkt_ref[0, 0] = _row_group_rmsnorm(_dot_nt(wt_ref[o:o + df_w, :], h), gdk_ref[...])
    o = o + df_w
    krt = _dot_nt(wt_ref[o:o + rope, :], h)
    krn = krt * lax.rsqrt(jnp.mean(krt * krt, axis=0, keepdims=True) + EPS) * gkr_ref[...]
    x1, x2 = krn[:half], krn[half:]
    cos_t, sin_t = tabt_ref[0:half, :], tabt_ref[half:rope, :]
    krt_ref[0, 0] = jnp.concatenate([x1 * cos_t - x2 * sin_t, x1 * sin_t + x2 * cos_t], axis=0)


def _in_proj_t(x2d, lw, tab, tabt, dims, *, batch, seq, layer, depth, prev):
    t, d = x2d.shape
    tm = ROW_TILE
    assert seq % tm == 0 and t == batch * seq
    nst = seq // tm
    row = lambda w: pl.BlockSpec((tm, w), lambda i: (i, 0))
    consts = [lw["gmix"], lw["wn"], lw["wt"], lw["gcq"], lw["wuq"], lw["gm_mla"], lw["gq"], lw["gckv"],
              lw["gkr_col"]]
    consts2 = [lw["gm64"], lw["gdq"], lw["gdk_col"]]
    sb_w, df_w, rope, kvr = dims["sb_w"], dims["df_w"], dims["rope"], dims["kv_rank"]
    n_alias = 0 if prev is None else len(prev)
    in_specs = ([row(d)] + [_const_spec(c.shape) for c in consts]
                + [pl.BlockSpec((tm, tab.shape[1]), lambda i: (i % nst, 0)),
                   pl.BlockSpec((rope, tm), lambda i: (0, i % nst))]
                + [_const_spec(c.shape) for c in consts2]
                + [pl.BlockSpec(memory_space=pl.ANY)] * n_alias)
    n_plain = len(in_specs) - n_alias
    tok = lambda w: pl.BlockSpec((1, 1, tm, w), lambda i: (layer, i // nst, i % nst, 0))
    feat = lambda r: pl.BlockSpec((1, 1, r, tm), lambda i: (layer, i // nst, 0, i % nst))
    n_dv = df_w // LANES
    out_shape = [
        jax.ShapeDtypeStruct((t, lw["wuq"].shape[1]), BF16),
        jax.ShapeDtypeStruct((t, sb_w), BF16),
        jax.ShapeDtypeStruct((t, df_w), BF16),
        jax.ShapeDtypeStruct((depth, batch, seq, kvr), F32),
        jax.ShapeDtypeStruct((depth, batch, rope, seq), F32),
        jax.ShapeDtypeStruct((depth, batch, sb_w, seq), F32),
        jax.ShapeDtypeStruct((depth, batch, sb_w, seq), F32),
        jax.ShapeDtypeStruct((depth, batch, df_w, seq), F32),
        jax.ShapeDtypeStruct((depth, batch, seq * n_dv, LANES), F32),
    ]
    out_specs = [row(out_shape[0].shape[1]), row(sb_w), row(df_w),
                 tok(kvr), feat(rope), feat(sb_w), feat(sb_w), feat(df_w),
                 pl.BlockSpec((1, 1, tm * n_dv, LANES), lambda i: (layer, i // nst, i % nst, 0))]
    kern = functools.partial(_in_proj_t_kernel, q_rank=dims["q_rank"], kv_rank=kvr, rope=rope,
                             sb_w=sb_w, df_w=df_w, n_alias=n_alias)
    return pl.pallas_call(
        kern,
        grid=(t // tm,),
        in_specs=in_specs,
        out_specs=out_specs,
        out_shape=out_shape,
        input_output_aliases={n_plain + j: 3 + j for j in range(n_alias)},
        compiler_params=_cparams(1),
        name="in_proj_t",
    )(x2d, *consts, tab, tabt, *consts2, *(prev or ()))


def _kv_up_kernel(ckv_ref, kr_ref, wukv_ref, gm64_ref, gkn_ref, rep_ref, k_ref, v_ref, *, kn_w):
    c = ckv_ref[...].astype(BF16)
    kv = _dot(c, wukv_ref[...])
    krrep = _dot(kr_ref[...].astype(BF16), rep_ref[...]).astype(BF16)
    for b in range(kn_w // (2 * LANES)):
        s = 2 * LANES * b
        blk = kv[:, s:s + 2 * LANES]
        r = lax.rsqrt(_group_mean_sq(blk, gm64_ref[...]) + EPS)
        kn = (blk * r * gkn_ref[:, s:s + 2 * LANES]).astype(BF16)
        for j in range(2):
            p = 2 * b + j
            k_ref[:, 2 * LANES * p:2 * LANES * p + LANES] = kn[:, LANES * j:LANES * (j + 1)]
            k_ref[:, 2 * LANES * p + LANES:2 * LANES * (p + 1)] = krrep
    v_ref[...] = kv[:, kn_w:].astype(BF16)


def _kv_up(ckv2d, kr2d, lw):
    r, kvr = ckv2d.shape
    tr = min(KV_ROW_TILE, r)
    assert r % tr == 0
    kn_w = lw["gkn"].shape[1]
    consts = [lw["wukv"], lw["gm64"], lw["gkn"], lw["rep"]]
    out_shape = [jax.ShapeDtypeStruct((r, 2 * kn_w), BF16),
                 jax.ShapeDtypeStruct((r, lw["wukv"].shape[1] - kn_w), BF16)]
    return pl.pallas_call(
        functools.partial(_kv_up_kernel, kn_w=kn_w),
        grid=(r // tr,),
        in_specs=[pl.BlockSpec((tr, kvr), lambda i: (i, 0)),
                  pl.BlockSpec((tr, kr2d.shape[1]), lambda i: (i, 0))]
                 + [_const_spec(c.shape) for c in consts],
        out_specs=[pl.BlockSpec((tr, s.shape[1]), lambda i: (i, 0)) for s in out_shape],
        out_shape=out_shape,
        compiler_params=_cparams(1),
        name="mla_kv_up",
    )(ckv2d, kr2d, *consts)


def _mla_kv_feature_major(c, kr, wukt, wuvt, gkn, kt_ref, vt_ref):
    half = kr.shape[0] // 2
    n = c.shape[0]
    knt = _row_group_rmsnorm(_dot_nt(wukt, c), gkn).astype(BF16)
    x1, x2 = kr[:half], kr[half:]
    krrep = jnp.concatenate([x1, x1, x2, x2, jnp.zeros((LANES - 4 * half, n), BF16)], axis=0)
    for p in range(knt.shape[0] // LANES):
        kt_ref[2 * LANES * p:2 * LANES * p + LANES, :] = knt[LANES * p:LANES * (p + 1)]
        kt_ref[2 * LANES * p + LANES:2 * LANES * (p + 1), :] = krrep
    vt_ref[...] = _dot_nt(wuvt, c).astype(BF16)


def _kv_up_t_kernel(ckv_ref, krt_ref, wukt_ref, wuvt_ref, gkn_ref, kt_ref, vt_ref):
    _mla_kv_feature_major(ckv_ref[0, 0].astype(BF16), krt_ref[0, 0].astype(BF16), wukt_ref[...],
                          wuvt_ref[...], gkn_ref[...], kt_ref.at[0], vt_ref.at[0])


def _kv_up_t(ckv4, krt4, layer, lw, rope):
    _, b, s, kvr = ckv4.shape
    tr = min(KV_ROW_TILE, s)
    assert s % tr == 0
    consts = [lw["wukt"], lw["wuvt"], lw["gkn_col"]]
    kn_w, v_w = lw["wukt"].shape[0], lw["wuvt"].shape[0]
    out_shape = [jax.ShapeDtypeStruct((b, 2 * kn_w, s), BF16), jax.ShapeDtypeStruct((b, v_w, s), BF16)]
    return pl.pallas_call(
        _kv_up_t_kernel,
        grid=(b, s // tr),
        in_specs=[pl.BlockSpec((1, 1, tr, kvr), lambda bi, i: (layer, bi, i, 0)),
                  pl.BlockSpec((1, 1, rope, tr), lambda bi, i: (layer, bi, 0, i))]
                 + [_const_spec(c.shape) for c in consts],
        out_specs=[pl.BlockSpec((1, o.shape[1], tr), lambda bi, i: (bi, 0, i)) for o in out_shape],
        out_shape=out_shape,
        compiler_params=_cparams(2),
        name="mla_kv_up_t",
    )(ckv4, krt4, *consts)


def _stream_lane_masks(kind, ns, width):
    lane = lax.broadcasted_iota(jnp.int32, (1, width), 1)
    masks = []
    for i in range(ns):
        if kind == "mla":
            r = lane - 2 * LANES * (i // 2)
            j = i % 2
            masks.append(((r >= HEAD * j) & (r < HEAD * (j + 1)))
                         | ((r >= 128 + 16 * j) & (r < 128 + 16 * (j + 1)))
                         | ((r >= 160 + 16 * j) & (r < 160 + 16 * (j + 1))))
        else:
            masks.append((lane >= HEAD * i) & (lane < HEAD * (i + 1)))
    return masks


def _attn_kernel(*refs, kind, srcs, ns, tq, q_pos0, lam_init):
    it = iter(refs)
    slopes_ref = next(it) if kind == "diff" else None
    q_ref = next(it)
    kv_refs = [tuple(next(it) for _ in range(5 if s["latent"] else 2)) for s in srcs]
    if kind == "diff":
        lam_ref, gsub_ref = next(it), next(it)
    o_ref = next(it)
    scr = [(next(it), next(it)) if s["cast"] else None for s in srcs]

    grp = pl.program_id(1)
    qi = pl.program_id(2)
    rows = ns * tq
    vw = HEAD * ns

    for s, sc, in_refs in zip(srcs, scr, kv_refs):
        if s["latent"]:
            @pl.when(qi == 0)
            def _(s=s, sc=sc, in_refs=in_refs):
                ckv_ref, krt_ref, wukt_ref, wuvt_ref, gkn_ref = in_refs
                _mla_kv_feature_major(ckv_ref[0, 0].astype(BF16), krt_ref[0, 0].astype(BF16), wukt_ref[...],
                                      wuvt_ref[...], gkn_ref[...], sc[0], sc[1])
        elif sc is not None:
            k_ref, v_ref = in_refs

            @pl.when(qi == 0)
            def _(s=s, sc=sc, k_ref=k_ref, v_ref=v_ref):
                sc[0][...] = k_ref[0, 0].astype(BF16)
                if s["vil"]:
                    for j in range(vw // LANES):
                        rows_j = pl.ds(grp * (vw // LANES) + j, s["sk"], stride=s["vil"])
                        sc[1][:, LANES * j:LANES * (j + 1)] = v_ref[0, 0, rows_j, :].astype(BF16)
                else:
                    sc[1][...] = v_ref[0, 0].astype(BF16)

    q = q_ref[0]
    zero = jnp.zeros_like(q)
    qs = jnp.concatenate([jnp.where(m, q, zero) for m in _stream_lane_masks(kind, ns, q.shape[1])],
                         axis=0)
    q_start = q_pos0 + qi * tq
    qpos = jnp.concatenate([q_start + lax.broadcasted_iota(jnp.int32, (tq, 1), 0)] * ns, axis=0)
    if kind == "diff":
        slope = jnp.concatenate(
            [jnp.full((tq, 1), slopes_ref[grp * (ns // 2) + i // 2], F32) for i in range(ns)], axis=0)

    if kind == "sb":
        init = (jnp.zeros((rows, 1), F32), jnp.zeros((rows, vw), F32))
    else:
        init = (jnp.full((rows, 1), NEG_BIG, F32), jnp.zeros((rows, 1), F32),
                jnp.zeros((rows, vw), F32))

    def tile_step(st, k_t, v_t, k_start, src, masked, upper):
        tk = src["tk"]
        kpos = k_start + lax.broadcasted_iota(jnp.int32, (1, tk), 1)
        s = _dot(qs, k_t) if src["kt"] else _dot_nt(qs, k_t)
        pv = (lambda p: _dot_nt(p, v_t)) if src["vt"] else (lambda p: _dot(p, v_t))
        if masked:
            valid = (kpos < qpos) if kind == "sb" else ((kpos >> CHUNK_SHIFT) <= (qpos >> CHUNK_SHIFT))
        if kind == "sb":
            c, acc = st
            lp = jnp.log(1.0 + jnp.exp2(-jnp.abs(s))) * LOG2E
            ls = jnp.minimum(s, 0.0) - lp
            l1 = ls - s
            if masked:
                l1 = jnp.where(valid, l1, 0.0)
            hi = l1.astype(BF16)
            lo = (l1 - hi.astype(F32)).astype(BF16)
            cb = upper.shape[0]
            nb = tk // cb
            if nb > 1:
                hi = jnp.concatenate([hi[:, cb * j:cb * (j + 1)] for j in range(nb)], axis=0)
                lo = jnp.concatenate([lo[:, cb * j:cb * (j + 1)] for j in range(nb)], axis=0)
            suffix = _dot(hi, upper) + _dot(lo, upper)
            parts = [None] * nb
            for j in reversed(range(nb)):
                parts[j] = jnp.exp2(ls[:, cb * j:cb * (j + 1)] + suffix[rows * j:rows * (j + 1)] + c)
                c = c + jnp.sum(l1[:, cb * j:cb * (j + 1)], axis=-1, keepdims=True)
            a = parts[0] if nb == 1 else jnp.concatenate(parts, axis=1)
            if masked:
                a = jnp.where(valid, a, 0.0)
            return (c, acc + pv(a.astype(BF16)))
        m, l, acc = st
        if kind == "diff":
            dist = qpos - kpos
            s = s - slope * (jnp.abs(dist) if masked else dist).astype(F32)
        if masked:
            s = jnp.where(valid, s, -jnp.inf)
        m_new = jnp.maximum(m, jnp.max(s, axis=-1, keepdims=True))
        alpha = jnp.exp2(m - m_new)
        p = jnp.exp2(s - m_new)
        l = alpha * l + jnp.sum(p, axis=-1, keepdims=True)
        return (m_new, l, alpha * acc + pv(p.astype(BF16)))

    def upper_tri(tk):
        if kind != "sb":
            return None
        cb = min(tk, SB_BLOCK)
        r = lax.broadcasted_iota(jnp.int32, (cb, cb), 0)
        c = lax.broadcasted_iota(jnp.int32, (cb, cb), 1)
        return jnp.where(r > c, 1.0, 0.0).astype(BF16)

    st = init
    for s, sc, in_refs in reversed(list(zip(srcs, scr, kv_refs))):
        k_ref, v_ref = (None, None) if s["latent"] else in_refs
        tk = s["tk"]
        upper = upper_tri(tk)

        def load(k_off, src, _sc=sc, _k=k_ref, _v=v_ref):
            def one(ref4, ref2, transposed):
                win = pl.ds(k_off, src["tk"])
                if ref2 is not None:
                    return ref2[:, win] if transposed else ref2[win, :]
                return ref4[0, 0, :, win] if transposed else ref4[0, 0, win, :]
            return (one(_k, _sc and _sc[0], src["kt"]), one(_v, _sc and _sc[1], src["vt"]))

        if s["mode"] == "self":
            k_t, v_t = load(pl.multiple_of(qi * tk, tk), s)
            st = tile_step(st, k_t, v_t, s["pos0"] + qi * tk, s, True, upper)
            if kind == "sb" and tk > SB_BLOCK:
                s = dict(s, tk=SB_BLOCK)
            n_full = qi * (tk // s["tk"])
        elif s["mode"] == "masked":
            k_t, v_t = load(0, s)
            st = tile_step(st, k_t, v_t, s["pos0"], s, True, upper)
            continue
        else:
            n_full = s["sk"] // tk

        def body(i, st_, _s=s, _load=load, _n=n_full, _upper=upper):
            kj = _n - 1 - i
            k_t, v_t = _load(pl.multiple_of(kj * _s["tk"], _s["tk"]), _s)
            return tile_step(st_, k_t, v_t, _s["pos0"] + kj * _s["tk"], _s, False, _upper)

        if isinstance(n_full, int) and n_full <= 2:
            for i in range(n_full):
                st = body(i, st)
        elif kind == "sb":
            def live(st_):
                return (jnp.max(st_[0]) > SB_SKIP_LOG2).astype(jnp.int32)

            def wbody(carry, _body=body):
                i, _, st_ = carry
                st_ = _body(i, st_)
                return i + 1, live(st_), st_

            _, _, st = lax.while_loop(lambda carry, _n=n_full: (carry[0] < _n) & (carry[1] > 0),
                                      wbody, (jnp.int32(0), live(st), st))
        else:
            st = lax.fori_loop(0, n_full, body, st)

    acc = st[-1]
    if kind == "diff":
        l = st[1]
        lv = lam_ref[...]
        lam = (jnp.exp(jnp.sum(lv[0:1] * lv[1:2], axis=-1, keepdims=True))
               - jnp.exp(jnp.sum(lv[2:3] * lv[3:4], axis=-1, keepdims=True)) + lam_init)
        for hh in range(ns // 2):
            lanes = slice(2 * HEAD * hh, 2 * HEAD * (hh + 1))
            r0, r1 = slice(tq * 2 * hh, tq * (2 * hh + 1)), slice(tq * (2 * hh + 1), tq * (2 * hh + 2))
            o = acc[r0, lanes] / l[r0] - lam * (acc[r1, lanes] / l[r1])
            o_ref[0, :, lanes] = (_rmsnorm(o, gsub_ref[...]) * (1.0 - lam_init)).astype(BF16)
    else:
        lane = lax.broadcasted_iota(jnp.int32, (tq, vw), 1)
        o = jnp.zeros((tq, vw), F32)
        for i in range(ns):
            blk = acc[tq * i:tq * (i + 1)]
            if kind == "mla":
                blk = blk / st[1][tq * i:tq * (i + 1)]
            o = jnp.where((lane >= HEAD * i) & (lane < HEAD * (i + 1)), blk, o)
        o_ref[0] = o.astype(BF16)


def _attention(kind, q3, sources, *, ns, q_pos0, extras=(), lam_init=0.0):
    b, sq, qtot = q3.shape
    w = (2 * HEAD if kind == "mla" else HEAD) * ns
    vw = HEAD * ns
    n_groups = qtot // w
    tq = min(ATTN_TILE, sq)
    assert sq % tq == 0 and qtot % w == 0
    srcs, in_specs, args, scratch = [], [], [], []
    if kind == "diff":
        in_specs.append(pl.BlockSpec(memory_space=pltpu.SMEM))
        args.append(extras[0])
    in_specs.append(pl.BlockSpec((1, tq, w), lambda bi, g, qi: (bi, qi, g)))
    args.append(q3)

    def kv_spec(arr, width, transposed, layer):
        if transposed:
            assert arr.shape[2] == n_groups * width
            return pl.BlockSpec((1, 1, width, arr.shape[3]), lambda bi, g, qi: (layer, bi, g, 0))
        assert arr.shape[3] == n_groups * width
        return pl.BlockSpec((1, 1, arr.shape[2], width), lambda bi, g, qi: (layer, bi, 0, g))

    for s in sources:
        lat = s.get("latent")
        if lat is not None:
            assert kind == "mla" and s["mode"] == "full"
            ckv, krt, layer = lat["ckv"], lat["krt"], s["layer"]
            sk = ckv.shape[2]
            srcs.append(dict(mode="full", pos0=s["pos0"], sk=sk, tk=s["tk"], cast=True, kt=True, vt=True,
                             vil=0, latent=True))
            in_specs += [pl.BlockSpec((1, 1, sk, ckv.shape[3]), lambda bi, g, qi, _l=layer: (_l, bi, 0, 0)),
                         pl.BlockSpec((1, 1, krt.shape[2], sk), lambda bi, g, qi, _l=layer: (_l, bi, 0, 0)),
                         pl.BlockSpec((w // 2, ckv.shape[3]), lambda bi, g, qi: (g, 0)),
                         pl.BlockSpec((vw, ckv.shape[3]), lambda bi, g, qi: (g, 0)),
                         pl.BlockSpec((w // 2, 1), lambda bi, g, qi: (g, 0))]
            args += [ckv, krt, lat["wukt"], lat["wuvt"], lat["gkn"]]
            scratch += [pltpu.VMEM((w, sk), BF16), pltpu.VMEM((vw, sk), BF16)]
            continue
        k, v, kt, vt = s["k"], s["v"], s["kt"], s["vt"]
        sk = k.shape[3] if kt else k.shape[2]
        tk = tq if s["mode"] == "self" else (sk if s["mode"] == "masked" else s.get("tk", min(ATTN_TILE, sk)))
        assert sk % tk == 0 and (tk <= SB_BLOCK or tk % SB_BLOCK == 0)
        cast = k.dtype != BF16
        assert cast == (v.dtype != BF16)
        vil = s.get("vil", 0)
        vlayer = s.get("vlayer", s["layer"])
        srcs.append(dict(mode=s["mode"], pos0=s["pos0"], sk=sk, tk=tk, cast=cast, kt=kt, vt=vt, vil=vil,
                         latent=False))
        if vil:
            assert cast and not vt and v.shape[2:] == (sk * vil, LANES) and vw % LANES == 0
            v_spec = pl.BlockSpec((1, 1, sk * vil, LANES), lambda bi, g, qi, _l=vlayer: (_l, bi, 0, 0))
        else:
            v_spec = kv_spec(v, vw, vt, vlayer)
        in_specs += [kv_spec(k, w, kt, s["layer"]), v_spec]
        args += [k, v]
        if cast:
            scratch += [pltpu.VMEM((w, sk) if kt else (sk, w), BF16),
                        pltpu.VMEM((vw, sk) if vt else (sk, vw), BF16)]
    if kind == "diff":
        in_specs += [_const_spec(extras[1].shape), _const_spec(extras[2].shape)]
        args += [extras[1], extras[2]]
    kern = functools.partial(_attn_kernel, kind=kind, srcs=srcs, ns=ns, tq=tq, q_pos0=q_pos0,
                             lam_init=lam_init)
    return pl.pallas_call(
        kern,
        grid=(b, n_groups, sq // tq),
        in_specs=in_specs,
        out_specs=pl.BlockSpec((1, tq, vw), lambda bi, g, qi: (bi, qi, g)),
        out_shape=jax.ShapeDtypeStruct((b, sq, n_groups * vw), BF16),
        scratch_shapes=scratch,
        compiler_params=_cparams(3),
        name="attn_" + kind,
    )(*args)


def _merge_kernel(x_ref, gmix_ref, wg_ref, om_ref, os_ref, od_ref, wbr_ref, wout_ref, o_ref):
    x = x_ref[...]
    d = x.shape[1]
    h = _rmsnorm(x, gmix_ref[...]).astype(BF16)
    merged = None
    for i, br_ref in enumerate((om_ref, os_ref, od_ref)):
        gate = _sigmoid(_dot(h, wg_ref[:, i * d:(i + 1) * d]))
        term = gate * _dot(br_ref[...], wbr_ref[i])
        merged = term if merged is None else merged + term
    o_ref[...] = x + _dot(merged.astype(BF16), wout_ref[...])


def _merge(x2d, o_mla, o_sb, o_diff, lw):
    t, d = x2d.shape
    tm = ROW_TILE
    row = lambda w: pl.BlockSpec((tm, w), lambda i: (i, 0))
    return pl.pallas_call(
        _merge_kernel,
        grid=(t // tm,),
        in_specs=[row(d), _const_spec(lw["gmix"].shape), _const_spec(lw["wg"].shape),
                  row(o_mla.shape[1]), row(o_sb.shape[1]), row(o_diff.shape[1]),
                  _const_spec(lw["wbr"].shape), _const_spec(lw["wout"].shape)],
        out_specs=row(d),
        out_shape=jax.ShapeDtypeStruct((t, d), F32),
        compiler_params=_cparams(1),
        name="merge",
    )(x2d, lw["gmix"], lw["wg"], o_mla, o_sb, o_diff, lw["wbr"], lw["wout"])


def _conv_rows(a, h0, h1, cw, cb):
    n = a.shape[0]
    row = lax.broadcasted_iota(jnp.int32, (n, 1), 0)
    p1 = jnp.where(row == 0, h1, pltpu.roll(a, 1, 0))
    p2 = jnp.where(row == 0, h0, jnp.where(row == 1, h1, pltpu.roll(a, 2, 0)))
    return p2 * cw[0:1] + p1 * cw[1:2] + a * cw[2:3] + cb


def _ffn_kernel(*refs, seg, d_ff, has_hist):
    it = iter(refs)
    x_ref, g_ref, wup_ref, cw_ref, cb_ref = next(it), next(it), next(it), next(it), next(it)
    hist_ref = next(it) if has_hist else None
    wdn_ref, o_ref, conv_ref = next(it), next(it), next(it)
    carry_ref = next(it) if not has_hist else None
    act_ref = next(it) if has_hist else None

    x = x_ref[...]
    tm = x.shape[0]
    h = _rmsnorm(x, g_ref[...]).astype(BF16)
    cw, cb = cw_ref[...], cb_ref[...]

    if not has_hist:
        i = pl.program_id(0)
        tiles_per_seq = seg // tm

        @pl.when((i % tiles_per_seq) == 0)
        def _():
            carry_ref[...] = jnp.zeros_like(carry_ref)

        tail = carry_ref[...]
        y, tails = x, []
        chunks = [(c0, min(FF_CHUNK, d_ff - c0)) for c0 in range(0, d_ff, FF_CHUNK)]
        up = lambda c0, cs: (_dot(h, wup_ref[:, c0:c0 + cs]), _dot(h, wup_ref[:, d_ff + c0:d_ff + c0 + cs]))
        nxt = up(*chunks[0])
        for idx, (c0, cs) in enumerate(chunks):
            a, u = nxt
            if idx + 1 < len(chunks):
                nxt = up(*chunks[idx + 1])
            c = _conv_rows(a, tail[6:7, c0:c0 + cs], tail[7:8, c0:c0 + cs], cw[:, c0:c0 + cs],
                           cb[:, c0:c0 + cs])
            tails.append(a[tm - 8:tm])
            y = y + _dot((c * _sigmoid(c) * u).astype(BF16), wdn_ref[c0:c0 + cs, :])
        new_tail = jnp.concatenate(tails, axis=1)
        carry_ref[...] = new_tail

        @pl.when((i % tiles_per_seq) == tiles_per_seq - 1)
        def _():
            conv_ref[0] = new_tail[6:8]

        o_ref[...] = y
    else:
        a = _dot(h, wup_ref[:, 0:d_ff])
        u = _dot(h, wup_ref[:, d_ff:2 * d_ff])
        for s in range(tm // seg):
            a_s = a[s * seg:(s + 1) * seg]
            hs = hist_ref[0, s]
            c = _conv_rows(a_s, hs[0:1], hs[1:2], cw, cb)
            act_ref[s * seg:(s + 1) * seg, :] = (c * _sigmoid(c) * u[s * seg:(s + 1) * seg]).astype(BF16)
            conv_ref[s] = a_s[seg - 2:seg]
        o_ref[...] = x + _dot(act_ref[...], wdn_ref[...])


def _ffn(x2d, lw, *, seg, hist=None, layer=0):
    t, d = x2d.shape
    tm = ROW_TILE
    d_ff = lw["wdn"].shape[0]
    has_hist = hist is not None
    row = lambda w: pl.BlockSpec((tm, w), lambda i: (i, 0))
    in_specs = [row(d), _const_spec(lw["gffn"].shape), _const_spec(lw["wup"].shape),
                _const_spec(lw["cw"].shape), _const_spec(lw["cb"].shape)]
    args = [x2d, lw["gffn"], lw["wup"], lw["cw"], lw["cb"]]
    if has_hist:
        assert tm % seg == 0 and seg % 8 == 0
        nseq = tm // seg
        in_specs.append(pl.BlockSpec((1, nseq, 2, d_ff), lambda i: (layer, i, 0, 0)))
        args.append(hist)
        conv_spec = pl.BlockSpec((nseq, 2, d_ff), lambda i: (i, 0, 0))
        scratch = [pltpu.VMEM((tm, d_ff), BF16)]
    else:
        assert seg % tm == 0
        tps = seg // tm
        conv_spec = pl.BlockSpec((1, 2, d_ff), lambda i: (i // tps, 0, 0))
        scratch = [pltpu.VMEM((8, d_ff), F32)]
    in_specs.append(_const_spec(lw["wdn"].shape))
    args.append(lw["wdn"])
    return pl.pallas_call(
        functools.partial(_ffn_kernel, seg=seg, d_ff=d_ff, has_hist=has_hist),
        grid=(t // tm,),
        in_specs=in_specs,
        out_specs=[row(d), conv_spec],
        out_shape=[jax.ShapeDtypeStruct((t, d), F32),
                   jax.ShapeDtypeStruct((t // seg, 2, d_ff), F32)],
        scratch_shapes=scratch,
        compiler_params=_cparams(1),
        name="ffn",
    )(*args)


def _group_matrix(groups, width):
    m = np.zeros((width, width), np.float32)
    for g in groups:
        for i in g:
            m[i, g] = 1.0 / len(g)
    return jnp.asarray(m, dtype=BF16)


def _pack_layer(w, l, dims):
    nope, rope, heads = dims["nope"], dims["rope"], dims["mla_heads"]
    q_rank, kv_rank = dims["q_rank"], dims["kv_rank"]
    sb_w, df_w = dims["sb_w"], dims["df_w"]
    half = rope // 2
    assert nope == HEAD and rope == 32 and heads % 2 == 0 and q_rank + kv_rank + rope <= 4 * LANES
    assert (q_rank + kv_rank) % LANES == 0 and sb_w % (2 * LANES) == 0 and df_w % (2 * LANES) == 0
    w_in = w["w_in"][l]
    o_kr = q_rank + kv_rank
    o_sb = o_kr + rope
    o_df = o_sb + 3 * sb_w
    o_g = o_df + 3 * df_w
    col = lambda a, b: w_in[:, a:b]
    wa = jnp.pad(col(0, o_sb), ((0, 0), (0, 4 * LANES - o_sb)))
    w1 = jnp.concatenate([wa, col(o_sb, o_g)], axis=1).astype(BF16)
    wn = jnp.concatenate([col(0, o_kr), col(o_sb, o_sb + sb_w), col(o_df, o_df + df_w),
                          col(o_df + 2 * df_w, o_g)], axis=1).astype(BF16)
    wt = jnp.concatenate([col(o_sb + sb_w, o_df), col(o_df + df_w, o_df + 2 * df_w),
                          col(o_kr, o_sb)], axis=1).T.astype(BF16)

    cols = np.full((heads // 2) * 2 * LANES, -1, np.int64)
    gain_src = np.full(cols.shape, -1, np.int64)
    for p in range(heads // 2):
        base = 2 * LANES * p
        for j, hd in enumerate((2 * p, 2 * p + 1)):
            hb = hd * (nope + rope)
            cols[base + nope * j:base + nope * (j + 1)] = hb + np.arange(nope)
            gain_src[base + nope * j:base + nope * (j + 1)] = np.arange(nope)
            x1 = base + 2 * nope + half * j
            x2 = base + 2 * nope + 2 * half + half * j
            cols[x1:x1 + half] = hb + nope + np.arange(half)
            cols[x2:x2 + half] = hb + nope + half + np.arange(half)
            gain_src[x1:x1 + half] = nope + np.arange(half)
            gain_src[x2:x2 + half] = nope + half + np.arange(half)
    valid = jnp.asarray(cols >= 0)
    wuq = jnp.where(valid[None, :], w["mla_w_uq"][l][:, np.maximum(cols, 0)], 0.0).astype(BF16)
    qg = jnp.concatenate([w["mla_qn_g"][l], w["mla_qr_g"][l]])
    gq = jnp.where(valid, qg[np.maximum(gain_src, 0)], 0.0) * ((nope + rope) ** -0.5 * LOG2E)

    groups = [list(range(0, nope)), list(range(nope, 2 * nope))]
    for j in range(2):
        groups.append(list(range(2 * nope + half * j, 2 * nope + half * (j + 1)))
                      + list(range(2 * nope + 2 * half + half * j, 2 * nope + 2 * half + half * (j + 1))))
    gm_mla = _group_matrix(groups, 2 * LANES)
    gm64 = _group_matrix([list(range(HEAD * j, HEAD * (j + 1))) for j in range(4)], 2 * LANES)

    rep = np.zeros((rope, LANES), np.float32)
    for j in range(half):
        rep[j, [j, half + j]] = 1.0
        rep[half + j, [2 * half + j, 3 * half + j]] = 1.0

    row = lambda v: v.reshape(1, -1).astype(F32)
    colv = lambda v: v.reshape(-1, 1).astype(F32)
    gdk = jnp.tile(w["diff_kn_g"][l], df_w // HEAD)
    gkn = jnp.tile(w["mla_kn_g"][l], heads)
    return dict(
        gmix=row(w["mix_norm_g"][l]), w1=w1, wn=wn, wt=wt, gcq=row(w["mla_q_norm_g"][l]), wuq=wuq,
        gm_mla=gm_mla, gq=row(gq), gckv=row(w["mla_kv_norm_g"][l]),
        gkr=row(jnp.pad(w["mla_kr_g"][l], (0, LANES - rope))), gkr_col=colv(w["mla_kr_g"][l]),
        gm64=gm64,
        gdq=row(jnp.tile(w["diff_qn_g"][l], df_w // HEAD) * (LOG2E / 8.0)),
        gdk=row(gdk), gdk_col=colv(gdk),
        wukv=jnp.concatenate([w["mla_w_uk"][l], w["mla_w_uv"][l]], axis=1).astype(BF16),
        wukt=w["mla_w_uk"][l].T.astype(BF16), wuvt=w["mla_w_uv"][l].T.astype(BF16),
        gkn=row(gkn), gkn_col=colv(gkn),
        rep=jnp.asarray(rep, dtype=BF16),
        lam=w["diff_lambda"][l].astype(F32), gsub=row(w["diff_subln_g"][l]),
        wg=w_in[:, o_g:].astype(BF16),
        wbr=jnp.stack([w["w_br_mla"][l], w["w_br_sb"][l], w["w_br_diff"][l]]).astype(BF16),
        wout=w["w_out"][l].astype(BF16),
        gffn=row(w["ffn_norm_g"][l]), wup=w["ffn_w_up"][l].astype(BF16),
        cw=w["ffn_conv_w"][l].astype(F32), cb=row(w["ffn_conv_b"][l]),
        wdn=w["ffn_w_down"][l].astype(BF16),
    )


def _rope_tables(pos, rope, n_rows):
    half = rope // 2
    inv = ROPE_BASE ** (-jnp.arange(half, dtype=F32) / half)
    ang = pos.astype(F32)[:, None] * inv[None, :]
    cos, sin = jnp.cos(ang), jnp.sin(ang)
    z = lambda n: jnp.zeros((pos.shape[0], n), F32)
    tab = jnp.concatenate([
        cos, cos, cos, cos, z(LANES - 4 * half),
        -sin, -sin, sin, sin, z(LANES - 4 * half),
        cos, cos, z(LANES - 2 * half),
        -sin, sin, z(LANES - 2 * half)], axis=1)
    return jnp.tile(tab, (n_rows // pos.shape[0], 1)), jnp.concatenate([cos.T, sin.T], axis=0)


def _tail(x2d, o_mla, o_sb, o_df, lw, *, seq, hist=None, layer=0):
    f2 = lambda a: a.reshape(x2d.shape[0], a.shape[-1])
    x1 = _merge(x2d, f2(o_mla), f2(o_sb), f2(o_df), lw)
    return _ffn(x1, lw, seg=seq, hist=hist, layer=layer)


def _layer_prompt(x2d, lw, dims, *, batch, seq, tabs, layer, depth, bufs, slopes):
    q, sq, dq, *bufs = _in_proj_t(x2d, lw, tabs[0], tabs[1], dims, batch=batch, seq=seq,
                                  layer=layer, depth=depth, prev=bufs)
    ckv, krt, skt, svt, dkt, dvi = bufs
    r3 = lambda a: a.reshape(batch, seq, a.shape[-1])
    kpt, mvt = _kv_up_t(ckv, krt, layer, lw, dims["rope"])
    src = lambda k, v, lyr, **kw: [dict(k=k, v=v, layer=lyr, kt=True, pos0=0, mode="self", **kw)]
    lam_init = 0.8 - 0.6 * math.exp(-0.3 * layer)
    o_mla = _attention("mla", r3(q), src(kpt[None], mvt[None], 0, vt=True), ns=2, q_pos0=0)
    o_sb = _attention("sb", r3(sq), src(skt, svt, layer, vt=True), ns=2, q_pos0=0)
    o_df = _attention("diff", r3(dq), src(dkt, dvi, layer, vt=False, vil=dims["df_w"] // LANES),
                      ns=2, q_pos0=0, extras=(slopes, lw["lam"], lw["gsub"]), lam_init=lam_init)
    x2, conv = _tail(x2d, o_mla, o_sb, o_df, lw, seq=seq)
    return x2, conv, tuple(bufs)


def _layer_sample(x2d, lw, dims, *, batch, seq, tab, layer, caches, conv_state, slopes):
    q, ckv, kr, sq, sk, sv, dq, dk, dv = _in_proj(x2d, lw, tab, dims)
    r3 = lambda a: a.reshape(batch, seq, a.shape[-1])
    r4 = lambda a: a.reshape(1, batch, seq, a.shape[-1])
    kpk, mv = _kv_up(ckv, kr, lw)
    c_ckv, c_krt, c_skt, c_svt, c_dkt, c_dv = caches
    plen = c_ckv.shape[2]

    def srcs(ck, cv, lyr, nk, nv, **kw):
        return [dict(k=ck, v=cv, layer=lyr, kt=True, pos0=0, mode="full", tk=plen, **kw),
                dict(k=r4(nk), v=r4(nv), layer=0, kt=False, vt=False, pos0=plen, mode="masked")]

    lam_init = 0.8 - 0.6 * math.exp(-0.3 * layer)
    latent = dict(ckv=c_ckv, krt=c_krt, wukt=lw["wukt"], wuvt=lw["wuvt"], gkn=lw["gkn_col"])
    o_mla = _attention("mla", r3(q), srcs(None, None, layer, kpk, mv, latent=latent), ns=4, q_pos0=plen)
    o_sb = _attention("sb", r3(sq), srcs(c_skt, c_svt, layer, sk, sv, vt=True), ns=4, q_pos0=plen)
    o_df = _attention("diff", r3(dq),
                      srcs(c_dkt, c_dv, layer, dk, dv, vt=False, vil=dims["df_w"] // LANES),
                      ns=4, q_pos0=plen, extras=(slopes, lw["lam"], lw["gsub"]), lam_init=lam_init)
    x2, conv = _tail(x2d, o_mla, o_sb, o_df, lw, seq=seq, hist=conv_state, layer=layer)
    return x2, (ckv, kr, sk, sv, dk, dv), conv


def kernel(x_prompt, x_sample, cache_mla_ckv, cache_mla_krope, cache_sb_k, cache_sb_v, cache_diff_k, cache_diff_v, state_ffn_conv, mix_norm_g, w_in, mla_q_norm_g, mla_w_uq, mla_kv_norm_g, mla_w_uk, mla_w_uv, mla_qn_g, mla_kn_g, mla_qr_g, mla_kr_g, diff_qn_g, diff_kn_g, diff_lambda, diff_subln_g, w_br_mla, w_br_sb, w_br_diff, w_out, ffn_norm_g, ffn_w_up, ffn_conv_w, ffn_conv_b, ffn_w_down):
    w = dict(mix_norm_g=mix_norm_g, w_in=w_in, mla_q_norm_g=mla_q_norm_g, mla_w_uq=mla_w_uq,
             mla_kv_norm_g=mla_kv_norm_g, mla_w_uk=mla_w_uk, mla_w_uv=mla_w_uv, mla_qn_g=mla_qn_g,
             mla_kn_g=mla_kn_g, mla_qr_g=mla_qr_g, mla_kr_g=mla_kr_g, diff_qn_g=diff_qn_g,
             diff_kn_g=diff_kn_g, diff_lambda=diff_lambda, diff_subln_g=diff_subln_g,
             w_br_mla=w_br_mla, w_br_sb=w_br_sb, w_br_diff=w_br_diff, w_out=w_out,
             ffn_norm_g=ffn_norm_g, ffn_w_up=ffn_w_up, ffn_conv_w=ffn_conv_w,
             ffn_conv_b=ffn_conv_b, ffn_w_down=ffn_w_down)
    depth = w_in.shape[0]
    bp, sp, d = x_prompt.shape
    bs, ss, _ = x_sample.shape
    past_len = cache_mla_ckv.shape[2]
    sb_heads, sb_dim = cache_sb_k.shape[3], cache_sb_k.shape[4]
    df_heads, df_dim = cache_diff_k.shape[3], cache_diff_k.shape[5]
    assert sb_dim == HEAD and df_dim == HEAD and ffn_conv_w.shape[1] == 3
    nope, rope = mla_qn_g.shape[1], mla_qr_g.shape[1]
    dims = dict(d_model=d, q_rank=mla_q_norm_g.shape[1], kv_rank=mla_kv_norm_g.shape[1],
                nope=nope, rope=rope, mla_heads=mla_w_uk.shape[2] // nope,
                sb_w=sb_heads * sb_dim, df_w=2 * df_heads * df_dim)
    sb_w, df_w = dims["sb_w"], dims["df_w"]

    tabs_p = _rope_tables(jnp.arange(sp, dtype=jnp.int32), rope, max(sp, ROW_TILE))
    tab_s, _ = _rope_tables(past_len + jnp.arange(ss, dtype=jnp.int32), rope, max(ss, ROW_TILE))
    slopes = 2.0 ** (-8.0 * jnp.arange(1, df_heads + 1, dtype=F32) / df_heads) * LOG2E

    feat = lambda c: jnp.moveaxis(c, 2, -1).reshape(depth, bs, -1, past_len)
    assert 2 * df_dim == LANES
    caches = (cache_mla_ckv, feat(cache_mla_krope), feat(cache_sb_k), feat(cache_sb_v),
              feat(cache_diff_k), cache_diff_v.reshape(depth, bs, past_len * df_heads, 2 * df_dim))

    xp = x_prompt.reshape(bp * sp, d)
    xs = x_sample.reshape(bs * ss, d)
    bufs, rows_s, conv_p, conv_s = None, [], [], []
    for l in range(depth):
        lw = _pack_layer(w, l, dims)
        xp, cp, bufs = _layer_prompt(xp, lw, dims, batch=bp, seq=sp, tabs=tabs_p, layer=l,
                                     depth=depth, bufs=bufs, slopes=slopes)
        xs, rs, cs = _layer_sample(xs, lw, dims, batch=bs, seq=ss, tab=tab_s, layer=l,
                                   caches=caches, conv_state=state_ffn_conv, slopes=slopes)
        rows_s.append(rs)
        conv_p.append(cp)
        conv_s.append(cs)

    p_ckv, p_krt, p_skt, p_svt, p_dkt, p_dvi = bufs
    tok = lambda t, tail: jnp.moveaxis(t.reshape((depth, bp) + tail + (sp,)), -1, 2)
    prompt_rows = (p_ckv, tok(p_krt, (rope,)), tok(p_skt, (sb_heads, sb_dim)),
                   tok(p_svt, (sb_heads, sb_dim)), tok(p_dkt, (df_heads, 2, df_dim)),
                   p_dvi.reshape(depth, bp, sp, df_heads, 2 * df_dim))

    def stack(i, tail):
        return jnp.stack([r[i] for r in rows_s]).reshape((depth, bs, ss) + tail)

    sample_rows = (stack(0, (dims["kv_rank"],)), stack(1, (rope,)), stack(2, (sb_heads, sb_dim)),
                   stack(3, (sb_heads, sb_dim)), stack(4, (df_heads, 2, df_dim)),
                   stack(5, (df_heads, 2 * df_dim)))
    return ((xp.reshape(bp, sp, d), xs.reshape(bs, ss, d)) + prompt_rows + (jnp.stack(conv_p),)
            + sample_rows + (jnp.stack(conv_s),))
```

```python
import functools
import math

import numpy as np
import jax
import jax.numpy as jnp
from jax import lax
from jax.experimental import pallas as pl
from jax.experimental.pallas import tpu as pltpu

F32 = jnp.float32
BF16 = jnp.bfloat16

EPS = 1e-6
CHUNK = 64
CHUNK_SHIFT = CHUNK.bit_length() - 1
assert 1 << CHUNK_SHIFT == CHUNK
ROPE_BASE = 10000.0
NEG_BIG = -1e30
HEAD = 64

LANES = 128
ROW_TILE = 512
KV_ROW_TILE = 1024
ATTN_TILE = 512
SB_BLOCK = 256
SB_CACHE_TILE = 1024
FF_CHUNK = 512
SB_SKIP_LOG2 = -160.0
LOG2E = math.log2(math.e)
VMEM_LIMIT = 52 * 1024 * 1024


def _cparams(n_axes):
    return pltpu.CompilerParams(dimension_semantics=("arbitrary",) * n_axes,
                                vmem_limit_bytes=VMEM_LIMIT)


def _const_spec(shape):
    nd = len(shape)
    return pl.BlockSpec(shape, lambda *_: (0,) * nd, pipeline_mode=pl.Buffered(1))


def _dot(a, b):
    return jnp.dot(a, b, preferred_element_type=F32)


def _dot_nt(a, b):
    return lax.dot_general(a, b, (((1,), (1,)), ((), ())), preferred_element_type=F32)


def _rmsnorm(x, g):
    return x * lax.rsqrt(jnp.mean(x * x, axis=-1, keepdims=True) + EPS) * g


def _group_mean_sq(x, gmat):
    sq = x * x
    hi = sq.astype(BF16)
    lo = (sq - hi.astype(F32)).astype(BF16)
    return _dot(hi, gmat) + _dot(lo, gmat)


def _row_group_rmsnorm(xt, gcol):
    r, n = xt.shape
    x3 = xt.reshape(r // HEAD, HEAD, n)
    y3 = x3 * lax.rsqrt(jnp.mean(x3 * x3, axis=1, keepdims=True) + EPS)
    return y3.reshape(r, n) * gcol


def _sigmoid(x):
    return 1.0 / (1.0 + jnp.exp(-x))


def _mla_query(cq, gcq_ref, wuq_ref, gm_mla_ref, gq_ref, cos_q, sin_q, q_ref, half):
    tm = cq.shape[0]
    cqn = _rmsnorm(cq, gcq_ref[...]).astype(BF16)
    q = _dot(cqn, wuq_ref[...])
    lane = lax.broadcasted_iota(jnp.int32, (tm, LANES), 1)
    for p in range(q.shape[1] // (2 * LANES)):
        lo = 2 * LANES * p
        blk = q[:, lo:lo + 2 * LANES]
        y = blk * lax.rsqrt(_group_mean_sq(blk, gm_mla_ref[...]) + EPS) * gq_ref[:, lo:lo + 2 * LANES]
        rp = y[:, LANES:]
        partner = jnp.where(lane < 2 * half, pltpu.roll(rp, LANES - 2 * half, 1),
                            pltpu.roll(rp, 2 * half, 1))
        q_ref[:, lo:lo + LANES] = y[:, :LANES].astype(BF16)
        q_ref[:, lo + LANES:lo + 2 * LANES] = (rp * cos_q + partner * sin_q).astype(BF16)


def _diff_query(zq, gm64_ref, gdq_ref, dq_ref):
    for s in range(zq.shape[1] // (2 * LANES)):
        lo = 2 * LANES * s
        blk = zq[:, lo:lo + 2 * LANES]
        r = lax.rsqrt(_group_mean_sq(blk, gm64_ref[...]) + EPS)
        dq_ref[:, lo:lo + 2 * LANES] = (blk * r * gdq_ref[:, lo:lo + 2 * LANES]).astype(BF16)


def _in_proj_kernel(x_ref, gmix_ref, w_ref, gcq_ref, wuq_ref, gm_mla_ref, gq_ref, gckv_ref,
                    gkr_ref, tab_ref, gm64_ref, gdq_ref, gdk_ref,
                    q_ref, ckv_ref, kr_ref, sq_ref, sk_ref, sv_ref, dq_ref, dk_ref, dv_ref,
                    *, q_rank, kv_rank, rope, sb_w, df_w):
    tm = x_ref.shape[0]
    half = rope // 2
    h = _rmsnorm(x_ref[...], gmix_ref[...]).astype(BF16)
    tab = tab_ref[...]
    cos_k, sin_k = tab[:, 2 * LANES:3 * LANES], tab[:, 3 * LANES:4 * LANES]
    lane = lax.broadcasted_iota(jnp.int32, (tm, LANES), 1)

    za = _dot(h, w_ref[:, 0:4 * LANES])
    _mla_query(za[:, 0:q_rank], gcq_ref, wuq_ref, gm_mla_ref, gq_ref, tab[:, 0:LANES],
               tab[:, LANES:2 * LANES], q_ref, half)
    ckv_ref[...] = _rmsnorm(za[:, q_rank:q_rank + kv_rank], gckv_ref[...])

    kr = za[:, q_rank + kv_rank:4 * LANES]
    ms = jnp.sum(kr * kr, axis=-1, keepdims=True) * (1.0 / rope)
    krn = kr * lax.rsqrt(ms + EPS) * gkr_ref[...]
    partner = jnp.where(lane < half, pltpu.roll(krn, LANES - half, 1), pltpu.roll(krn, half, 1))
    kr_ref[...] = (krn * cos_k + partner * sin_k)[:, :rope]

    o = 4 * LANES
    zs = _dot(h, w_ref[:, o:o + 3 * sb_w])
    sq_ref[...] = (zs[:, 0:sb_w] * (LOG2E / 8.0)).astype(BF16)
    sk_ref[...] = zs[:, sb_w:2 * sb_w]
    sv_ref[...] = zs[:, 2 * sb_w:3 * sb_w]

    o = o + 3 * sb_w
    zd = _dot(h, w_ref[:, o:o + 3 * df_w])
    _diff_query(zd[:, 0:df_w], gm64_ref, gdq_ref, dq_ref)
    for s in range(df_w // (2 * LANES)):
        lo = 2 * LANES * s
        blk = zd[:, df_w + lo:df_w + lo + 2 * LANES]
        r = lax.rsqrt(_group_mean_sq(blk, gm64_ref[...]) + EPS)
        dk_ref[:, lo:lo + 2 * LANES] = blk * r * gdk_ref[:, lo:lo + 2 * LANES]
    dv_ref[...] = zd[:, 2 * df_w:3 * df_w]


def _in_proj(x2d, lw, tab, dims):
    t, d = x2d.shape
    tm = ROW_TILE
    assert t % tm == 0 and tab.shape[0] % tm == 0
    n_tab = tab.shape[0] // tm
    row = lambda w: pl.BlockSpec((tm, w), lambda i: (i, 0))
    consts = [lw["gmix"], lw["w1"], lw["gcq"], lw["wuq"], lw["gm_mla"], lw["gq"], lw["gckv"], lw["gkr"]]
    consts2 = [lw["gm64"], lw["gdq"], lw["gdk"]]
    sb_w, df_w = dims["sb_w"], dims["df_w"]
    out_shape = [
        jax.ShapeDtypeStruct((t, lw["wuq"].shape[1]), BF16),
        jax.ShapeDtypeStruct((t, dims["kv_rank"]), F32),
        jax.ShapeDtypeStruct((t, dims["rope"]), F32),
        jax.ShapeDtypeStruct((t, sb_w), BF16),
        jax.ShapeDtypeStruct((t, sb_w), F32),
        jax.ShapeDtypeStruct((t, sb_w), F32),
        jax.ShapeDtypeStruct((t, df_w), BF16),
        jax.ShapeDtypeStruct((t, df_w), F32),
        jax.ShapeDtypeStruct((t, df_w), F32),
    ]
    kern = functools.partial(_in_proj_kernel, q_rank=dims["q_rank"], kv_rank=dims["kv_rank"],
                             rope=dims["rope"], sb_w=sb_w, df_w=df_w)
    return pl.pallas_call(
        kern,
        grid=(t // tm,),
        in_specs=([row(d)] + [_const_spec(c.shape) for c in consts]
                  + [pl.BlockSpec((tm, tab.shape[1]), lambda i: (i % n_tab, 0))]
                  + [_const_spec(c.shape) for c in consts2]),
        out_specs=[row(s.shape[1]) for s in out_shape],
        out_shape=out_shape,
        compiler_params=_cparams(1),
        name="in_proj",
    )(x2d, *consts, tab, *consts2)


def _in_proj_t_kernel(*refs, q_rank, kv_rank, rope, sb_w, df_w, n_alias):
    (x_ref, gmix_ref, wn_ref, wt_ref, gcq_ref, wuq_ref, gm_mla_ref, gq_ref, gckv_ref, gkr_ref,
     tab_ref, tabt_ref, gm64_ref, gdq_ref, gdk_ref) = refs[:15]
    (q_ref, sq_ref, dq_ref, ckv_ref, krt_ref, skt_ref, svt_ref, dkt_ref, dvi_ref) = refs[15 + n_alias:]
    half = rope // 2
    tm = x_ref.shape[0]
    h = _rmsnorm(x_ref[...], gmix_ref[...]).astype(BF16)
    tab = tab_ref[...]

    o = q_rank + kv_rank
    za = _dot(h, wn_ref[:, 0:o])
    _mla_query(za[:, 0:q_rank], gcq_ref, wuq_ref, gm_mla_ref, gq_ref, tab[:, 0:LANES],
               tab[:, LANES:2 * LANES], q_ref, half)
    ckv_ref[0, 0] = _rmsnorm(za[:, q_rank:o], gckv_ref[...])

    sq_ref[...] = (_dot(h, wn_ref[:, o:o + sb_w]) * (LOG2E / 8.0)).astype(BF16)
    o = o + sb_w
    zd = _dot(h, wn_ref[:, o:o + 2 * df_w])
    _diff_query(zd[:, 0:df_w], gm64_ref, gdq_ref, dq_ref)
    n_dv = df_w // LANES
    for hh in range(n_dv):
        dvi_ref[0, 0, pl.ds(hh, tm, stride=n_dv), :] = zd[:, df_w + LANES * hh:df_w + LANES * (hh + 1)]

    skt_ref[0, 0] = _dot_nt(wt_ref[0:sb_w, :], h)
    svt_ref[0, 0] = _dot_nt(wt_ref[sb_w:2 * sb_w, :], h)
    o = 2 * sb_w
    dkt_ref[0, 0] = _row_group_rmsnorm(_dot_nt(wt_ref[o:o + df_w, :], h), gdk_ref[...])
    o = o + df_w
    krt = _dot_nt(wt_ref[o:o + rope, :], h)
    krn = krt * lax.rsqrt(jnp.mean(krt * krt, axis=0, keepdims=True) + EPS) * gkr_ref[...]
    x1, x2 = krn[:half], krn[half:]
    cos_t, sin_t = tabt_ref[0:half, :], tabt_ref[half:rope, :]
    krt_ref[0, 0] = jnp.concatenate([x1 * cos_t - x2 * sin_t, x1 * sin_t + x2 * cos_t], axis=0)


def _in_proj_t(x2d, lw, tab, tabt, dims, *, batch, seq, layer, depth, prev):
    t, d = x2d.shape
    tm = ROW_TILE
    assert seq % tm == 0 and t == batch * seq
    nst = seq // tm
    row = lambda w: pl.BlockSpec((tm, w), lambda i: (i, 0))
    consts = [lw["gmix"], lw["wn"], lw["wt"], lw["gcq"], lw["wuq"], lw["gm_mla"], lw["gq"], lw["gckv"],
              lw["gkr_col"]]
    consts2 = [lw["gm64"], lw["gdq"], lw["gdk_col"]]
    sb_w, df_w, rope, kvr = dims["sb_w"], dims["df_w"], dims["rope"], dims["kv_rank"]
    n_alias = 0 if prev is None else len(prev)
    in_specs = ([row(d)] + [_const_spec(c.shape) for c in consts]
                + [pl.BlockSpec((tm, tab.shape[1]), lambda i: (i % nst, 0)),
                   pl.BlockSpec((rope, tm), lambda i: (0, i % nst))]
                + [_const_spec(c.shape) for c in consts2]
                + [pl.BlockSpec(memory_space=pl.ANY)] * n_alias)
    n_plain = len(in_specs) - n_alias
    tok = lambda w: pl.BlockSpec((1, 1, tm, w), lambda i: (layer, i // nst, i % nst, 0))
    feat = lambda r: pl.BlockSpec((1, 1, r, tm), lambda i: (layer, i // nst, 0, i % nst))
    n_dv = df_w // LANES
    out_shape = [
        jax.ShapeDtypeStruct((t, lw["wuq"].shape[1]), BF16),
        jax.ShapeDtypeStruct((t, sb_w), BF16),
        jax.ShapeDtypeStruct((t, df_w), BF16),
        jax.ShapeDtypeStruct((depth, batch, seq, kvr), F32),
        jax.ShapeDtypeStruct((depth, batch, rope, seq), F32),
        jax.ShapeDtypeStruct((depth, batch, sb_w, seq), F32),
        jax.ShapeDtypeStruct((depth, batch, sb_w, seq), F32),
        jax.ShapeDtypeStruct((depth, batch, df_w, seq), F32),
        jax.ShapeDtypeStruct((depth, batch, seq * n_dv, LANES), F32),
    ]
    out_specs = [row(out_shape[0].shape[1]), row(sb_w), row(df_w),
                 tok(kvr), feat(rope), feat(sb_w), feat(sb_w), feat(df_w),
                 pl.BlockSpec((1, 1, tm * n_dv, LANES), lambda i: (layer, i // nst, i % nst, 0))]
    kern = functools.partial(_in_proj_t_kernel, q_rank=dims["q_rank"], kv_rank=kvr, rope=rope,
                             sb_w=sb_w, df_w=df_w, n_alias=n_alias)
    return pl.pallas_call(
        kern,
        grid=(t // tm,),
        in_specs=in_specs,
        out_specs=out_specs,
        out_shape=out_shape,
        input_output_aliases={n_plain + j: 3 + j for j in range(n_alias)},
        compiler_params=_cparams(1),
        name="in_proj_t",
    )(x2d, *consts, tab, tabt, *consts2, *(prev or ()))


def _kv_up_kernel(ckv_ref, kr_ref, wukv_ref, gm64_ref, gkn_ref, rep_ref, k_ref, v_ref, *, kn_w):
    c = ckv_ref[...].astype(BF16)
    kv = _dot(c, wukv_ref[...])
    krrep = _dot(kr_ref[...].astype(BF16), rep_ref[...]).astype(BF16)
    for b in range(kn_w // (2 * LANES)):
        s = 2 * LANES * b
        blk = kv[:, s:s + 2 * LANES]
        r = lax.rsqrt(_group_mean_sq(blk, gm64_ref[...]) + EPS)
        kn = (blk * r * gkn_ref[:, s:s + 2 * LANES]).astype(BF16)
        for j in range(2):
            p = 2 * b + j
            k_ref[:, 2 * LANES * p:2 * LANES * p + LANES] = kn[:, LANES * j:LANES * (j + 1)]
            k_ref[:, 2 * LANES * p + LANES:2 * LANES * (p + 1)] = krrep
    v_ref[...] = kv[:, kn_w:].astype(BF16)


def _kv_up(ckv2d, kr2d, lw):
    r, kvr = ckv2d.shape
    tr = min(KV_ROW_TILE, r)
    assert r % tr == 0
    kn_w = lw["gkn"].shape[1]
    consts = [lw["wukv"], lw["gm64"], lw["gkn"], lw["rep"]]
    out_shape = [jax.ShapeDtypeStruct((r, 2 * kn_w), BF16),
                 jax.ShapeDtypeStruct((r, lw["wukv"].shape[1] - kn_w), BF16)]
    return pl.pallas_call(
        functools.partial(_kv_up_kernel, kn_w=kn_w),
        grid=(r // tr,),
        in_specs=[pl.BlockSpec((tr, kvr), lambda i: (i, 0)),
                  pl.BlockSpec((tr, kr2d.shape[1]), lambda i: (i, 0))]
                 + [_const_spec(c.shape) for c in consts],
        out_specs=[pl.BlockSpec((tr, s.shape[1]), lambda i: (i, 0)) for s in out_shape],
        out_shape=out_shape,
        compiler_params=_cparams(1),
        name="mla_kv_up",
    )(ckv2d, kr2d, *consts)


def _mla_kv_feature_major(c, kr, wukt, wuvt, gkn, kt_ref, vt_ref):
    half = kr.shape[0] // 2
    n = c.shape[0]
    knt = _row_group_rmsnorm(_dot_nt(wukt, c), gkn).astype(BF16)
    x1, x2 = kr[:half], kr[half:]
    krrep = jnp.concatenate([x1, x1, x2, x2, jnp.zeros((LANES - 4 * half, n), BF16)], axis=0)
    for p in range(knt.shape[0] // LANES):
        kt_ref[2 * LANES * p:2 * LANES * p + LANES, :] = knt[LANES * p:LANES * (p + 1)]
        kt_ref[2 * LANES * p + LANES:2 * LANES * (p + 1), :] = krrep
    vt_ref[...] = _dot_nt(wuvt, c).astype(BF16)


def _stream_lane_masks(kind, ns, width):
    lane = lax.broadcasted_iota(jnp.int32, (1, width), 1)
    masks = []
    for i in range(ns):
        if kind == "mla":
            r = lane - 2 * LANES * (i // 2)
            j = i % 2
            masks.append(((r >= HEAD * j) & (r < HEAD * (j + 1)))
                         | ((r >= 128 + 16 * j) & (r < 128 + 16 * (j + 1)))
                         | ((r >= 160 + 16 * j) & (r < 160 + 16 * (j + 1))))
        else:
            masks.append((lane >= HEAD * i) & (lane < HEAD * (i + 1)))
    return masks


def _attn_kernel(*refs, kind, srcs, ns, tq, q_pos0, lam_init):
    it = iter(refs)
    slopes_ref = next(it) if kind == "diff" else None
    q_ref = next(it)
    kv_refs = [tuple(next(it) for _ in range(5 if s["latent"] else 2)) for s in srcs]
    if kind == "diff":
        lam_ref, gsub_ref = next(it), next(it)
    o_ref = next(it)
    scr = [(next(it), next(it)) if s["cast"] else None for s in srcs]

    grp = pl.program_id(1)
    qi = pl.program_id(2)
    rows = ns * tq
    vw = HEAD * ns

    for s, sc, in_refs in zip(srcs, scr, kv_refs):
        if s["latent"]:
            @pl.when(qi == 0)
            def _(s=s, sc=sc, in_refs=in_refs):
                ckv_ref, krt_ref, wukt_ref, wuvt_ref, gkn_ref = in_refs
                _mla_kv_feature_major(ckv_ref[0, 0].astype(BF16), krt_ref[0, 0].astype(BF16), wukt_ref[...],
                                      wuvt_ref[...], gkn_ref[...], sc[0], sc[1])
        elif sc is not None:
            k_ref, v_ref = in_refs

            @pl.when(qi == 0)
            def _(s=s, sc=sc, k_ref=k_ref, v_ref=v_ref):
                sc[0][...] = k_ref[0, 0].astype(BF16)
                if s["vil"]:
                    for j in range(vw // LANES):
                        rows_j = pl.ds(grp * (vw // LANES) + j, s["sk"], stride=s["vil"])
                        sc[1][:, LANES * j:LANES * (j + 1)] = v_ref[0, 0, rows_j, :].astype(BF16)
                else:
                    sc[1][...] = v_ref[0, 0].astype(BF16)

    q = q_ref[0]
    zero = jnp.zeros_like(q)
    qs = jnp.concatenate([jnp.where(m, q, zero) for m in _stream_lane_masks(kind, ns, q.shape[1])],
                         axis=0)
    q_start = q_pos0 + qi * tq
    qpos = jnp.concatenate([q_start + lax.broadcasted_iota(jnp.int32, (tq, 1), 0)] * ns, axis=0)
    if kind == "diff":
        slope = jnp.concatenate(
            [jnp.full((tq, 1), slopes_ref[grp * (ns // 2) + i // 2], F32) for i in range(ns)], axis=0)

    if kind == "sb":
        init = (jnp.zeros((rows, 1), F32), jnp.zeros((rows, vw), F32))
    else:
        init = (jnp.full((rows, 1), NEG_BIG, F32), jnp.zeros((rows, 1), F32),
                jnp.zeros((rows, vw), F32))

    def tile_step(st, k_t, v_t, k_start, src, masked, upper):
        tk = src["tk"]
        kpos = k_start + lax.broadcasted_iota(jnp.int32, (1, tk), 1)
        s = _dot(qs, k_t) if src["kt"] else _dot_nt(qs, k_t)
        pv = (lambda p: _dot_nt(p, v_t)) if src["vt"] else (lambda p: _dot(p, v_t))
        if masked:
            valid = (kpos < qpos) if kind == "sb" else ((kpos >> CHUNK_SHIFT) <= (qpos >> CHUNK_SHIFT))
        if kind == "sb":
            c, acc = st
            lp = jnp.log(1.0 + jnp.exp2(-jnp.abs(s))) * LOG2E
            ls = jnp.minimum(s, 0.0) - lp
            l1 = ls - s
            if masked:
                l1 = jnp.where(valid, l1, 0.0)
            hi = l1.astype(BF16)
            lo = (l1 - hi.astype(F32)).astype(BF16)
            cb = upper.shape[0]
            nb = tk // cb
            if nb > 1:
                hi = jnp.concatenate([hi[:, cb * j:cb * (j + 1)] for j in range(nb)], axis=0)
                lo = jnp.concatenate([lo[:, cb * j:cb * (j + 1)] for j in range(nb)], axis=0)
            suffix = _dot(hi, upper) + _dot(lo, upper)
            parts = [None] * nb
            for j in reversed(range(nb)):
                parts[j] = jnp.exp2(ls[:, cb * j:cb * (j + 1)] + suffix[rows * j:rows * (j + 1)] + c)
                c = c + jnp.sum(l1[:, cb * j:cb * (j + 1)], axis=-1, keepdims=True)
            a = parts[0] if nb == 1 else jnp.concatenate(parts, axis=1)
            if masked:
                a = jnp.where(valid, a, 0.0)
            return (c, acc + pv(a.astype(BF16)))
        m, l, acc = st
        if kind == "diff":
            dist = qpos - kpos
            s = s - slope * (jnp.abs(dist) if masked else dist).astype(F32)
        if masked:
            s = jnp.where(valid, s, -jnp.inf)
        m_new = jnp.maximum(m, jnp.max(s, axis=-1, keepdims=True))
        alpha = jnp.exp2(m - m_new)
        p = jnp.exp2(s - m_new)
        l = alpha * l + jnp.sum(p, axis=-1, keepdims=True)
        return (m_new, l, alpha * acc + pv(p.astype(BF16)))

    def upper_tri(tk):
        if kind != "sb":
            return None
        cb = min(tk, SB_BLOCK)
        r = lax.broadcasted_iota(jnp.int32, (cb, cb), 0)
        c = lax.broadcasted_iota(jnp.int32, (cb, cb), 1)
        return jnp.where(r > c, 1.0, 0.0).astype(BF16)

    st = init
    for s, sc, in_refs in reversed(list(zip(srcs, scr, kv_refs))):
        k_ref, v_ref = (None, None) if s["latent"] else in_refs
        tk = s["tk"]
        upper = upper_tri(tk)

        def load(k_off, src, _sc=sc, _k=k_ref, _v=v_ref):
            def one(ref4, ref2, transposed):
                win = pl.ds(k_off, src["tk"])
                if ref2 is not None:
                    return ref2[:, win] if transposed else ref2[win, :]
                return ref4[0, 0, :, win] if transposed else ref4[0, 0, win, :]
            return (one(_k, _sc and _sc[0], src["kt"]), one(_v, _sc and _sc[1], src["vt"]))

        if s["mode"] == "self":
            k_t, v_t = load(pl.multiple_of(qi * tk, tk), s)
            st = tile_step(st, k_t, v_t, s["pos0"] + qi * tk, s, True, upper)
            if kind == "sb" and tk > SB_BLOCK:
                s = dict(s, tk=SB_BLOCK)
            n_full = qi * (tk // s["tk"])
        elif s["mode"] == "masked":
            k_t, v_t = load(0, s)
            st = tile_step(st, k_t, v_t, s["pos0"], s, True, upper)
            continue
        else:
            n_full = s["sk"] // tk

        def body(i, st_, _s=s, _load=load, _n=n_full, _upper=upper):
            kj = _n - 1 - i
            k_t, v_t = _load(pl.multiple_of(kj * _s["tk"], _s["tk"]), _s)
            return tile_step(st_, k_t, v_t, _s["pos0"] + kj * _s["tk"], _s, False, _upper)

        if isinstance(n_full, int) and n_full <= 2:
            for i in range(n_full):
                st = body(i, st)
        elif kind == "sb":
            def live(st_):
                return (jnp.max(st_[0]) > SB_SKIP_LOG2).astype(jnp.int32)

            def wbody(carry, _body=body):
                i, _, st_ = carry
                st_ = _body(i, st_)
                return i + 1, live(st_), st_

            _, _, st = lax.while_loop(lambda carry, _n=n_full: (carry[0] < _n) & (carry[1] > 0),
                                      wbody, (jnp.int32(0), live(st), st))
        else:
            st = lax.fori_loop(0, n_full, body, st)

    acc = st[-1]
    if kind == "diff":
        l = st[1]
        lv = lam_ref[...]
        lam = (jnp.exp(jnp.sum(lv[0:1] * lv[1:2], axis=-1, keepdims=True))
               - jnp.exp(jnp.sum(lv[2:3] * lv[3:4], axis=-1, keepdims=True)) + lam_init)
        for hh in range(ns // 2):
            lanes = slice(2 * HEAD * hh, 2 * HEAD * (hh + 1))
            r0, r1 = slice(tq * 2 * hh, tq * (2 * hh + 1)), slice(tq * (2 * hh + 1), tq * (2 * hh + 2))
            o = acc[r0, lanes] / l[r0] - lam * (acc[r1, lanes] / l[r1])
            o_ref[0, :, lanes] = (_rmsnorm(o, gsub_ref[...]) * (1.0 - lam_init)).astype(BF16)
    else:
        lane = lax.broadcasted_iota(jnp.int32, (tq, vw), 1)
        o = jnp.zeros((tq, vw), F32)
        for i in range(ns):
            blk = acc[tq * i:tq * (i + 1)]
            if kind == "mla":
                blk = blk / st[1][tq * i:tq * (i + 1)]
            o = jnp.where((lane >= HEAD * i) & (lane < HEAD * (i + 1)), blk, o)
        o_ref[0] = o.astype(BF16)


def _attention(kind, q3, sources, *, ns, q_pos0, extras=(), lam_init=0.0):
    b, sq, qtot = q3.shape
    w = (2 * HEAD if kind == "mla" else HEAD) * ns
    vw = HEAD * ns
    n_groups = qtot // w
    tq = min(ATTN_TILE, sq)
    assert sq % tq == 0 and qtot % w == 0
    srcs, in_specs, args, scratch = [], [], [], []
    if kind == "diff":
        in_specs.append(pl.BlockSpec(memory_space=pltpu.SMEM))
        args.append(extras[0])
    in_specs.append(pl.BlockSpec((1, tq, w), lambda bi, g, qi: (bi, qi, g)))
    args.append(q3)

    def kv_spec(arr, width, transposed, layer):
        if transposed:
            assert arr.shape[2] == n_groups * width
            return pl.BlockSpec((1, 1, width, arr.shape[3]), lambda bi, g, qi: (layer, bi, g, 0))
        assert arr.shape[3] == n_groups * width
        return pl.BlockSpec((1, 1, arr.shape[2], width), lambda bi, g, qi: (layer, bi, 0, g))

    for s in sources:
        lat = s.get("latent")
        if lat is not None:
            assert kind == "mla" and s["mode"] in ("full", "self")
            ckv, krt, layer = lat["ckv"], lat["krt"], s["layer"]
            sk = ckv.shape[2]
            tk = tq if s["mode"] == "self" else s["tk"]
            assert sk % tk == 0
            srcs.append(dict(mode=s["mode"], pos0=s["pos0"], sk=sk, tk=tk, cast=True, kt=True, vt=True,
                             vil=0, latent=True))
            in_specs += [pl.BlockSpec((1, 1, sk, ckv.shape[3]), lambda bi, g, qi, _l=layer: (_l, bi, 0, 0)),
                         pl.BlockSpec((1, 1, krt.shape[2], sk), lambda bi, g, qi, _l=layer: (_l, bi, 0, 0)),
                         pl.BlockSpec((w // 2, ckv.shape[3]), lambda bi, g, qi: (g, 0)),
                         pl.BlockSpec((vw, ckv.shape[3]), lambda bi, g, qi: (g, 0)),
                         pl.BlockSpec((w // 2, 1), lambda bi, g, qi: (g, 0))]
            args += [ckv, krt, lat["wukt"], lat["wuvt"], lat["gkn"]]
            scratch += [pltpu.VMEM((w, sk), BF16), pltpu.VMEM((vw, sk), BF16)]
            continue
        k, v, kt, vt = s["k"], s["v"], s["kt"], s["vt"]
        sk = k.shape[3] if kt else k.shape[2]
        tk = tq if s["mode"] == "self" else (sk if s["mode"] == "masked" else s.get("tk", min(ATTN_TILE, sk)))
        assert sk % tk == 0 and (tk <= SB_BLOCK or tk % SB_BLOCK == 0)
        cast = k.dtype != BF16
        assert cast == (v.dtype != BF16)
        vil = s.get("vil", 0)
        vlayer = s.get("vlayer", s["layer"])
        srcs.append(dict(mode=s["mode"], pos0=s["pos0"], sk=sk, tk=tk, cast=cast, kt=kt, vt=vt, vil=vil,
                         latent=False))
        if vil:
            assert cast and not vt and v.shape[2:] == (sk * vil, LANES) and vw % LANES == 0
            v_spec = pl.BlockSpec((1, 1, sk * vil, LANES), lambda bi, g, qi, _l=vlayer: (_l, bi, 0, 0))
        else:
            v_spec = kv_spec(v, vw, vt, vlayer)
        in_specs += [kv_spec(k, w, kt, s["layer"]), v_spec]
        args += [k, v]
        if cast:
            scratch += [pltpu.VMEM((w, sk) if kt else (sk, w), BF16),
                        pltpu.VMEM((vw, sk) if vt else (sk, vw), BF16)]
    if kind == "diff":
        in_specs += [_const_spec(extras[1].shape), _const_spec(extras[2].shape)]
        args += [extras[1], extras[2]]
    kern = functools.partial(_attn_kernel, kind=kind, srcs=srcs, ns=ns, tq=tq, q_pos0=q_pos0,
                             lam_init=lam_init)
    return pl.pallas_call(
        kern,
        grid=(b, n_groups, sq // tq),
        in_specs=in_specs,
        out_specs=pl.BlockSpec((1, tq, vw), lambda bi, g, qi: (bi, qi, g)),
        out_shape=jax.ShapeDtypeStruct((b, sq, n_groups * vw), BF16),
        scratch_shapes=scratch,
        compiler_params=_cparams(3),
        name="attn_" + kind,
    )(*args)


def _merge_kernel(x_ref, gmix_ref, wg_ref, om_ref, os_ref, od_ref, wbr_ref, wout_ref, o_ref):
    x = x_ref[...]
    d = x.shape[1]
    h = _rmsnorm(x, gmix_ref[...]).astype(BF16)
    merged = None
    for i, br_ref in enumerate((om_ref, os_ref, od_ref)):
        gate = _sigmoid(_dot(h, wg_ref[:, i * d:(i + 1) * d]))
        term = gate * _dot(br_ref[...], wbr_ref[i])
        merged = term if merged is None else merged + term
    o_ref[...] = x + _dot(merged.astype(BF16), wout_ref[...])


def _merge(x2d, o_mla, o_sb, o_diff, lw):
    t, d = x2d.shape
    tm = ROW_TILE
    row = lambda w: pl.BlockSpec((tm, w), lambda i: (i, 0))
    return pl.pallas_call(
        _merge_kernel,
        grid=(t // tm,),
        in_specs=[row(d), _const_spec(lw["gmix"].shape), _const_spec(lw["wg"].shape),
                  row(o_mla.shape[1]), row(o_sb.shape[1]), row(o_diff.shape[1]),
                  _const_spec(lw["wbr"].shape), _const_spec(lw["wout"].shape)],
        out_specs=row(d),
        out_shape=jax.ShapeDtypeStruct((t, d), F32),
        compiler_params=_cparams(1),
        name="merge",
    )(x2d, lw["gmix"], lw["wg"], o_mla, o_sb, o_diff, lw["wbr"], lw["wout"])


def _conv_rows(a, h0, h1, cw, cb):
    n = a.shape[0]
    row = lax.broadcasted_iota(jnp.int32, (n, 1), 0)
    p1 = jnp.where(row == 0, h1, pltpu.roll(a, 1, 0))
    p2 = jnp.where(row == 0, h0, jnp.where(row == 1, h1, pltpu.roll(a, 2, 0)))
    return p2 * cw[0:1] + p1 * cw[1:2] + a * cw[2:3] + cb


def _ffn_kernel(*refs, seg, d_ff, has_hist):
    it = iter(refs)
    x_ref, g_ref, wup_ref, cw_ref, cb_ref = next(it), next(it), next(it), next(it), next(it)
    hist_ref = next(it) if has_hist else None
    wdn_ref, o_ref, conv_ref = next(it), next(it), next(it)
    carry_ref = next(it) if not has_hist else None
    act_ref = next(it) if has_hist else None

    x = x_ref[...]
    tm = x.shape[0]
    h = _rmsnorm(x, g_ref[...]).astype(BF16)
    cw, cb = cw_ref[...], cb_ref[...]

    if not has_hist:
        i = pl.program_id(0)
        tiles_per_seq = seg // tm

        @pl.when((i % tiles_per_seq) == 0)
        def _():
            carry_ref[...] = jnp.zeros_like(carry_ref)

        tail = carry_ref[...]
        y, tails = x, []
        chunks = [(c0, min(FF_CHUNK, d_ff - c0)) for c0 in range(0, d_ff, FF_CHUNK)]
        up = lambda c0, cs: (_dot(h, wup_ref[:, c0:c0 + cs]), _dot(h, wup_ref[:, d_ff + c0:d_ff + c0 + cs]))
        nxt = up(*chunks[0])
        for idx, (c0, cs) in enumerate(chunks):
            a, u = nxt
            if idx + 1 < len(chunks):
                nxt = up(*chunks[idx + 1])
            c = _conv_rows(a, tail[6:7, c0:c0 + cs], tail[7:8, c0:c0 + cs], cw[:, c0:c0 + cs],
                           cb[:, c0:c0 + cs])
            tails.append(a[tm - 8:tm])
            y = y + _dot((c * _sigmoid(c) * u).astype(BF16), wdn_ref[c0:c0 + cs, :])
        new_tail = jnp.concatenate(tails, axis=1)
        carry_ref[...] = new_tail

        @pl.when((i % tiles_per_seq) == tiles_per_seq - 1)
        def _():
            conv_ref[0] = new_tail[6:8]

        o_ref[...] = y
    else:
        a = _dot(h, wup_ref[:, 0:d_ff])
        u = _dot(h, wup_ref[:, d_ff:2 * d_ff])
        for s in range(tm // seg):
            a_s = a[s * seg:(s + 1) * seg]
            hs = hist_ref[0, s]
            c = _conv_rows(a_s, hs[0:1], hs[1:2], cw, cb)
            act_ref[s * seg:(s + 1) * seg, :] = (c * _sigmoid(c) * u[s * seg:(s + 1) * seg]).astype(BF16)
            conv_ref[s] = a_s[seg - 2:seg]
        o_ref[...] = x + _dot(act_ref[...], wdn_ref[...])


def _ffn(x2d, lw, *, seg, hist=None, layer=0):
    t, d = x2d.shape
    tm = ROW_TILE
    d_ff = lw["wdn"].shape[0]
    has_hist = hist is not None
    row = lambda w: pl.BlockSpec((tm, w), lambda i: (i, 0))
    in_specs = [row(d), _const_spec(lw["gffn"].shape), _const_spec(lw["wup"].shape),
                _const_spec(lw["cw"].shape), _const_spec(lw["cb"].shape)]
    args = [x2d, lw["gffn"], lw["wup"], lw["cw"], lw["cb"]]
    if has_hist:
        assert tm % seg == 0 and seg % 8 == 0
        nseq = tm // seg
        in_specs.append(pl.BlockSpec((1, nseq, 2, d_ff), lambda i: (layer, i, 0, 0)))
        args.append(hist)
        conv_spec = pl.BlockSpec((nseq, 2, d_ff), lambda i: (i, 0, 0))
        scratch = [pltpu.VMEM((tm, d_ff), BF16)]
    else:
        assert seg % tm == 0
        tps = seg // tm
        conv_spec = pl.BlockSpec((1, 2, d_ff), lambda i: (i // tps, 0, 0))
        scratch = [pltpu.VMEM((8, d_ff), F32)]
    in_specs.append(_const_spec(lw["wdn"].shape))
    args.append(lw["wdn"])
    return pl.pallas_call(
        functools.partial(_ffn_kernel, seg=seg, d_ff=d_ff, has_hist=has_hist),
        grid=(t // tm,),
        in_specs=in_specs,
        out_specs=[row(d), conv_spec],
        out_shape=[jax.ShapeDtypeStruct((t, d), F32),
                   jax.ShapeDtypeStruct((t // seg, 2, d_ff), F32)],
        scratch_shapes=scratch,
        compiler_params=_cparams(1),
        name="ffn",
    )(*args)


def _group_matrix(groups, width):
    m = np.zeros((width, width), np.float32)
    for g in groups:
        for i in g:
            m[i, g] = 1.0 / len(g)
    return jnp.asarray(m, dtype=BF16)


def _pack_layer(w, l, dims):
    nope, rope, heads = dims["nope"], dims["rope"], dims["mla_heads"]
    q_rank, kv_rank = dims["q_rank"], dims["kv_rank"]
    sb_w, df_w = dims["sb_w"], dims["df_w"]
    half = rope // 2
    assert nope == HEAD and rope == 32 and heads % 2 == 0 and q_rank + kv_rank + rope <= 4 * LANES
    assert (q_rank + kv_rank) % LANES == 0 and sb_w % (2 * LANES) == 0 and df_w % (2 * LANES) == 0
    w_in = w["w_in"][l]
    o_kr = q_rank + kv_rank
    o_sb = o_kr + rope
    o_df = o_sb + 3 * sb_w
    o_g = o_df + 3 * df_w
    col = lambda a, b: w_in[:, a:b]
    wa = jnp.pad(col(0, o_sb), ((0, 0), (0, 4 * LANES - o_sb)))
    w1 = jnp.concatenate([wa, col(o_sb, o_g)], axis=1).astype(BF16)
    wn = jnp.concatenate([col(0, o_kr), col(o_sb, o_sb + sb_w), col(o_df, o_df + df_w),
                          col(o_df + 2 * df_w, o_g)], axis=1).astype(BF16)
    wt = jnp.concatenate([col(o_sb + sb_w, o_df), col(o_df + df_w, o_df + 2 * df_w),
                          col(o_kr, o_sb)], axis=1).T.astype(BF16)

    cols = np.full((heads // 2) * 2 * LANES, -1, np.int64)
    gain_src = np.full(cols.shape, -1, np.int64)
    for p in range(heads // 2):
        base = 2 * LANES * p
        for j, hd in enumerate((2 * p, 2 * p + 1)):
            hb = hd * (nope + rope)
            cols[base + nope * j:base + nope * (j + 1)] = hb + np.arange(nope)
            gain_src[base + nope * j:base + nope * (j + 1)] = np.arange(nope)
            x1 = base + 2 * nope + half * j
            x2 = base + 2 * nope + 2 * half + half * j
            cols[x1:x1 + half] = hb + nope + np.arange(half)
            cols[x2:x2 + half] = hb + nope + half + np.arange(half)
            gain_src[x1:x1 + half] = nope + np.arange(half)
            gain_src[x2:x2 + half] = nope + half + np.arange(half)
    valid = jnp.asarray(cols >= 0)
    wuq = jnp.where(valid[None, :], w["mla_w_uq"][l][:, np.maximum(cols, 0)], 0.0).astype(BF16)
    qg = jnp.concatenate([w["mla_qn_g"][l], w["mla_qr_g"][l]])
    gq = jnp.where(valid, qg[np.maximum(gain_src, 0)], 0.0) * ((nope + rope) ** -0.5 * LOG2E)

    groups = [list(range(0, nope)), list(range(nope, 2 * nope))]
    for j in range(2):
        groups.append(list(range(2 * nope + half * j, 2 * nope + half * (j + 1)))
                      + list(range(2 * nope + 2 * half + half * j, 2 * nope + 2 * half + half * (j + 1))))
    gm_mla = _group_matrix(groups, 2 * LANES)
    gm64 = _group_matrix([list(range(HEAD * j, HEAD * (j + 1))) for j in range(4)], 2 * LANES)

    rep = np.zeros((rope, LANES), np.float32)
    for j in range(half):
        rep[j, [j, half + j]] = 1.0
        rep[half + j, [2 * half + j, 3 * half + j]] = 1.0

    row = lambda v: v.reshape(1, -1).astype(F32)
    colv = lambda v: v.reshape(-1, 1).astype(F32)
    gdk = jnp.tile(w["diff_kn_g"][l], df_w // HEAD)
    gkn = jnp.tile(w["mla_kn_g"][l], heads)
    return dict(
        gmix=row(w["mix_norm_g"][l]), w1=w1, wn=wn, wt=wt, gcq=row(w["mla_q_norm_g"][l]), wuq=wuq,
        gm_mla=gm_mla, gq=row(gq), gckv=row(w["mla_kv_norm_g"][l]),
        gkr=row(jnp.pad(w["mla_kr_g"][l], (0, LANES - rope))), gkr_col=colv(w["mla_kr_g"][l]),
        gm64=gm64,
        gdq=row(jnp.tile(w["diff_qn_g"][l], df_w // HEAD) * (LOG2E / 8.0)),
        gdk=row(gdk), gdk_col=colv(gdk),
        wukv=jnp.concatenate([w["mla_w_uk"][l], w["mla_w_uv"][l]], axis=1).astype(BF16),
        wukt=w["mla_w_uk"][l].T.astype(BF16), wuvt=w["mla_w_uv"][l].T.astype(BF16),
        gkn=row(gkn), gkn_col=colv(gkn),
        rep=jnp.asarray(rep, dtype=BF16),
        lam=w["diff_lambda"][l].astype(F32), gsub=row(w["diff_subln_g"][l]),
        wg=w_in[:, o_g:].astype(BF16),
        wbr=jnp.stack([w["w_br_mla"][l], w["w_br_sb"][l], w["w_br_diff"][l]]).astype(BF16),
        wout=w["w_out"][l].astype(BF16),
        gffn=row(w["ffn_norm_g"][l]), wup=w["ffn_w_up"][l].astype(BF16),
        cw=w["ffn_conv_w"][l].astype(F32), cb=row(w["ffn_conv_b"][l]),
        wdn=w["ffn_w_down"][l].astype(BF16),
    )


def _rope_tables(pos, rope, n_rows):
    half = rope // 2
    inv = ROPE_BASE ** (-jnp.arange(half, dtype=F32) / half)
    ang = pos.astype(F32)[:, None] * inv[None, :]
    cos, sin = jnp.cos(ang), jnp.sin(ang)
    z = lambda n: jnp.zeros((pos.shape[0], n), F32)
    tab = jnp.concatenate([
        cos, cos, cos, cos, z(LANES - 4 * half),
        -sin, -sin, sin, sin, z(LANES - 4 * half),
        cos, cos, z(LANES - 2 * half),
        -sin, sin, z(LANES - 2 * half)], axis=1)
    return jnp.tile(tab, (n_rows // pos.shape[0], 1)), jnp.concatenate([cos.T, sin.T], axis=0)


def _tail(x2d, o_mla, o_sb, o_df, lw, *, seq, hist=None, layer=0):
    f2 = lambda a: a.reshape(x2d.shape[0], a.shape[-1])
    x1 = _merge(x2d, f2(o_mla), f2(o_sb), f2(o_df), lw)
    return _ffn(x1, lw, seg=seq, hist=hist, layer=layer)


def _layer_prompt(x2d, lw, dims, *, batch, seq, tabs, layer, depth, bufs, slopes):
    q, sq, dq, *bufs = _in_proj_t(x2d, lw, tabs[0], tabs[1], dims, batch=batch, seq=seq,
                                  layer=layer, depth=depth, prev=bufs)
    ckv, krt, skt, svt, dkt, dvi = bufs
    r3 = lambda a: a.reshape(batch, seq, a.shape[-1])
    src = lambda k, v, lyr, **kw: [dict(k=k, v=v, layer=lyr, kt=True, pos0=0, mode="self", **kw)]
    lam_init = 0.8 - 0.6 * math.exp(-0.3 * layer)
    latent = dict(ckv=ckv, krt=krt, wukt=lw["wukt"], wuvt=lw["wuvt"], gkn=lw["gkn_col"])
    o_mla = _attention("mla", r3(q), src(None, None, layer, latent=latent), ns=2, q_pos0=0)
    o_sb = _attention("sb", r3(sq), src(skt, svt, layer, vt=True), ns=2, q_pos0=0)
    o_df = _attention("diff", r3(dq), src(dkt, dvi, layer, vt=False, vil=dims["df_w"] // LANES),
                      ns=2, q_pos0=0, extras=(slopes, lw["lam"], lw["gsub"]), lam_init=lam_init)
    x2, conv = _tail(x2d, o_mla, o_sb, o_df, lw, seq=seq)
    return x2, conv, tuple(bufs)


def _layer_sample(x2d, lw, dims, *, batch, seq, tab, layer, caches, conv_state, slopes):
    q, ckv, kr, sq, sk, sv, dq, dk, dv = _in_proj(x2d, lw, tab, dims)
    r3 = lambda a: a.reshape(batch, seq, a.shape[-1])
    r4 = lambda a: a.reshape(1, batch, seq, a.shape[-1])
    kpk, mv = _kv_up(ckv, kr, lw)
    c_ckv, c_krt, c_skt, c_svt, c_dkt, c_dv = caches
    plen = c_ckv.shape[2]

    def srcs(ck, cv, lyr, nk, nv, tk=plen, **kw):
        return [dict(k=ck, v=cv, layer=lyr, kt=True, pos0=0, mode="full", tk=tk, **kw),
                dict(k=r4(nk), v=r4(nv), layer=0, kt=False, vt=False, pos0=plen, mode="masked")]

    lam_init = 0.8 - 0.6 * math.exp(-0.3 * layer)
    latent = dict(ckv=c_ckv, krt=c_krt, wukt=lw["wukt"], wuvt=lw["wuvt"], gkn=lw["gkn_col"])
    o_mla = _attention("mla", r3(q), srcs(None, None, layer, kpk, mv, latent=latent), ns=4, q_pos0=plen)
    o_sb = _attention("sb", r3(sq), srcs(c_skt, c_svt, layer, sk, sv, tk=min(SB_CACHE_TILE, plen), vt=True),
                      ns=4, q_pos0=plen)
    o_df = _attention("diff", r3(dq),
                      srcs(c_dkt, c_dv, layer, dk, dv, vt=False, vil=dims["df_w"] // LANES),
                      ns=4, q_pos0=plen, extras=(slopes, lw["lam"], lw["gsub"]), lam_init=lam_init)
    x2, conv = _tail(x2d, o_mla, o_sb, o_df, lw, seq=seq, hist=conv_state, layer=layer)
    return x2, (ckv, kr, sk, sv, dk, dv), conv


def kernel(x_prompt, x_sample, cache_mla_ckv, cache_mla_krope, cache_sb_k, cache_sb_v, cache_diff_k, cache_diff_v, state_ffn_conv, mix_norm_g, w_in, mla_q_norm_g, mla_w_uq, mla_kv_norm_g, mla_w_uk, mla_w_uv, mla_qn_g, mla_kn_g, mla_qr_g, mla_kr_g, diff_qn_g, diff_kn_g, diff_lambda, diff_subln_g, w_br_mla, w_br_sb, w_br_diff, w_out, ffn_norm_g, ffn_w_up, ffn_conv_w, ffn_conv_b, ffn_w_down):
    w = dict(mix_norm_g=mix_norm_g, w_in=w_in, mla_q_norm_g=mla_q_norm_g, mla_w_uq=mla_w_uq,
             mla_kv_norm_g=mla_kv_norm_g, mla_w_uk=mla_w_uk, mla_w_uv=mla_w_uv, mla_qn_g=mla_qn_g,
             mla_kn_g=mla_kn_g, mla_qr_g=mla_qr_g, mla_kr_g=mla_kr_g, diff_qn_g=diff_qn_g,
             diff_kn_g=diff_kn_g, diff_lambda=diff_lambda, diff_subln_g=diff_subln_g,
             w_br_mla=w_br_mla, w_br_sb=w_br_sb, w_br_diff=w_br_diff, w_out=w_out,
             ffn_norm_g=ffn_norm_g, ffn_w_up=ffn_w_up, ffn_conv_w=ffn_conv_w,
             ffn_conv_b=ffn_conv_b, ffn_w_down=ffn_w_down)
    depth = w_in.shape[0]
    bp, sp, d = x_prompt.shape
    bs, ss, _ = x_sample.shape
    past_len = cache_mla_ckv.shape[2]
    sb_heads, sb_dim = cache_sb_k.shape[3], cache_sb_k.shape[4]
    df_heads, df_dim = cache_diff_k.shape[3], cache_diff_k.shape[5]
    assert sb_dim == HEAD and df_dim == HEAD and ffn_conv_w.shape[1] == 3
    nope, rope = mla_qn_g.shape[1], mla_qr_g.shape[1]
    dims = dict(d_model=d, q_rank=mla_q_norm_g.shape[1], kv_rank=mla_kv_norm_g.shape[1],
                nope=nope, rope=rope, mla_heads=mla_w_uk.shape[2] // nope,
                sb_w=sb_heads * sb_dim, df_w=2 * df_heads * df_dim)
    sb_w, df_w = dims["sb_w"], dims["df_w"]

    tabs_p = _rope_tables(jnp.arange(sp, dtype=jnp.int32), rope, max(sp, ROW_TILE))
    tab_s, _ = _rope_tables(past_len + jnp.arange(ss, dtype=jnp.int32), rope, max(ss, ROW_TILE))
    slopes = 2.0 ** (-8.0 * jnp.arange(1, df_heads + 1, dtype=F32) / df_heads) * LOG2E

    feat = lambda c: jnp.moveaxis(c, 2, -1).reshape(depth, bs, -1, past_len)
    assert 2 * df_dim == LANES
    caches = (cache_mla_ckv, feat(cache_mla_krope), feat(cache_sb_k), feat(cache_sb_v),
              feat(cache_diff_k), cache_diff_v.reshape(depth, bs, past_len * df_heads, 2 * df_dim))

    xp = x_prompt.reshape(bp * sp, d)
    xs = x_sample.reshape(bs * ss, d)
    bufs, rows_s, conv_p, conv_s = None, [], [], []
    for l in range(depth):
        lw = _pack_layer(w, l, dims)
        xp, cp, bufs = _layer_prompt(xp, lw, dims, batch=bp, seq=sp, tabs=tabs_p, layer=l,
                                     depth=depth, bufs=bufs, slopes=slopes)
        xs, rs, cs = _layer_sample(xs, lw, dims, batch=bs, seq=ss, tab=tab_s, layer=l,
                                   caches=caches, conv_state=state_ffn_conv, slopes=slopes)
        rows_s.append(rs)
        conv_p.append(cp)
        conv_s.append(cs)

    p_ckv, p_krt, p_skt, p_svt, p_dkt, p_dvi = bufs
    tok = lambda t, tail: jnp.moveaxis(t.reshape((depth, bp) + tail + (sp,)), -1, 2)
    prompt_rows = (p_ckv, tok(p_krt, (rope,)), tok(p_skt, (sb_heads, sb_dim)),
                   tok(p_svt, (sb_heads, sb_dim)), tok(p_dkt, (df_heads, 2, df_dim)),
                   p_dvi.reshape(depth, bp, sp, df_heads, 2 * df_dim))

    def stack(i, tail):
        return jnp.stack([r[i] for r in rows_s]).reshape((depth, bs, ss) + tail)

    sample_rows = (stack(0, (dims["kv_rank"],)), stack(1, (rope,)), stack(2, (sb_heads, sb_dim)),
                   stack(3, (sb_heads, sb_dim)), stack(4, (df_heads, 2, df_dim)),
                   stack(5, (df_heads, 2 * df_dim)))
    return ((xp.reshape(bp, sp, d), xs.reshape(bs, ss, d)) + prompt_rows + (jnp.stack(conv_p),)
            + sample_rows + (jnp.stack(conv_s),))
```

```python
import functools
import math

import numpy as np
import jax
import jax.numpy as jnp
from jax import lax
from jax.experimental import pallas as pl
from jax.experimental.pallas import tpu as pltpu

F32 = jnp.float32
BF16 = jnp.bfloat16

EPS = 1e-6
CHUNK = 64
CHUNK_SHIFT = CHUNK.bit_length() - 1
assert 1 << CHUNK_SHIFT == CHUNK
ROPE_BASE = 10000.0
NEG_BIG = -1e30
HEAD = 64

LANES = 128
ROW_TILE = 512
KV_ROW_TILE = 1024
ATTN_TILE = 512
SB_BLOCK = 256
SB_CACHE_TILE = 1024
FF_CHUNK = 512
SB_SKIP_LOG2 = -160.0
LOG2E = math.log2(math.e)
VMEM_LIMIT = 52 * 1024 * 1024


def _cparams(n_axes):
    return pltpu.CompilerParams(dimension_semantics=("arbitrary",) * n_axes,
                                vmem_limit_bytes=VMEM_LIMIT)


def _const_spec(shape):
    nd = len(shape)
    return pl.BlockSpec(shape, lambda *_: (0,) * nd, pipeline_mode=pl.Buffered(1))


def _dot(a, b):
    return jnp.dot(a, b, preferred_element_type=F32)


def _dot_nt(a, b):
    return lax.dot_general(a, b, (((1,), (1,)), ((), ())), preferred_element_type=F32)


def _rmsnorm(x, g):
    return x * lax.rsqrt(jnp.mean(x * x, axis=-1, keepdims=True) + EPS) * g


def _group_mean_sq(x, gmat):
    sq = x * x
    hi = sq.astype(BF16)
    lo = (sq - hi.astype(F32)).astype(BF16)
    return _dot(hi, gmat) + _dot(lo, gmat)


def _row_group_rmsnorm(xt, gcol):
    r, n = xt.shape
    x3 = xt.reshape(r // HEAD, HEAD, n)
    y3 = x3 * lax.rsqrt(jnp.mean(x3 * x3, axis=1, keepdims=True) + EPS)
    return y3.reshape(r, n) * gcol


def _sigmoid(x):
    return 1.0 / (1.0 + jnp.exp(-x))


def _mla_query(cq, gcq_ref, wuq_ref, gm_mla_ref, gq_ref, cos_q, sin_q, q_ref, half):
    tm = cq.shape[0]
    cqn = _rmsnorm(cq, gcq_ref[...]).astype(BF16)
    q = _dot(cqn, wuq_ref[...])
    lane = lax.broadcasted_iota(jnp.int32, (tm, LANES), 1)
    for p in range(q.shape[1] // (2 * LANES)):
        lo = 2 * LANES * p
        blk = q[:, lo:lo + 2 * LANES]
        y = blk * lax.rsqrt(_group_mean_sq(blk, gm_mla_ref[...]) + EPS) * gq_ref[:, lo:lo + 2 * LANES]
        rp = y[:, LANES:]
        partner = jnp.where(lane < 2 * half, pltpu.roll(rp, LANES - 2 * half, 1),
                            pltpu.roll(rp, 2 * half, 1))
        q_ref[:, lo:lo + LANES] = y[:, :LANES].astype(BF16)
        q_ref[:, lo + LANES:lo + 2 * LANES] = (rp * cos_q + partner * sin_q).astype(BF16)


def _diff_query(zq, gm64_ref, gdq_ref, dq_ref):
    for s in range(zq.shape[1] // (2 * LANES)):
        lo = 2 * LANES * s
        blk = zq[:, lo:lo + 2 * LANES]
        r = lax.rsqrt(_group_mean_sq(blk, gm64_ref[...]) + EPS)
        dq_ref[:, lo:lo + 2 * LANES] = (blk * r * gdq_ref[:, lo:lo + 2 * LANES]).astype(BF16)


def _in_proj_kernel(x_ref, gmix_ref, w_ref, gcq_ref, wuq_ref, gm_mla_ref, gq_ref, gckv_ref,
                    gkr_ref, tab_ref, gm64_ref, gdq_ref, gdk_ref,
                    q_ref, ckv_ref, kr_ref, sq_ref, sk_ref, sv_ref, dq_ref, dk_ref, dv_ref,
                    *, q_rank, kv_rank, rope, sb_w, df_w):
    tm = x_ref.shape[0]
    half = rope // 2
    h = _rmsnorm(x_ref[...], gmix_ref[...]).astype(BF16)
    tab = tab_ref[...]
    cos_k, sin_k = tab[:, 2 * LANES:3 * LANES], tab[:, 3 * LANES:4 * LANES]
    lane = lax.broadcasted_iota(jnp.int32, (tm, LANES), 1)

    za = _dot(h, w_ref[:, 0:4 * LANES])
    _mla_query(za[:, 0:q_rank], gcq_ref, wuq_ref, gm_mla_ref, gq_ref, tab[:, 0:LANES],
               tab[:, LANES:2 * LANES], q_ref, half)
    ckv_ref[...] = _rmsnorm(za[:, q_rank:q_rank + kv_rank], gckv_ref[...])

    kr = za[:, q_rank + kv_rank:4 * LANES]
    ms = jnp.sum(kr * kr, axis=-1, keepdims=True) * (1.0 / rope)
    krn = kr * lax.rsqrt(ms + EPS) * gkr_ref[...]
    partner = jnp.where(lane < half, pltpu.roll(krn, LANES - half, 1), pltpu.roll(krn, half, 1))
    kr_ref[...] = (krn * cos_k + partner * sin_k)[:, :rope]

    o = 4 * LANES
    zs = _dot(h, w_ref[:, o:o + 3 * sb_w])
    sq_ref[...] = (zs[:, 0:sb_w] * (LOG2E / 8.0)).astype(BF16)
    sk_ref[...] = zs[:, sb_w:2 * sb_w]
    sv_ref[...] = zs[:, 2 * sb_w:3 * sb_w]

    o = o + 3 * sb_w
    zd = _dot(h, w_ref[:, o:o + 3 * df_w])
    _diff_query(zd[:, 0:df_w], gm64_ref, gdq_ref, dq_ref)
    for s in range(df_w // (2 * LANES)):
        lo = 2 * LANES * s
        blk = zd[:, df_w + lo:df_w + lo + 2 * LANES]
        r = lax.rsqrt(_group_mean_sq(blk, gm64_ref[...]) + EPS)
        dk_ref[:, lo:lo + 2 * LANES] = blk * r * gdk_ref[:, lo:lo + 2 * LANES]
    dv_ref[...] = zd[:, 2 * df_w:3 * df_w]


def _in_proj(x2d, lw, tab, dims):
    t, d = x2d.shape
    tm = ROW_TILE
    assert t % tm == 0 and tab.shape[0] % tm == 0
    n_tab = tab.shape[0] // tm
    row = lambda w: pl.BlockSpec((tm, w), lambda i: (i, 0))
    consts = [lw["gmix"], lw["w1"], lw["gcq"], lw["wuq"], lw["gm_mla"], lw["gq"], lw["gckv"], lw["gkr"]]
    consts2 = [lw["gm64"], lw["gdq"], lw["gdk"]]
    sb_w, df_w = dims["sb_w"], dims["df_w"]
    out_shape = [
        jax.ShapeDtypeStruct((t, lw["wuq"].shape[1]), BF16),
        jax.ShapeDtypeStruct((t, dims["kv_rank"]), F32),
        jax.ShapeDtypeStruct((t, dims["rope"]), F32),
        jax.ShapeDtypeStruct((t, sb_w), BF16),
        jax.ShapeDtypeStruct((t, sb_w), F32),
        jax.ShapeDtypeStruct((t, sb_w), F32),
        jax.ShapeDtypeStruct((t, df_w), BF16),
        jax.ShapeDtypeStruct((t, df_w), F32),
        jax.ShapeDtypeStruct((t, df_w), F32),
    ]
    kern = functools.partial(_in_proj_kernel, q_rank=dims["q_rank"], kv_rank=dims["kv_rank"],
                             rope=dims["rope"], sb_w=sb_w, df_w=df_w)
    return pl.pallas_call(
        kern,
        grid=(t // tm,),
        in_specs=([row(d)] + [_const_spec(c.shape) for c in consts]
                  + [pl.BlockSpec((tm, tab.shape[1]), lambda i: (i % n_tab, 0))]
                  + [_const_spec(c.shape) for c in consts2]),
        out_specs=[row(s.shape[1]) for s in out_shape],
        out_shape=out_shape,
        compiler_params=_cparams(1),
        name="in_proj",
    )(x2d, *consts, tab, *consts2)


def _in_proj_t_kernel(*refs, q_rank, kv_rank, rope, sb_w, df_w, n_alias):
    (x_ref, gmix_ref, wn_ref, wt_ref, gcq_ref, wuq_ref, gm_mla_ref, gq_ref, gckv_ref, gkr_ref,
     tab_ref, tabt_ref, gm64_ref, gdq_ref, gdk_ref) = refs[:15]
    (q_ref, sq_ref, dq_ref, ckv_ref, krt_ref, skt_ref, svt_ref, dkt_ref, dvi_ref) = refs[15 + n_alias:]
    half = rope // 2
    tm = x_ref.shape[0]
    h = _rmsnorm(x_ref[...], gmix_ref[...]).astype(BF16)
    tab = tab_ref[...]

    o = q_rank + kv_rank
    za = _dot(h, wn_ref[:, 0:o])
    _mla_query(za[:, 0:q_rank], gcq_ref, wuq_ref, gm_mla_ref, gq_ref, tab[:, 0:LANES],
               tab[:, LANES:2 * LANES], q_ref, half)
    ckv_ref[0, 0] = _rmsnorm(za[:, q_rank:o], gckv_ref[...])

    sq_ref[...] = (_dot(h, wn_ref[:, o:o + sb_w]) * (LOG2E / 8.0)).astype(BF16)
    o = o + sb_w
    zd = _dot(h, wn_ref[:, o:o + 2 * df_w])
    _diff_query(zd[:, 0:df_w], gm64_ref, gdq_ref, dq_ref)
    n_dv = df_w // LANES
    for hh in range(n_dv):
        dvi_ref[0, 0, pl.ds(hh, tm, stride=n_dv), :] = zd[:, df_w + LANES * hh:df_w + LANES * (hh + 1)]

    skt_ref[0, 0] = _dot_nt(wt_ref[0:sb_w, :], h)
    svt_ref[0, 0] = _dot_nt(wt_ref[sb_w:2 * sb_w, :], h)
    o = 2 * sb_w
    dkt_ref[0, 0] = _row_group_rmsnorm(_dot_nt(wt_ref[o:o + df_w, :], h), gdk_ref[...])
    o = o + df_w
    krt = _dot_nt(wt_ref[o:o + rope, :], h)
    krn = krt * lax.rsqrt(jnp.mean(krt * krt, axis=0, keepdims=True) + EPS) * gkr_ref[...]
    x1, x2 = krn[:half], krn[half:]
    cos_t, sin_t = tabt_ref[0:half, :], tabt_ref[half:rope, :]
    krt_ref[0, 0] = jnp.concatenate([x1 * cos_t - x2 * sin_t, x1 * sin_t + x2 * cos_t], axis=0)


def _in_proj_t(x2d, lw, tab, tabt, dims, *, batch, seq, layer, depth, prev):
    t, d = x2d.shape
    tm = ROW_TILE
    assert seq % tm == 0 and t == batch * seq
    nst = seq // tm
    row = lambda w: pl.BlockSpec((tm, w), lambda i: (i, 0))
    consts = [lw["gmix"], lw["wn"], lw["wt"], lw["gcq"], lw["wuq"], lw["gm_mla"], lw["gq"], lw["gckv"],
              lw["gkr_col"]]
    consts2 = [lw["gm64"], lw["gdq"], lw["gdk_col"]]
    sb_w, df_w, rope, kvr = dims["sb_w"], dims["df_w"], dims["rope"], dims["kv_rank"]
    n_alias = 0 if prev is None else len(prev)
    in_specs = ([row(d)] + [_const_spec(c.shape) for c in consts]
                + [pl.BlockSpec((tm, tab.shape[1]), lambda i: (i % nst, 0)),
                   pl.BlockSpec((rope, tm), lambda i: (0, i % nst))]
                + [_const_spec(c.shape) for c in consts2]
                + [pl.BlockSpec(memory_space=pl.ANY)] * n_alias)
    n_plain = len(in_specs) - n_alias
    tok = lambda w: pl.BlockSpec((1, 1, tm, w), lambda i: (layer, i // nst, i % nst, 0))
    feat = lambda r: pl.BlockSpec((1, 1, r, tm), lambda i: (layer, i // nst, 0, i % nst))
    n_dv = df_w // LANES
    out_shape = [
        jax.ShapeDtypeStruct((t, lw["wuq"].shape[1]), BF16),
        jax.ShapeDtypeStruct((t, sb_w), BF16),
        jax.ShapeDtypeStruct((t, df_w), BF16),
        jax.ShapeDtypeStruct((depth, batch, seq, kvr), F32),
        jax.ShapeDtypeStruct((depth, batch, rope, seq), F32),
        jax.ShapeDtypeStruct((depth, batch, sb_w, seq), F32),
        jax.ShapeDtypeStruct((depth, batch, sb_w, seq), F32),
        jax.ShapeDtypeStruct((depth, batch, df_w, seq), F32),
        jax.ShapeDtypeStruct((depth, batch, seq * n_dv, LANES), F32),
    ]
    out_specs = [row(out_shape[0].shape[1]), row(sb_w), row(df_w),
                 tok(kvr), feat(rope), feat(sb_w), feat(sb_w), feat(df_w),
                 pl.BlockSpec((1, 1, tm * n_dv, LANES), lambda i: (layer, i // nst, i % nst, 0))]
    kern = functools.partial(_in_proj_t_kernel, q_rank=dims["q_rank"], kv_rank=kvr, rope=rope,
                             sb_w=sb_w, df_w=df_w, n_alias=n_alias)
    return pl.pallas_call(
        kern,
        grid=(t // tm,),
        in_specs=in_specs,
        out_specs=out_specs,
        out_shape=out_shape,
        input_output_aliases={n_plain + j: 3 + j for j in range(n_alias)},
        compiler_params=_cparams(1),
        name="in_proj_t",
    )(x2d, *consts, tab, tabt, *consts2, *(prev or ()))


def _kv_up_kernel(ckv_ref, kr_ref, wukv_ref, gm64_ref, gkn_ref, rep_ref, k_ref, v_ref, *, kn_w):
    c = ckv_ref[...].astype(BF16)
    kv = _dot(c, wukv_ref[...])
    krrep = _dot(kr_ref[...].astype(BF16), rep_ref[...]).astype(BF16)
    for b in range(kn_w // (2 * LANES)):
        s = 2 * LANES * b
        blk = kv[:, s:s + 2 * LANES]
        r = lax.rsqrt(_group_mean_sq(blk, gm64_ref[...]) + EPS)
        kn = (blk * r * gkn_ref[:, s:s + 2 * LANES]).astype(BF16)
        for j in range(2):
            p = 2 * b + j
            k_ref[:, 2 * LANES * p:2 * LANES * p + LANES] = kn[:, LANES * j:LANES * (j + 1)]
            k_ref[:, 2 * LANES * p + LANES:2 * LANES * (p + 1)] = krrep
    v_ref[...] = kv[:, kn_w:].astype(BF16)


def _kv_up(ckv2d, kr2d, lw):
    r, kvr = ckv2d.shape
    tr = min(KV_ROW_TILE, r)
    assert r % tr == 0
    kn_w = lw["gkn"].shape[1]
    consts = [lw["wukv"], lw["gm64"], lw["gkn"], lw["rep"]]
    out_shape = [jax.ShapeDtypeStruct((r, 2 * kn_w), BF16),
                 jax.ShapeDtypeStruct((r, lw["wukv"].shape[1] - kn_w), BF16)]
    return pl.pallas_call(
        functools.partial(_kv_up_kernel, kn_w=kn_w),
        grid=(r // tr,),
        in_specs=[pl.BlockSpec((tr, kvr), lambda i: (i, 0)),
                  pl.BlockSpec((tr, kr2d.shape[1]), lambda i: (i, 0))]
                 + [_const_spec(c.shape) for c in consts],
        out_specs=[pl.BlockSpec((tr, s.shape[1]), lambda i: (i, 0)) for s in out_shape],
        out_shape=out_shape,
        compiler_params=_cparams(1),
        name="mla_kv_up",
    )(ckv2d, kr2d, *consts)


def _mla_kv_feature_major(c, kr, wukt, wuvt, gkn, kt_ref, vt_ref):
    half = kr.shape[0] // 2
    n = c.shape[0]
    knt = _row_group_rmsnorm(_dot_nt(wukt, c), gkn).astype(BF16)
    x1, x2 = kr[:half], kr[half:]
    krrep = jnp.concatenate([x1, x1, x2, x2, jnp.zeros((LANES - 4 * half, n), BF16)], axis=0)
    for p in range(knt.shape[0] // LANES):
        kt_ref[2 * LANES * p:2 * LANES * p + LANES, :] = knt[LANES * p:LANES * (p + 1)]
        kt_ref[2 * LANES * p + LANES:2 * LANES * (p + 1), :] = krrep
    vt_ref[...] = _dot_nt(wuvt, c).astype(BF16)


def _stream_lane_masks(kind, ns, width):
    lane = lax.broadcasted_iota(jnp.int32, (1, width), 1)
    masks = []
    for i in range(ns):
        if kind == "mla":
            r = lane - 2 * LANES * (i // 2)
            j = i % 2
            masks.append(((r >= HEAD * j) & (r < HEAD * (j + 1)))
                         | ((r >= 128 + 16 * j) & (r < 128 + 16 * (j + 1)))
                         | ((r >= 160 + 16 * j) & (r < 160 + 16 * (j + 1))))
        else:
            masks.append((lane >= HEAD * i) & (lane < HEAD * (i + 1)))
    return masks


def _attn_kernel(*refs, kind, srcs, ns, tq, q_pos0, lam_init, ids=None):
    it = iter(refs)
    slopes_ref = next(it) if kind == "diff" else None
    q_ref = next(it)
    kv_refs = [tuple(next(it) for _ in range(5 if s["latent"] else 2)) for s in srcs]
    if kind == "diff":
        lam_ref, gsub_ref = next(it), next(it)
    o_ref = next(it)
    scr = [(next(it), next(it)) if s["cast"] else None for s in srcs]

    grp, qi = ids if ids is not None else (pl.program_id(1), pl.program_id(2))
    rows = ns * tq
    vw = HEAD * ns

    for s, sc, in_refs in zip(srcs, scr, kv_refs):
        if s["latent"]:
            @pl.when(qi == 0)
            def _(s=s, sc=sc, in_refs=in_refs):
                ckv_ref, krt_ref, wukt_ref, wuvt_ref, gkn_ref = in_refs
                _mla_kv_feature_major(ckv_ref[0, 0].astype(BF16), krt_ref[0, 0].astype(BF16), wukt_ref[...],
                                      wuvt_ref[...], gkn_ref[...], sc[0], sc[1])
        elif sc is not None:
            k_ref, v_ref = in_refs

            @pl.when(qi == 0)
            def _(s=s, sc=sc, k_ref=k_ref, v_ref=v_ref):
                sc[0][...] = k_ref[0, 0].astype(BF16)
                if s["vil"]:
                    for j in range(vw // LANES):
                        rows_j = pl.ds(grp * (vw // LANES) + j, s["sk"], stride=s["vil"])
                        sc[1][:, LANES * j:LANES * (j + 1)] = v_ref[0, 0, rows_j, :].astype(BF16)
                else:
                    sc[1][...] = v_ref[0, 0].astype(BF16)

    q = q_ref[0]
    zero = jnp.zeros_like(q)
    qs = jnp.concatenate([jnp.where(m, q, zero) for m in _stream_lane_masks(kind, ns, q.shape[1])],
                         axis=0)
    q_start = q_pos0 + qi * tq
    qpos = jnp.concatenate([q_start + lax.broadcasted_iota(jnp.int32, (tq, 1), 0)] * ns, axis=0)
    if kind == "diff":
        slope = jnp.concatenate(
            [jnp.full((tq, 1), slopes_ref[grp * (ns // 2) + i // 2], F32) for i in range(ns)], axis=0)

    if kind == "sb":
        init = (jnp.zeros((rows, 1), F32), jnp.zeros((rows, vw), F32))
    else:
        init = (jnp.full((rows, 1), NEG_BIG, F32), jnp.zeros((rows, 1), F32),
                jnp.zeros((rows, vw), F32))

    def tile_step(st, k_t, v_t, k_start, src, masked, upper):
        tk = src["tk"]
        kpos = k_start + lax.broadcasted_iota(jnp.int32, (1, tk), 1)
        s = _dot(qs, k_t) if src["kt"] else _dot_nt(qs, k_t)
        pv = (lambda p: _dot_nt(p, v_t)) if src["vt"] else (lambda p: _dot(p, v_t))
        if masked:
            valid = (kpos < qpos) if kind == "sb" else ((kpos >> CHUNK_SHIFT) <= (qpos >> CHUNK_SHIFT))
        if kind == "sb":
            c, acc = st
            lp = jnp.log(1.0 + jnp.exp2(-jnp.abs(s))) * LOG2E
            ls = jnp.minimum(s, 0.0) - lp
            l1 = ls - s
            if masked:
                l1 = jnp.where(valid, l1, 0.0)
            hi = l1.astype(BF16)
            lo = (l1 - hi.astype(F32)).astype(BF16)
            cb = upper.shape[0]
            nb = tk // cb
            if nb > 1:
                hi = jnp.concatenate([hi[:, cb * j:cb * (j + 1)] for j in range(nb)], axis=0)
                lo = jnp.concatenate([lo[:, cb * j:cb * (j + 1)] for j in range(nb)], axis=0)
            suffix = _dot(hi, upper) + _dot(lo, upper)
            parts = [None] * nb
            for j in reversed(range(nb)):
                parts[j] = jnp.exp2(ls[:, cb * j:cb * (j + 1)] + suffix[rows * j:rows * (j + 1)] + c)
                c = c + jnp.sum(l1[:, cb * j:cb * (j + 1)], axis=-1, keepdims=True)
            a = parts[0] if nb == 1 else jnp.concatenate(parts, axis=1)
            if masked:
                a = jnp.where(valid, a, 0.0)
            return (c, acc + pv(a.astype(BF16)))
        m, l, acc = st
        if kind == "diff":
            dist = qpos - kpos
            s = s - slope * (jnp.abs(dist) if masked else dist).astype(F32)
        if masked:
            s = jnp.where(valid, s, -jnp.inf)
        m_new = jnp.maximum(m, jnp.max(s, axis=-1, keepdims=True))
        alpha = jnp.exp2(m - m_new)
        p = jnp.exp2(s - m_new)
        l = alpha * l + jnp.sum(p, axis=-1, keepdims=True)
        return (m_new, l, alpha * acc + pv(p.astype(BF16)))

    def upper_tri(tk):
        if kind != "sb":
            return None
        cb = min(tk, SB_BLOCK)
        r = lax.broadcasted_iota(jnp.int32, (cb, cb), 0)
        c = lax.broadcasted_iota(jnp.int32, (cb, cb), 1)
        return jnp.where(r > c, 1.0, 0.0).astype(BF16)

    st = init
    for s, sc, in_refs in reversed(list(zip(srcs, scr, kv_refs))):
        k_ref, v_ref = (None, None) if s["latent"] else in_refs
        tk = s["tk"]
        upper = upper_tri(tk)

        def load(k_off, src, _sc=sc, _k=k_ref, _v=v_ref):
            def one(ref4, ref2, transposed):
                win = pl.ds(k_off, src["tk"])
                if ref2 is not None:
                    return ref2[:, win] if transposed else ref2[win, :]
                return ref4[0, 0, :, win] if transposed else ref4[0, 0, win, :]
            return (one(_k, _sc and _sc[0], src["kt"]), one(_v, _sc and _sc[1], src["vt"]))

        if s["mode"] == "self":
            k_t, v_t = load(pl.multiple_of(qi * tk, tk), s)
            st = tile_step(st, k_t, v_t, s["pos0"] + qi * tk, s, True, upper)
            if kind == "sb" and tk > SB_BLOCK:
                s = dict(s, tk=SB_BLOCK)
            n_full = qi * (tk // s["tk"])
        elif s["mode"] == "masked":
            k_t, v_t = load(0, s)
            st = tile_step(st, k_t, v_t, s["pos0"], s, True, upper)
            continue
        else:
            n_full = s["sk"] // tk

        def body(i, st_, _s=s, _load=load, _n=n_full, _upper=upper):
            kj = _n - 1 - i
            k_t, v_t = _load(pl.multiple_of(kj * _s["tk"], _s["tk"]), _s)
            return tile_step(st_, k_t, v_t, _s["pos0"] + kj * _s["tk"], _s, False, _upper)

        if isinstance(n_full, int) and n_full <= 2:
            for i in range(n_full):
                st = body(i, st)
        elif kind == "sb":
            def live(st_):
                return (jnp.max(st_[0]) > SB_SKIP_LOG2).astype(jnp.int32)

            def wbody(carry, _body=body):
                i, _, st_ = carry
                st_ = _body(i, st_)
                return i + 1, live(st_), st_

            _, _, st = lax.while_loop(lambda carry, _n=n_full: (carry[0] < _n) & (carry[1] > 0),
                                      wbody, (jnp.int32(0), live(st), st))
        else:
            st = lax.fori_loop(0, n_full, body, st)

    acc = st[-1]
    if kind == "diff":
        l = st[1]
        lv = lam_ref[...]
        lam = (jnp.exp(jnp.sum(lv[0:1] * lv[1:2], axis=-1, keepdims=True))
               - jnp.exp(jnp.sum(lv[2:3] * lv[3:4], axis=-1, keepdims=True)) + lam_init)
        for hh in range(ns // 2):
            lanes = slice(2 * HEAD * hh, 2 * HEAD * (hh + 1))
            r0, r1 = slice(tq * 2 * hh, tq * (2 * hh + 1)), slice(tq * (2 * hh + 1), tq * (2 * hh + 2))
            o = acc[r0, lanes] / l[r0] - lam * (acc[r1, lanes] / l[r1])
            o_ref[0, :, lanes] = (_rmsnorm(o, gsub_ref[...]) * (1.0 - lam_init)).astype(BF16)
    else:
        lane = lax.broadcasted_iota(jnp.int32, (tq, vw), 1)
        o = jnp.zeros((tq, vw), F32)
        for i in range(ns):
            blk = acc[tq * i:tq * (i + 1)]
            if kind == "mla":
                blk = blk / st[1][tq * i:tq * (i + 1)]
            o = jnp.where((lane >= HEAD * i) & (lane < HEAD * (i + 1)), blk, o)
        o_ref[0] = o.astype(BF16)


def _attention_plan(kind, q3, sources, *, ns, q_pos0, extras=(), lam_init=0.0):
    b, sq, qtot = q3.shape
    w = (2 * HEAD if kind == "mla" else HEAD) * ns
    vw = HEAD * ns
    n_groups = qtot // w
    tq = min(ATTN_TILE, sq)
    assert sq % tq == 0 and qtot % w == 0
    srcs, in_specs, args, scratch = [], [], [], []
    if kind == "diff":
        in_specs.append(pl.BlockSpec(memory_space=pltpu.SMEM))
        args.append(extras[0])
    in_specs.append(pl.BlockSpec((1, tq, w), lambda bi, g, qi: (bi, qi, g)))
    args.append(q3)

    def kv_spec(arr, width, transposed, layer):
        if transposed:
            assert arr.shape[2] == n_groups * width
            return pl.BlockSpec((1, 1, width, arr.shape[3]), lambda bi, g, qi: (layer, bi, g, 0))
        assert arr.shape[3] == n_groups * width
        return pl.BlockSpec((1, 1, arr.shape[2], width), lambda bi, g, qi: (layer, bi, 0, g))

    for s in sources:
        lat = s.get("latent")
        if lat is not None:
            assert kind == "mla" and s["mode"] in ("full", "self")
            ckv, krt, layer = lat["ckv"], lat["krt"], s["layer"]
            sk = ckv.shape[2]
            tk = tq if s["mode"] == "self" else s["tk"]
            assert sk % tk == 0
            srcs.append(dict(mode=s["mode"], pos0=s["pos0"], sk=sk, tk=tk, cast=True, kt=True, vt=True,
                             vil=0, latent=True))
            in_specs += [pl.BlockSpec((1, 1, sk, ckv.shape[3]), lambda bi, g, qi, _l=layer: (_l, bi, 0, 0)),
                         pl.BlockSpec((1, 1, krt.shape[2], sk), lambda bi, g, qi, _l=layer: (_l, bi, 0, 0)),
                         pl.BlockSpec((w // 2, ckv.shape[3]), lambda bi, g, qi: (g, 0)),
                         pl.BlockSpec((vw, ckv.shape[3]), lambda bi, g, qi: (g, 0)),
                         pl.BlockSpec((w // 2, 1), lambda bi, g, qi: (g, 0))]
            args += [ckv, krt, lat["wukt"], lat["wuvt"], lat["gkn"]]
            scratch += [pltpu.VMEM((w, sk), BF16), pltpu.VMEM((vw, sk), BF16)]
            continue
        k, v, kt, vt = s["k"], s["v"], s["kt"], s["vt"]
        sk = k.shape[3] if kt else k.shape[2]
        tk = tq if s["mode"] == "self" else (sk if s["mode"] == "masked" else s.get("tk", min(ATTN_TILE, sk)))
        assert sk % tk == 0 and (tk <= SB_BLOCK or tk % SB_BLOCK == 0)
        cast = k.dtype != BF16
        assert cast == (v.dtype != BF16)
        vil = s.get("vil", 0)
        vlayer = s.get("vlayer", s["layer"])
        srcs.append(dict(mode=s["mode"], pos0=s["pos0"], sk=sk, tk=tk, cast=cast, kt=kt, vt=vt, vil=vil,
                         latent=False))
        if vil:
            assert cast and not vt and v.shape[2:] == (sk * vil, LANES) and vw % LANES == 0
            v_spec = pl.BlockSpec((1, 1, sk * vil, LANES), lambda bi, g, qi, _l=vlayer: (_l, bi, 0, 0))
        else:
            v_spec = kv_spec(v, vw, vt, vlayer)
        in_specs += [kv_spec(k, w, kt, s["layer"]), v_spec]
        args += [k, v]
        if cast:
            scratch += [pltpu.VMEM((w, sk) if kt else (sk, w), BF16),
                        pltpu.VMEM((vw, sk) if vt else (sk, vw), BF16)]
    if kind == "diff":
        in_specs += [_const_spec(extras[1].shape), _const_spec(extras[2].shape)]
        args += [extras[1], extras[2]]
    kern = functools.partial(_attn_kernel, kind=kind, srcs=srcs, ns=ns, tq=tq, q_pos0=q_pos0,
                             lam_init=lam_init)
    return dict(kern=kern, grid=(b, n_groups, sq // tq), in_specs=in_specs, args=args,
                out_spec=pl.BlockSpec((1, tq, vw), lambda bi, g, qi: (bi, qi, g)),
                out_shape=jax.ShapeDtypeStruct((b, sq, n_groups * vw), BF16),
                scratch=scratch, name="attn_" + kind)


def _remap(spec, ids):
    if spec.index_map is None:
        return spec
    return pl.BlockSpec(spec.block_shape, lambda *g, _m=spec.index_map: _m(*ids(*g)),
                        pipeline_mode=spec.pipeline_mode, memory_space=spec.memory_space)


def _attention_fused(pp, ps):
    bp, gp, nq = pp["grid"]
    bs, gs, one = ps["grid"]
    assert one == 1 and bp * gp == bs * gs and nq >= 2
    n_pi, n_si, n_ps = len(pp["in_specs"]), len(ps["in_specs"]), len(pp["scratch"])

    def p_ids(bi, g, qi):
        return bi, g, jnp.maximum(qi - 1, 0)

    def s_ids(bi, g, qi, ahead=True):
        idx = bi * gp + g
        if ahead:
            idx = jnp.minimum(idx + jnp.where(qi >= 2, 1, 0), bs * gs - 1)
        return idx // gs, idx % gs, 0

    def kern(*refs):
        ins_p, ins_s = refs[:n_pi], refs[n_pi:n_pi + n_si]
        out_p, out_s = refs[n_pi + n_si:n_pi + n_si + 2]
        scr = refs[n_pi + n_si + 2:]
        bi, g, qi = pl.program_id(0), pl.program_id(1), pl.program_id(2)

        @pl.when(qi == 0)
        def _():
            ps["kern"](*ins_s, out_s, *scr[n_ps:], ids=((bi * gp + g) % gs, jnp.int32(0)))

        @pl.when(qi > 0)
        def _():
            pp["kern"](*ins_p, out_p, *scr[:n_ps], ids=(g, qi - 1))

    return pl.pallas_call(
        kern,
        grid=(bp, gp, nq + 1),
        in_specs=[_remap(s, p_ids) for s in pp["in_specs"]] + [_remap(s, s_ids) for s in ps["in_specs"]],
        out_specs=[_remap(pp["out_spec"], p_ids),
                   _remap(ps["out_spec"], functools.partial(s_ids, ahead=False))],
        out_shape=[pp["out_shape"], ps["out_shape"]],
        scratch_shapes=pp["scratch"] + ps["scratch"],
        compiler_params=_cparams(3),
        name=pp["name"] + "_fused",
    )(*pp["args"], *ps["args"])


def _merge_kernel(x_ref, gmix_ref, wg_ref, om_ref, os_ref, od_ref, wbr_ref, wout_ref, o_ref):
    x = x_ref[...]
    d = x.shape[1]
    h = _rmsnorm(x, gmix_ref[...]).astype(BF16)
    merged = None
    for i, br_ref in enumerate((om_ref, os_ref, od_ref)):
        gate = _sigmoid(_dot(h, wg_ref[:, i * d:(i + 1) * d]))
        term = gate * _dot(br_ref[...], wbr_ref[i])
        merged = term if merged is None else merged + term
    o_ref[...] = x + _dot(merged.astype(BF16), wout_ref[...])


def _merge(x2d, o_mla, o_sb, o_diff, lw):
    t, d = x2d.shape
    tm = ROW_TILE
    row = lambda w: pl.BlockSpec((tm, w), lambda i: (i, 0))
    return pl.pallas_call(
        _merge_kernel,
        grid=(t // tm,),
        in_specs=[row(d), _const_spec(lw["gmix"].shape), _const_spec(lw["wg"].shape),
                  row(o_mla.shape[1]), row(o_sb.shape[1]), row(o_diff.shape[1]),
                  _const_spec(lw["wbr"].shape), _const_spec(lw["wout"].shape)],
        out_specs=row(d),
        out_shape=jax.ShapeDtypeStruct((t, d), F32),
        compiler_params=_cparams(1),
        name="merge",
    )(x2d, lw["gmix"], lw["wg"], o_mla, o_sb, o_diff, lw["wbr"], lw["wout"])


def _conv_rows(a, h0, h1, cw, cb):
    n = a.shape[0]
    row = lax.broadcasted_iota(jnp.int32, (n, 1), 0)
    p1 = jnp.where(row == 0, h1, pltpu.roll(a, 1, 0))
    p2 = jnp.where(row == 0, h0, jnp.where(row == 1, h1, pltpu.roll(a, 2, 0)))
    return p2 * cw[0:1] + p1 * cw[1:2] + a * cw[2:3] + cb


def _ffn_kernel(*refs, seg, d_ff, has_hist):
    it = iter(refs)
    x_ref, g_ref, wup_ref, cw_ref, cb_ref = next(it), next(it), next(it), next(it), next(it)
    hist_ref = next(it) if has_hist else None
    wdn_ref, o_ref, conv_ref = next(it), next(it), next(it)
    carry_ref = next(it) if not has_hist else None
    act_ref = next(it) if has_hist else None

    x = x_ref[...]
    tm = x.shape[0]
    h = _rmsnorm(x, g_ref[...]).astype(BF16)
    cw, cb = cw_ref[...], cb_ref[...]

    if not has_hist:
        i = pl.program_id(0)
        tiles_per_seq = seg // tm

        @pl.when((i % tiles_per_seq) == 0)
        def _():
            carry_ref[...] = jnp.zeros_like(carry_ref)

        tail = carry_ref[...]
        y, tails = x, []
        chunks = [(c0, min(FF_CHUNK, d_ff - c0)) for c0 in range(0, d_ff, FF_CHUNK)]
        up = lambda c0, cs: (_dot(h, wup_ref[:, c0:c0 + cs]), _dot(h, wup_ref[:, d_ff + c0:d_ff + c0 + cs]))
        nxt = up(*chunks[0])
        for idx, (c0, cs) in enumerate(chunks):
            a, u = nxt
            if idx + 1 < len(chunks):
                nxt = up(*chunks[idx + 1])
            c = _conv_rows(a, tail[6:7, c0:c0 + cs], tail[7:8, c0:c0 + cs], cw[:, c0:c0 + cs],
                           cb[:, c0:c0 + cs])
            tails.append(a[tm - 8:tm])
            y = y + _dot((c * _sigmoid(c) * u).astype(BF16), wdn_ref[c0:c0 + cs, :])
        new_tail = jnp.concatenate(tails, axis=1)
        carry_ref[...] = new_tail

        @pl.when((i % tiles_per_seq) == tiles_per_seq - 1)
        def _():
            conv_ref[0] = new_tail[6:8]

        o_ref[...] = y
    else:
        a = _dot(h, wup_ref[:, 0:d_ff])
        u = _dot(h, wup_ref[:, d_ff:2 * d_ff])
        for s in range(tm // seg):
            a_s = a[s * seg:(s + 1) * seg]
            hs = hist_ref[0, s]
            c = _conv_rows(a_s, hs[0:1], hs[1:2], cw, cb)
            act_ref[s * seg:(s + 1) * seg, :] = (c * _sigmoid(c) * u[s * seg:(s + 1) * seg]).astype(BF16)
            conv_ref[s] = a_s[seg - 2:seg]
        o_ref[...] = x + _dot(act_ref[...], wdn_ref[...])


def _ffn(x2d, lw, *, seg, hist=None, layer=0):
    t, d = x2d.shape
    tm = ROW_TILE
    d_ff = lw["wdn"].shape[0]
    has_hist = hist is not None
    row = lambda w: pl.BlockSpec((tm, w), lambda i: (i, 0))
    in_specs = [row(d), _const_spec(lw["gffn"].shape), _const_spec(lw["wup"].shape),
                _const_spec(lw["cw"].shape), _const_spec(lw["cb"].shape)]
    args = [x2d, lw["gffn"], lw["wup"], lw["cw"], lw["cb"]]
    if has_hist:
        assert tm % seg == 0 and seg % 8 == 0
        nseq = tm // seg
        in_specs.append(pl.BlockSpec((1, nseq, 2, d_ff), lambda i: (layer, i, 0, 0)))
        args.append(hist)
        conv_spec = pl.BlockSpec((nseq, 2, d_ff), lambda i: (i, 0, 0))
        scratch = [pltpu.VMEM((tm, d_ff), BF16)]
    else:
        assert seg % tm == 0
        tps = seg // tm
        conv_spec = pl.BlockSpec((1, 2, d_ff), lambda i: (i // tps, 0, 0))
        scratch = [pltpu.VMEM((8, d_ff), F32)]
    in_specs.append(_const_spec(lw["wdn"].shape))
    args.append(lw["wdn"])
    return pl.pallas_call(
        functools.partial(_ffn_kernel, seg=seg, d_ff=d_ff, has_hist=has_hist),
        grid=(t // tm,),
        in_specs=in_specs,
        out_specs=[row(d), conv_spec],
        out_shape=[jax.ShapeDtypeStruct((t, d), F32),
                   jax.ShapeDtypeStruct((t // seg, 2, d_ff), F32)],
        scratch_shapes=scratch,
        compiler_params=_cparams(1),
        name="ffn",
    )(*args)


def _group_matrix(groups, width):
    m = np.zeros((width, width), np.float32)
    for g in groups:
        for i in g:
            m[i, g] = 1.0 / len(g)
    return jnp.asarray(m, dtype=BF16)


def _pack_layer(w, l, dims):
    nope, rope, heads = dims["nope"], dims["rope"], dims["mla_heads"]
    q_rank, kv_rank = dims["q_rank"], dims["kv_rank"]
    sb_w, df_w = dims["sb_w"], dims["df_w"]
    half = rope // 2
    assert nope == HEAD and rope == 32 and heads % 2 == 0 and q_rank + kv_rank + rope <= 4 * LANES
    assert (q_rank + kv_rank) % LANES == 0 and sb_w % (2 * LANES) == 0 and df_w % (2 * LANES) == 0
    w_in = w["w_in"][l]
    o_kr = q_rank + kv_rank
    o_sb = o_kr + rope
    o_df = o_sb + 3 * sb_w
    o_g = o_df + 3 * df_w
    col = lambda a, b: w_in[:, a:b]
    wa = jnp.pad(col(0, o_sb), ((0, 0), (0, 4 * LANES - o_sb)))
    w1 = jnp.concatenate([wa, col(o_sb, o_g)], axis=1).astype(BF16)
    wn = jnp.concatenate([col(0, o_kr), col(o_sb, o_sb + sb_w), col(o_df, o_df + df_w),
                          col(o_df + 2 * df_w, o_g)], axis=1).astype(BF16)
    wt = jnp.concatenate([col(o_sb + sb_w, o_df), col(o_df + df_w, o_df + 2 * df_w),
                          col(o_kr, o_sb)], axis=1).T.astype(BF16)

    cols = np.full((heads // 2) * 2 * LANES, -1, np.int64)
    gain_src = np.full(cols.shape, -1, np.int64)
    for p in range(heads // 2):
        base = 2 * LANES * p
        for j, hd in enumerate((2 * p, 2 * p + 1)):
            hb = hd * (nope + rope)
            cols[base + nope * j:base + nope * (j + 1)] = hb + np.arange(nope)
            gain_src[base + nope * j:base + nope * (j + 1)] = np.arange(nope)
            x1 = base + 2 * nope + half * j
            x2 = base + 2 * nope + 2 * half + half * j
            cols[x1:x1 + half] = hb + nope + np.arange(half)
            cols[x2:x2 + half] = hb + nope + half + np.arange(half)
            gain_src[x1:x1 + half] = nope + np.arange(half)
            gain_src[x2:x2 + half] = nope + half + np.arange(half)
    valid = jnp.asarray(cols >= 0)
    wuq = jnp.where(valid[None, :], w["mla_w_uq"][l][:, np.maximum(cols, 0)], 0.0).astype(BF16)
    qg = jnp.concatenate([w["mla_qn_g"][l], w["mla_qr_g"][l]])
    gq = jnp.where(valid, qg[np.maximum(gain_src, 0)], 0.0) * ((nope + rope) ** -0.5 * LOG2E)

    groups = [list(range(0, nope)), list(range(nope, 2 * nope))]
    for j in range(2):
        groups.append(list(range(2 * nope + half * j, 2 * nope + half * (j + 1)))
                      + list(range(2 * nope + 2 * half + half * j, 2 * nope + 2 * half + half * (j + 1))))
    gm_mla = _group_matrix(groups, 2 * LANES)
    gm64 = _group_matrix([list(range(HEAD * j, HEAD * (j + 1))) for j in range(4)], 2 * LANES)

    rep = np.zeros((rope, LANES), np.float32)
    for j in range(half):
        rep[j, [j, half + j]] = 1.0
        rep[half + j, [2 * half + j, 3 * half + j]] = 1.0

    row = lambda v: v.reshape(1, -1).astype(F32)
    colv = lambda v: v.reshape(-1, 1).astype(F32)
    gdk = jnp.tile(w["diff_kn_g"][l], df_w // HEAD)
    gkn = jnp.tile(w["mla_kn_g"][l], heads)
    return dict(
        gmix=row(w["mix_norm_g"][l]), w1=w1, wn=wn, wt=wt, gcq=row(w["mla_q_norm_g"][l]), wuq=wuq,
        gm_mla=gm_mla, gq=row(gq), gckv=row(w["mla_kv_norm_g"][l]),
        gkr=row(jnp.pad(w["mla_kr_g"][l], (0, LANES - rope))), gkr_col=colv(w["mla_kr_g"][l]),
        gm64=gm64,
        gdq=row(jnp.tile(w["diff_qn_g"][l], df_w // HEAD) * (LOG2E / 8.0)),
        gdk=row(gdk), gdk_col=colv(gdk),
        wukv=jnp.concatenate([w["mla_w_uk"][l], w["mla_w_uv"][l]], axis=1).astype(BF16),
        wukt=w["mla_w_uk"][l].T.astype(BF16), wuvt=w["mla_w_uv"][l].T.astype(BF16),
        gkn=row(gkn), gkn_col=colv(gkn),
        rep=jnp.asarray(rep, dtype=BF16),
        lam=w["diff_lambda"][l].astype(F32), gsub=row(w["diff_subln_g"][l]),
        wg=w_in[:, o_g:].astype(BF16),
        wbr=jnp.stack([w["w_br_mla"][l], w["w_br_sb"][l], w["w_br_diff"][l]]).astype(BF16),
        wout=w["w_out"][l].astype(BF16),
        gffn=row(w["ffn_norm_g"][l]), wup=w["ffn_w_up"][l].astype(BF16),
        cw=w["ffn_conv_w"][l].astype(F32), cb=row(w["ffn_conv_b"][l]),
        wdn=w["ffn_w_down"][l].astype(BF16),
    )


def _rope_tables(pos, rope, n_rows):
    half = rope // 2
    inv = ROPE_BASE ** (-jnp.arange(half, dtype=F32) / half)
    ang = pos.astype(F32)[:, None] * inv[None, :]
    cos, sin = jnp.cos(ang), jnp.sin(ang)
    z = lambda n: jnp.zeros((pos.shape[0], n), F32)
    tab = jnp.concatenate([
        cos, cos, cos, cos, z(LANES - 4 * half),
        -sin, -sin, sin, sin, z(LANES - 4 * half),
        cos, cos, z(LANES - 2 * half),
        -sin, sin, z(LANES - 2 * half)], axis=1)
    return jnp.tile(tab, (n_rows // pos.shape[0], 1)), jnp.concatenate([cos.T, sin.T], axis=0)


def _tail(x2d, o_mla, o_sb, o_df, lw, *, seq, hist=None, layer=0):
    f2 = lambda a: a.reshape(x2d.shape[0], a.shape[-1])
    x1 = _merge(x2d, f2(o_mla), f2(o_sb), f2(o_df), lw)
    return _ffn(x1, lw, seg=seq, hist=hist, layer=layer)


def _layer(xp, xs, lw, dims, *, bp, sp, bs, ss, tabs_p, tab_s, layer, depth, bufs, caches, conv_state,
           slopes):
    q, sq, dq, *bufs = _in_proj_t(xp, lw, tabs_p[0], tabs_p[1], dims, batch=bp, seq=sp,
                                  layer=layer, depth=depth, prev=bufs)
    ckv, krt, skt, svt, dkt, dvi = bufs
    qs_, ckv_s, kr_s, sqs, sks, svs, dqs, dks, dvs = _in_proj(xs, lw, tab_s, dims)
    kpk, mv = _kv_up(ckv_s, kr_s, lw)
    c_ckv, c_krt, c_skt, c_svt, c_dkt, c_dv = caches
    plen = c_ckv.shape[2]
    n_dv = dims["df_w"] // LANES
    p3 = lambda a: a.reshape(bp, sp, a.shape[-1])
    s3 = lambda a: a.reshape(bs, ss, a.shape[-1])
    s4 = lambda a: a.reshape(1, bs, ss, a.shape[-1])
    lam_init = 0.8 - 0.6 * math.exp(-0.3 * layer)
    extras = (slopes, lw["lam"], lw["gsub"])

    def own(k, v, **kw):
        return [dict(k=k, v=v, layer=layer, kt=True, pos0=0, mode="self", **kw)]

    def past(ck, cv, nk, nv, tk=plen, **kw):
        return [dict(k=ck, v=cv, layer=layer, kt=True, pos0=0, mode="full", tk=tk, **kw),
                dict(k=s4(nk), v=s4(nv), layer=0, kt=False, vt=False, pos0=plen, mode="masked")]

    lat = lambda c, kr: dict(ckv=c, krt=kr, wukt=lw["wukt"], wuvt=lw["wuvt"], gkn=lw["gkn_col"])
    o_mla, o_mla_s = _attention_fused(
        _attention_plan("mla", p3(q), own(None, None, latent=lat(ckv, krt)), ns=2, q_pos0=0),
        _attention_plan("mla", s3(qs_), past(None, None, kpk, mv, latent=lat(c_ckv, c_krt)), ns=4,
                        q_pos0=plen))
    o_sb, o_sb_s = _attention_fused(
        _attention_plan("sb", p3(sq), own(skt, svt, vt=True), ns=2, q_pos0=0),
        _attention_plan("sb", s3(sqs), past(c_skt, c_svt, sks, svs, tk=min(SB_CACHE_TILE, plen), vt=True),
                        ns=4, q_pos0=plen))
    o_df, o_df_s = _attention_fused(
        _attention_plan("diff", p3(dq), own(dkt, dvi, vt=False, vil=n_dv), ns=2, q_pos0=0,
                        extras=extras, lam_init=lam_init),
        _attention_plan("diff", s3(dqs), past(c_dkt, c_dv, dks, dvs, vt=False, vil=n_dv), ns=4,
                        q_pos0=plen, extras=extras, lam_init=lam_init))
    xp2, conv_p = _tail(xp, o_mla, o_sb, o_df, lw, seq=sp)
    xs2, conv_s = _tail(xs, o_mla_s, o_sb_s, o_df_s, lw, seq=ss, hist=conv_state, layer=layer)
    return xp2, xs2, tuple(bufs), (ckv_s, kr_s, sks, svs, dks, dvs), conv_p, conv_s


def kernel(x_prompt, x_sample, cache_mla_ckv, cache_mla_krope, cache_sb_k, cache_sb_v, cache_diff_k, cache_diff_v, state_ffn_conv, mix_norm_g, w_in, mla_q_norm_g, mla_w_uq, mla_kv_norm_g, mla_w_uk, mla_w_uv, mla_qn_g, mla_kn_g, mla_qr_g, mla_kr_g, diff_qn_g, diff_kn_g, diff_lambda, diff_subln_g, w_br_mla, w_br_sb, w_br_diff, w_out, ffn_norm_g, ffn_w_up, ffn_conv_w, ffn_conv_b, ffn_w_down):
    w = dict(mix_norm_g=mix_norm_g, w_in=w_in, mla_q_norm_g=mla_q_norm_g, mla_w_uq=mla_w_uq,
             mla_kv_norm_g=mla_kv_norm_g, mla_w_uk=mla_w_uk, mla_w_uv=mla_w_uv, mla_qn_g=mla_qn_g,
             mla_kn_g=mla_kn_g, mla_qr_g=mla_qr_g, mla_kr_g=mla_kr_g, diff_qn_g=diff_qn_g,
             diff_kn_g=diff_kn_g, diff_lambda=diff_lambda, diff_subln_g=diff_subln_g,
             w_br_mla=w_br_mla, w_br_sb=w_br_sb, w_br_diff=w_br_diff, w_out=w_out,
             ffn_norm_g=ffn_norm_g, ffn_w_up=ffn_w_up, ffn_conv_w=ffn_conv_w,
             ffn_conv_b=ffn_conv_b, ffn_w_down=ffn_w_down)
    depth = w_in.shape[0]
    bp, sp, d = x_prompt.shape
    bs, ss, _ = x_sample.shape
    past_len = cache_mla_ckv.shape[2]
    sb_heads, sb_dim = cache_sb_k.shape[3], cache_sb_k.shape[4]
    df_heads, df_dim = cache_diff_k.shape[3], cache_diff_k.shape[5]
    assert sb_dim == HEAD and df_dim == HEAD and ffn_conv_w.shape[1] == 3
    nope, rope = mla_qn_g.shape[1], mla_qr_g.shape[1]
    dims = dict(d_model=d, q_rank=mla_q_norm_g.shape[1], kv_rank=mla_kv_norm_g.shape[1],
                nope=nope, rope=rope, mla_heads=mla_w_uk.shape[2] // nope,
                sb_w=sb_heads * sb_dim, df_w=2 * df_heads * df_dim)
    sb_w, df_w = dims["sb_w"], dims["df_w"]

    tabs_p = _rope_tables(jnp.arange(sp, dtype=jnp.int32), rope, max(sp, ROW_TILE))
    tab_s, _ = _rope_tables(past_len + jnp.arange(ss, dtype=jnp.int32), rope, max(ss, ROW_TILE))
    slopes = 2.0 ** (-8.0 * jnp.arange(1, df_heads + 1, dtype=F32) / df_heads) * LOG2E

    feat = lambda c: jnp.moveaxis(c, 2, -1).reshape(depth, bs, -1, past_len)
    assert 2 * df_dim == LANES
    caches = (cache_mla_ckv, feat(cache_mla_krope), feat(cache_sb_k), feat(cache_sb_v),
              feat(cache_diff_k), cache_diff_v.reshape(depth, bs, past_len * df_heads, 2 * df_dim))

    xp = x_prompt.reshape(bp * sp, d)
    xs = x_sample.reshape(bs * ss, d)
    bufs, rows_s, conv_p, conv_s = None, [], [], []
    for l in range(depth):
        lw = _pack_layer(w, l, dims)
        xp, xs, bufs, rs, cp, cs = _layer(xp, xs, lw, dims, bp=bp, sp=sp, bs=bs, ss=ss, tabs_p=tabs_p,
                                          tab_s=tab_s, layer=l, depth=depth, bufs=bufs, caches=caches,
                                          conv_state=state_ffn_conv, slopes=slopes)
        rows_s.append(rs)
        conv_p.append(cp)
        conv_s.append(cs)

    p_ckv, p_krt, p_skt, p_svt, p_dkt, p_dvi = bufs
    tok = lambda t, tail: jnp.moveaxis(t.reshape((depth, bp) + tail + (sp,)), -1, 2)
    prompt_rows = (p_ckv, tok(p_krt, (rope,)), tok(p_skt, (sb_heads, sb_dim)),
                   tok(p_svt, (sb_heads, sb_dim)), tok(p_dkt, (df_heads, 2, df_dim)),
                   p_dvi.reshape(depth, bp, sp, df_heads, 2 * df_dim))

    def stack(i, tail):
        return jnp.stack([r[i] for r in rows_s]).reshape((depth, bs, ss) + tail)

    sample_rows = (stack(0, (dims["kv_rank"],)), stack(1, (rope,)), stack(2, (sb_heads, sb_dim)),
                   stack(3, (sb_heads, sb_dim)), stack(4, (df_heads, 2, df_dim)),
                   stack(5, (df_heads, 2 * df_dim)))
    return ((xp.reshape(bp, sp, d), xs.reshape(bs, ss, d)) + prompt_rows + (jnp.stack(conv_p),)
            + sample_rows + (jnp.stack(conv_s),))
```

```python
import functools
import math

import numpy as np
import jax
import jax.numpy as jnp
from jax import lax
from jax.experimental import pallas as pl
from jax.experimental.pallas import tpu as pltpu

F32 = jnp.float32
BF16 = jnp.bfloat16

EPS = 1e-6
CHUNK = 64
CHUNK_SHIFT = CHUNK.bit_length() - 1
assert 1 << CHUNK_SHIFT == CHUNK
ROPE_BASE = 10000.0
NEG_BIG = -1e30
HEAD = 64

LANES = 128
ROW_TILE = 512
KV_ROW_TILE = 1024
ATTN_TILE = 512
SB_BLOCK = 256
SB_CACHE_TILE = 1024
FF_CHUNK = 512
SB_SKIP_LOG2 = -160.0
LOG2E = math.log2(math.e)
VMEM_LIMIT = 52 * 1024 * 1024


def _cparams(n_axes):
    return pltpu.CompilerParams(dimension_semantics=("arbitrary",) * n_axes,
                                vmem_limit_bytes=VMEM_LIMIT)


def _const_spec(shape):
    nd = len(shape)
    return pl.BlockSpec(shape, lambda *_: (0,) * nd, pipeline_mode=pl.Buffered(1))


def _dot(a, b):
    return jnp.dot(a, b, preferred_element_type=F32)


def _dot_nt(a, b):
    return lax.dot_general(a, b, (((1,), (1,)), ((), ())), preferred_element_type=F32)


def _rmsnorm(x, g):
    return x * lax.rsqrt(jnp.mean(x * x, axis=-1, keepdims=True) + EPS) * g


def _group_mean_sq(x, gmat):
    sq = x * x
    hi = sq.astype(BF16)
    lo = (sq - hi.astype(F32)).astype(BF16)
    return _dot(hi, gmat) + _dot(lo, gmat)


def _row_group_rmsnorm(xt, gcol):
    r, n = xt.shape
    x3 = xt.reshape(r // HEAD, HEAD, n)
    y3 = x3 * lax.rsqrt(jnp.mean(x3 * x3, axis=1, keepdims=True) + EPS)
    return y3.reshape(r, n) * gcol


def _sigmoid(x):
    return 1.0 / (1.0 + jnp.exp(-x))


def _mla_query(cq, gcq_ref, wuq_ref, gm_mla_ref, gq_ref, cos_q, sin_q, q_ref, half):
    tm = cq.shape[0]
    cqn = _rmsnorm(cq, gcq_ref[...]).astype(BF16)
    q = _dot(cqn, wuq_ref[...])
    lane = lax.broadcasted_iota(jnp.int32, (tm, LANES), 1)
    for p in range(q.shape[1] // (2 * LANES)):
        lo = 2 * LANES * p
        blk = q[:, lo:lo + 2 * LANES]
        y = blk * lax.rsqrt(_group_mean_sq(blk, gm_mla_ref[...]) + EPS) * gq_ref[:, lo:lo + 2 * LANES]
        rp = y[:, LANES:]
        partner = jnp.where(lane < 2 * half, pltpu.roll(rp, LANES - 2 * half, 1),
                            pltpu.roll(rp, 2 * half, 1))
        q_ref[:, lo:lo + LANES] = y[:, :LANES].astype(BF16)
        q_ref[:, lo + LANES:lo + 2 * LANES] = (rp * cos_q + partner * sin_q).astype(BF16)


def _diff_query(zq, gm64_ref, gdq_ref, dq_ref):
    for s in range(zq.shape[1] // (2 * LANES)):
        lo = 2 * LANES * s
        blk = zq[:, lo:lo + 2 * LANES]
        r = lax.rsqrt(_group_mean_sq(blk, gm64_ref[...]) + EPS)
        dq_ref[:, lo:lo + 2 * LANES] = (blk * r * gdq_ref[:, lo:lo + 2 * LANES]).astype(BF16)


def _in_proj_kernel(x_ref, gmix_ref, w_ref, gcq_ref, wuq_ref, gm_mla_ref, gq_ref, gckv_ref,
                    gkr_ref, tab_ref, gm64_ref, gdq_ref, gdk_ref,
                    q_ref, ckv_ref, kr_ref, sq_ref, sk_ref, sv_ref, dq_ref, dk_ref, dv_ref,
                    *, q_rank, kv_rank, rope, sb_w, df_w):
    tm = x_ref.shape[0]
    half = rope // 2
    h = _rmsnorm(x_ref[...], gmix_ref[...]).astype(BF16)
    tab = tab_ref[...]
    cos_k, sin_k = tab[:, 2 * LANES:3 * LANES], tab[:, 3 * LANES:4 * LANES]
    lane = lax.broadcasted_iota(jnp.int32, (tm, LANES), 1)

    za = _dot(h, w_ref[:, 0:4 * LANES])
    _mla_query(za[:, 0:q_rank], gcq_ref, wuq_ref, gm_mla_ref, gq_ref, tab[:, 0:LANES],
               tab[:, LANES:2 * LANES], q_ref, half)
    ckv_ref[...] = _rmsnorm(za[:, q_rank:q_rank + kv_rank], gckv_ref[...])

    kr = za[:, q_rank + kv_rank:4 * LANES]
    ms = jnp.sum(kr * kr, axis=-1, keepdims=True) * (1.0 / rope)
    krn = kr * lax.rsqrt(ms + EPS) * gkr_ref[...]
    partner = jnp.where(lane < half, pltpu.roll(krn, LANES - half, 1), pltpu.roll(krn, half, 1))
    kr_ref[...] = (krn * cos_k + partner * sin_k)[:, :rope]

    o = 4 * LANES
    zs = _dot(h, w_ref[:, o:o + 3 * sb_w])
    sq_ref[...] = (zs[:, 0:sb_w] * (LOG2E / 8.0)).astype(BF16)
    sk_ref[...] = zs[:, sb_w:2 * sb_w]
    sv_ref[...] = zs[:, 2 * sb_w:3 * sb_w]

    o = o + 3 * sb_w
    zd = _dot(h, w_ref[:, o:o + 3 * df_w])
    _diff_query(zd[:, 0:df_w], gm64_ref, gdq_ref, dq_ref)
    for s in range(df_w // (2 * LANES)):
        lo = 2 * LANES * s
        blk = zd[:, df_w + lo:df_w + lo + 2 * LANES]
        r = lax.rsqrt(_group_mean_sq(blk, gm64_ref[...]) + EPS)
        dk_ref[:, lo:lo + 2 * LANES] = blk * r * gdk_ref[:, lo:lo + 2 * LANES]
    dv_ref[...] = zd[:, 2 * df_w:3 * df_w]


def _in_proj(x2d, lw, tab, dims):
    t, d = x2d.shape
    tm = ROW_TILE
    assert t % tm == 0 and tab.shape[0] % tm == 0
    n_tab = tab.shape[0] // tm
    row = lambda w: pl.BlockSpec((tm, w), lambda i: (i, 0))
    consts = [lw["gmix"], lw["w1"], lw["gcq"], lw["wuq"], lw["gm_mla"], lw["gq"], lw["gckv"], lw["gkr"]]
    consts2 = [lw["gm64"], lw["gdq"], lw["gdk"]]
    sb_w, df_w = dims["sb_w"], dims["df_w"]
    out_shape = [
        jax.ShapeDtypeStruct((t, lw["wuq"].shape[1]), BF16),
        jax.ShapeDtypeStruct((t, dims["kv_rank"]), F32),
        jax.ShapeDtypeStruct((t, dims["rope"]), F32),
        jax.ShapeDtypeStruct((t, sb_w), BF16),
        jax.ShapeDtypeStruct((t, sb_w), F32),
        jax.ShapeDtypeStruct((t, sb_w), F32),
        jax.ShapeDtypeStruct((t, df_w), BF16),
        jax.ShapeDtypeStruct((t, df_w), F32),
        jax.ShapeDtypeStruct((t, df_w), F32),
    ]
    kern = functools.partial(_in_proj_kernel, q_rank=dims["q_rank"], kv_rank=dims["kv_rank"],
                             rope=dims["rope"], sb_w=sb_w, df_w=df_w)
    return pl.pallas_call(
        kern,
        grid=(t // tm,),
        in_specs=([row(d)] + [_const_spec(c.shape) for c in consts]
                  + [pl.BlockSpec((tm, tab.shape[1]), lambda i: (i % n_tab, 0))]
                  + [_const_spec(c.shape) for c in consts2]),
        out_specs=[row(s.shape[1]) for s in out_shape],
        out_shape=out_shape,
        compiler_params=_cparams(1),
        name="in_proj",
    )(x2d, *consts, tab, *consts2)


def _in_proj_t_kernel(*refs, q_rank, kv_rank, rope, sb_w, df_w, n_alias):
    (x_ref, gmix_ref, wn_ref, wt_ref, gcq_ref, wuq_ref, gm_mla_ref, gq_ref, gckv_ref, gkr_ref,
     tab_ref, tabt_ref, gm64_ref, gdq_ref, gdk_ref) = refs[:15]
    (q_ref, sq_ref, dq_ref, ckv_ref, krt_ref, skt_ref, svt_ref, dkt_ref, dvi_ref) = refs[15 + n_alias:]
    half = rope // 2
    tm = x_ref.shape[0]
    h = _rmsnorm(x_ref[...], gmix_ref[...]).astype(BF16)
    tab = tab_ref[...]

    o = q_rank + kv_rank
    za = _dot(h, wn_ref[:, 0:o])
    _mla_query(za[:, 0:q_rank], gcq_ref, wuq_ref, gm_mla_ref, gq_ref, tab[:, 0:LANES],
               tab[:, LANES:2 * LANES], q_ref, half)
    ckv_ref[0, 0] = _rmsnorm(za[:, q_rank:o], gckv_ref[...])

    sq_ref[...] = (_dot(h, wn_ref[:, o:o + sb_w]) * (LOG2E / 8.0)).astype(BF16)
    o = o + sb_w
    zd = _dot(h, wn_ref[:, o:o + 2 * df_w])
    _diff_query(zd[:, 0:df_w], gm64_ref, gdq_ref, dq_ref)
    n_dv = df_w // LANES
    for hh in range(n_dv):
        dvi_ref[0, 0, pl.ds(hh, tm, stride=n_dv), :] = zd[:, df_w + LANES * hh:df_w + LANES * (hh + 1)]

    skt_ref[0, 0] = _dot_nt(wt_ref[0:sb_w, :], h)
    svt_ref[0, 0] = _dot_nt(wt_ref[sb_w:2 * sb_w, :], h)
    o = 2 * sb_w
    dkt_ref[0, 0] = _row_group_rmsnorm(_dot_nt(wt_ref[o:o + df_w, :], h), gdk_ref[...])
    o = o + df_w
    krt = _dot_nt(wt_ref[o:o + rope, :], h)
    krn = krt * lax.rsqrt(jnp.mean(krt * krt, axis=0, keepdims=True) + EPS) * gkr_ref[...]
    x1, x2 = krn[:half], krn[half:]
    cos_t, sin_t = tabt_ref[0:half, :], tabt_ref[half:rope, :]
    krt_ref[0, 0] = jnp.concatenate([x1 * cos_t - x2 * sin_t, x1 * sin_t + x2 * cos_t], axis=0)


def _in_proj_t(x2d, lw, tab, tabt, dims, *, batch, seq, layer, depth, prev):
    t, d = x2d.shape
    tm = ROW_TILE
    assert seq % tm == 0 and t == batch * seq
    nst = seq // tm
    row = lambda w: pl.BlockSpec((tm, w), lambda i: (i, 0))
    consts = [lw["gmix"], lw["wn"], lw["wt"], lw["gcq"], lw["wuq"], lw["gm_mla"], lw["gq"], lw["gckv"],
              lw["gkr_col"]]
    consts2 = [lw["gm64"], lw["gdq"], lw["gdk_col"]]
    sb_w, df_w, rope, kvr = dims["sb_w"], dims["df_w"], dims["rope"], dims["kv_rank"]
    n_alias = 0 if prev is None else len(prev)
    in_specs = ([row(d)] + [_const_spec(c.shape) for c in consts]
                + [pl.BlockSpec((tm, tab.shape[1]), lambda i: (i % nst, 0)),
                   pl.BlockSpec((rope, tm), lambda i: (0, i % nst))]
                + [_const_spec(c.shape) for c in consts2]
                + [pl.BlockSpec(memory_space=pl.ANY)] * n_alias)
    n_plain = len(in_specs) - n_alias
    tok = lambda w: pl.BlockSpec((1, 1, tm, w), lambda i: (layer, i // nst, i % nst, 0))
    feat = lambda r: pl.BlockSpec((1, 1, r, tm), lambda i: (layer, i // nst, 0, i % nst))
    n_dv = df_w // LANES
    out_shape = [
        jax.ShapeDtypeStruct((t, lw["wuq"].shape[1]), BF16),
        jax.ShapeDtypeStruct((t, sb_w), BF16),
        jax.ShapeDtypeStruct((t, df_w), BF16),
        jax.ShapeDtypeStruct((depth, batch, seq, kvr), F32),
        jax.ShapeDtypeStruct((depth, batch, rope, seq), F32),
        jax.ShapeDtypeStruct((depth, batch, sb_w, seq), F32),
        jax.ShapeDtypeStruct((depth, batch, sb_w, seq), F32),
        jax.ShapeDtypeStruct((depth, batch, df_w, seq), F32),
        jax.ShapeDtypeStruct((depth, batch, seq * n_dv, LANES), F32),
    ]
    out_specs = [row(out_shape[0].shape[1]), row(sb_w), row(df_w),
                 tok(kvr), feat(rope), feat(sb_w), feat(sb_w), feat(df_w),
                 pl.BlockSpec((1, 1, tm * n_dv, LANES), lambda i: (layer, i // nst, i % nst, 0))]
    kern = functools.partial(_in_proj_t_kernel, q_rank=dims["q_rank"], kv_rank=kvr, rope=rope,
                             sb_w=sb_w, df_w=df_w, n_alias=n_alias)
    return pl.pallas_call(
        kern,
        grid=(t // tm,),
        in_specs=in_specs,
        out_specs=out_specs,
        out_shape=out_shape,
        input_output_aliases={n_plain + j: 3 + j for j in range(n_alias)},
        compiler_params=_cparams(1),
        name="in_proj_t",
    )(x2d, *consts, tab, tabt, *consts2, *(prev or ()))


def _kv_up_kernel(ckv_ref, kr_ref, wukv_ref, gm64_ref, gkn_ref, rep_ref, k_ref, v_ref, *, kn_w):
    c = ckv_ref[...].astype(BF16)
    kv = _dot(c, wukv_ref[...])
    krrep = _dot(kr_ref[...].astype(BF16), rep_ref[...]).astype(BF16)
    for b in range(kn_w // (2 * LANES)):
        s = 2 * LANES * b
        blk = kv[:, s:s + 2 * LANES]
        r = lax.rsqrt(_group_mean_sq(blk, gm64_ref[...]) + EPS)
        kn = (blk * r * gkn_ref[:, s:s + 2 * LANES]).astype(BF16)
        for j in range(2):
            p = 2 * b + j
            k_ref[:, 2 * LANES * p:2 * LANES * p + LANES] = kn[:, LANES * j:LANES * (j + 1)]
            k_ref[:, 2 * LANES * p + LANES:2 * LANES * (p + 1)] = krrep
    v_ref[...] = kv[:, kn_w:].astype(BF16)


def _kv_up(ckv2d, kr2d, lw):
    r, kvr = ckv2d.shape
    tr = min(KV_ROW_TILE, r)
    assert r % tr == 0
    kn_w = lw["gkn"].shape[1]
    consts = [lw["wukv"], lw["gm64"], lw["gkn"], lw["rep"]]
    out_shape = [jax.ShapeDtypeStruct((r, 2 * kn_w), BF16),
                 jax.ShapeDtypeStruct((r, lw["wukv"].shape[1] - kn_w), BF16)]
    return pl.pallas_call(
        functools.partial(_kv_up_kernel, kn_w=kn_w),
        grid=(r // tr,),
        in_specs=[pl.BlockSpec((tr, kvr), lambda i: (i, 0)),
                  pl.BlockSpec((tr, kr2d.shape[1]), lambda i: (i, 0))]
                 + [_const_spec(c.shape) for c in consts],
        out_specs=[pl.BlockSpec((tr, s.shape[1]), lambda i: (i, 0)) for s in out_shape],
        out_shape=out_shape,
        compiler_params=_cparams(1),
        name="mla_kv_up",
    )(ckv2d, kr2d, *consts)


def _mla_kv_feature_major(c, kr, wukt, wuvt, gkn, kt_ref, vt_ref):
    half = kr.shape[0] // 2
    n = c.shape[0]
    knt = _row_group_rmsnorm(_dot_nt(wukt, c), gkn).astype(BF16)
    x1, x2 = kr[:half], kr[half:]
    krrep = jnp.concatenate([x1, x1, x2, x2, jnp.zeros((LANES - 4 * half, n), BF16)], axis=0)
    for p in range(knt.shape[0] // LANES):
        kt_ref[2 * LANES * p:2 * LANES * p + LANES, :] = knt[LANES * p:LANES * (p + 1)]
        kt_ref[2 * LANES * p + LANES:2 * LANES * (p + 1), :] = krrep
    vt_ref[...] = _dot_nt(wuvt, c).astype(BF16)


def _stream_lane_masks(kind, ns, width):
    lane = lax.broadcasted_iota(jnp.int32, (1, width), 1)
    masks = []
    for i in range(ns):
        if kind == "mla":
            r = lane - 2 * LANES * (i // 2)
            j = i % 2
            masks.append(((r >= HEAD * j) & (r < HEAD * (j + 1)))
                         | ((r >= 128 + 16 * j) & (r < 128 + 16 * (j + 1)))
                         | ((r >= 160 + 16 * j) & (r < 160 + 16 * (j + 1))))
        else:
            masks.append((lane >= HEAD * i) & (lane < HEAD * (i + 1)))
    return masks


def _attn_kernel(*refs, kind, srcs, ns, tq, q_pos0, lam_init, ids=None):
    it = iter(refs)
    slopes_ref = next(it) if kind == "diff" else None
    q_ref = next(it)
    kv_refs = [tuple(next(it) for _ in range(5 if s["latent"] else 2)) for s in srcs]
    if kind == "diff":
        lam_ref, gsub_ref = next(it), next(it)
    o_ref = next(it)
    scr = [(next(it), next(it)) if s["cast"] else None for s in srcs]

    grp, qi = ids if ids is not None else (pl.program_id(1), pl.program_id(2))
    rows = ns * tq
    vw = HEAD * ns

    for s, sc, in_refs in zip(srcs, scr, kv_refs):
        if s["latent"]:
            @pl.when(qi == 0)
            def _(s=s, sc=sc, in_refs=in_refs):
                ckv_ref, krt_ref, wukt_ref, wuvt_ref, gkn_ref = in_refs
                _mla_kv_feature_major(ckv_ref[0, 0].astype(BF16), krt_ref[0, 0].astype(BF16), wukt_ref[...],
                                      wuvt_ref[...], gkn_ref[...], sc[0], sc[1])
        elif sc is not None:
            k_ref, v_ref = in_refs

            @pl.when(qi == 0)
            def _(s=s, sc=sc, k_ref=k_ref, v_ref=v_ref):
                sc[0][...] = k_ref[0, 0].astype(BF16)
                if s["vil"]:
                    for j in range(vw // LANES):
                        rows_j = pl.ds(grp * (vw // LANES) + j, s["sk"], stride=s["vil"])
                        sc[1][:, LANES * j:LANES * (j + 1)] = v_ref[0, 0, rows_j, :].astype(BF16)
                else:
                    sc[1][...] = v_ref[0, 0].astype(BF16)

    q = q_ref[0]
    zero = jnp.zeros_like(q)
    qs = jnp.concatenate([jnp.where(m, q, zero) for m in _stream_lane_masks(kind, ns, q.shape[1])],
                         axis=0)
    q_start = q_pos0 + qi * tq
    qpos = jnp.concatenate([q_start + lax.broadcasted_iota(jnp.int32, (tq, 1), 0)] * ns, axis=0)
    if kind == "diff":
        slope = jnp.concatenate(
            [jnp.full((tq, 1), slopes_ref[grp * (ns // 2) + i // 2], F32) for i in range(ns)], axis=0)

    if kind == "sb":
        init = (jnp.zeros((rows, 1), F32), jnp.zeros((rows, vw), F32))
    else:
        init = (jnp.full((rows, 1), NEG_BIG, F32), jnp.zeros((rows, 1), F32),
                jnp.zeros((rows, vw), F32))

    def tile_step(st, k_t, v_t, k_start, src, masked, upper):
        tk = src["tk"]
        kpos = k_start + lax.broadcasted_iota(jnp.int32, (1, tk), 1)
        s = _dot(qs, k_t) if src["kt"] else _dot_nt(qs, k_t)
        pv = (lambda p: _dot_nt(p, v_t)) if src["vt"] else (lambda p: _dot(p, v_t))
        if masked:
            valid = (kpos < qpos) if kind == "sb" else ((kpos >> CHUNK_SHIFT) <= (qpos >> CHUNK_SHIFT))
        if kind == "sb":
            c, acc = st
            lp = jnp.log(1.0 + jnp.exp2(-jnp.abs(s))) * LOG2E
            ls = jnp.minimum(s, 0.0) - lp
            l1 = ls - s
            if masked:
                l1 = jnp.where(valid, l1, 0.0)
            hi = l1.astype(BF16)
            lo = (l1 - hi.astype(F32)).astype(BF16)
            cb = upper.shape[0]
            nb = tk // cb
            if nb > 1:
                hi = jnp.concatenate([hi[:, cb * j:cb * (j + 1)] for j in range(nb)], axis=0)
                lo = jnp.concatenate([lo[:, cb * j:cb * (j + 1)] for j in range(nb)], axis=0)
            suffix = _dot(hi, upper) + _dot(lo, upper)
            parts = [None] * nb
            for j in reversed(range(nb)):
                parts[j] = jnp.exp2(ls[:, cb * j:cb * (j + 1)] + suffix[rows * j:rows * (j + 1)] + c)
                c = c + jnp.sum(l1[:, cb * j:cb * (j + 1)], axis=-1, keepdims=True)
            a = parts[0] if nb == 1 else jnp.concatenate(parts, axis=1)
            if masked:
                a = jnp.where(valid, a, 0.0)
            return (c, acc + pv(a.astype(BF16)))
        m, l, acc = st
        if kind == "diff":
            dist = qpos - kpos
            s = s - slope * (jnp.abs(dist) if masked else dist).astype(F32)
        if masked:
            s = jnp.where(valid, s, -jnp.inf)
        m_new = jnp.maximum(m, jnp.max(s, axis=-1, keepdims=True))
        alpha = jnp.exp2(m - m_new)
        p = jnp.exp2(s - m_new)
        l = alpha * l + jnp.sum(p, axis=-1, keepdims=True)
        return (m_new, l, alpha * acc + pv(p.astype(BF16)))

    def upper_tri(tk):
        if kind != "sb":
            return None
        cb = min(tk, SB_BLOCK)
        r = lax.broadcasted_iota(jnp.int32, (cb, cb), 0)
        c = lax.broadcasted_iota(jnp.int32, (cb, cb), 1)
        return jnp.where(r > c, 1.0, 0.0).astype(BF16)

    st = init
    for s, sc, in_refs in reversed(list(zip(srcs, scr, kv_refs))):
        k_ref, v_ref = (None, None) if s["latent"] else in_refs
        tk = s["tk"]
        upper = upper_tri(tk)

        def load(k_off, src, _sc=sc, _k=k_ref, _v=v_ref):
            def one(ref4, ref2, transposed):
                win = pl.ds(k_off, src["tk"])
                if ref2 is not None:
                    return ref2[:, win] if transposed else ref2[win, :]
                return ref4[0, 0, :, win] if transposed else ref4[0, 0, win, :]
            return (one(_k, _sc and _sc[0], src["kt"]), one(_v, _sc and _sc[1], src["vt"]))

        if s["mode"] == "self":
            k_t, v_t = load(pl.multiple_of(qi * tk, tk), s)
            st = tile_step(st, k_t, v_t, s["pos0"] + qi * tk, s, True, upper)
            if kind == "sb" and tk > SB_BLOCK:
                s = dict(s, tk=SB_BLOCK)
            n_full = qi * (tk // s["tk"])
        elif s["mode"] == "masked":
            k_t, v_t = load(0, s)
            st = tile_step(st, k_t, v_t, s["pos0"], s, True, upper)
            continue
        else:
            n_full = s["sk"] // tk

        def body(i, st_, _s=s, _load=load, _n=n_full, _upper=upper):
            kj = _n - 1 - i
            k_t, v_t = _load(pl.multiple_of(kj * _s["tk"], _s["tk"]), _s)
            return tile_step(st_, k_t, v_t, _s["pos0"] + kj * _s["tk"], _s, False, _upper)

        if isinstance(n_full, int) and n_full <= 2:
            for i in range(n_full):
                st = body(i, st)
        elif kind == "sb":
            def live(st_):
                return (jnp.max(st_[0]) > SB_SKIP_LOG2).astype(jnp.int32)

            def wbody(carry, _body=body):
                i, _, st_ = carry
                st_ = _body(i, st_)
                return i + 1, live(st_), st_

            _, _, st = lax.while_loop(lambda carry, _n=n_full: (carry[0] < _n) & (carry[1] > 0),
                                      wbody, (jnp.int32(0), live(st), st))
        else:
            st = lax.fori_loop(0, n_full, body, st)

    acc = st[-1]
    if kind == "diff":
        l = st[1]
        lv = lam_ref[...]
        lam = (jnp.exp(jnp.sum(lv[0:1] * lv[1:2], axis=-1, keepdims=True))
               - jnp.exp(jnp.sum(lv[2:3] * lv[3:4], axis=-1, keepdims=True)) + lam_init)
        for hh in range(ns // 2):
            lanes = slice(2 * HEAD * hh, 2 * HEAD * (hh + 1))
            r0, r1 = slice(tq * 2 * hh, tq * (2 * hh + 1)), slice(tq * (2 * hh + 1), tq * (2 * hh + 2))
            o = acc[r0, lanes] / l[r0] - lam * (acc[r1, lanes] / l[r1])
            o_ref[0, :, lanes] = (_rmsnorm(o, gsub_ref[...]) * (1.0 - lam_init)).astype(BF16)
    else:
        lane = lax.broadcasted_iota(jnp.int32, (tq, vw), 1)
        o = jnp.zeros((tq, vw), F32)
        for i in range(ns):
            blk = acc[tq * i:tq * (i + 1)]
            if kind == "mla":
                blk = blk / st[1][tq * i:tq * (i + 1)]
            o = jnp.where((lane >= HEAD * i) & (lane < HEAD * (i + 1)), blk, o)
        o_ref[0] = o.astype(BF16)


def _attention_plan(kind, q3, sources, *, ns, q_pos0, extras=(), lam_init=0.0):
    b, sq, qtot = q3.shape
    w = (2 * HEAD if kind == "mla" else HEAD) * ns
    vw = HEAD * ns
    n_groups = qtot // w
    tq = min(ATTN_TILE, sq)
    assert sq % tq == 0 and qtot % w == 0
    srcs, in_specs, args, scratch = [], [], [], []
    if kind == "diff":
        in_specs.append(pl.BlockSpec(memory_space=pltpu.SMEM))
        args.append(extras[0])
    in_specs.append(pl.BlockSpec((1, tq, w), lambda bi, g, qi: (bi, qi, g)))
    args.append(q3)

    def kv_spec(arr, width, transposed, layer):
        if transposed:
            assert arr.shape[2] == n_groups * width
            return pl.BlockSpec((1, 1, width, arr.shape[3]), lambda bi, g, qi: (layer, bi, g, 0))
        assert arr.shape[3] == n_groups * width
        return pl.BlockSpec((1, 1, arr.shape[2], width), lambda bi, g, qi: (layer, bi, 0, g))

    for s in sources:
        lat = s.get("latent")
        if lat is not None:
            assert kind == "mla" and s["mode"] in ("full", "self")
            ckv, krt, layer = lat["ckv"], lat["krt"], s["layer"]
            sk = ckv.shape[2]
            tk = tq if s["mode"] == "self" else s["tk"]
            assert sk % tk == 0
            srcs.append(dict(mode=s["mode"], pos0=s["pos0"], sk=sk, tk=tk, cast=True, kt=True, vt=True,
                             vil=0, latent=True))
            in_specs += [pl.BlockSpec((1, 1, sk, ckv.shape[3]), lambda bi, g, qi, _l=layer: (_l, bi, 0, 0)),
                         pl.BlockSpec((1, 1, krt.shape[2], sk), lambda bi, g, qi, _l=layer: (_l, bi, 0, 0)),
                         pl.BlockSpec((w // 2, ckv.shape[3]), lambda bi, g, qi: (g, 0)),
                         pl.BlockSpec((vw, ckv.shape[3]), lambda bi, g, qi: (g, 0)),
                         pl.BlockSpec((w // 2, 1), lambda bi, g, qi: (g, 0))]
            args += [ckv, krt, lat["wukt"], lat["wuvt"], lat["gkn"]]
            scratch += [pltpu.VMEM((w, sk), BF16), pltpu.VMEM((vw, sk), BF16)]
            continue
        k, v, kt, vt = s["k"], s["v"], s["kt"], s["vt"]
        sk = k.shape[3] if kt else k.shape[2]
        tk = tq if s["mode"] == "self" else (sk if s["mode"] == "masked" else s.get("tk", min(ATTN_TILE, sk)))
        assert sk % tk == 0 and (tk <= SB_BLOCK or tk % SB_BLOCK == 0)
        cast = k.dtype != BF16
        assert cast == (v.dtype != BF16)
        vil = s.get("vil", 0)
        vlayer = s.get("vlayer", s["layer"])
        srcs.append(dict(mode=s["mode"], pos0=s["pos0"], sk=sk, tk=tk, cast=cast, kt=kt, vt=vt, vil=vil,
                         latent=False))
        if vil:
            assert cast and not vt and v.shape[2:] == (sk * vil, LANES) and vw % LANES == 0
            v_spec = pl.BlockSpec((1, 1, sk * vil, LANES), lambda bi, g, qi, _l=vlayer: (_l, bi, 0, 0))
        else:
            v_spec = kv_spec(v, vw, vt, vlayer)
        in_specs += [kv_spec(k, w, kt, s["layer"]), v_spec]
        args += [k, v]
        if cast:
            scratch += [pltpu.VMEM((w, sk) if kt else (sk, w), BF16),
                        pltpu.VMEM((vw, sk) if vt else (sk, vw), BF16)]
    if kind == "diff":
        in_specs += [_const_spec(extras[1].shape), _const_spec(extras[2].shape)]
        args += [extras[1], extras[2]]
    kern = functools.partial(_attn_kernel, kind=kind, srcs=srcs, ns=ns, tq=tq, q_pos0=q_pos0,
                             lam_init=lam_init)
    return dict(kern=kern, grid=(b, n_groups, sq // tq), in_specs=in_specs, args=args,
                out_spec=pl.BlockSpec((1, tq, vw), lambda bi, g, qi: (bi, qi, g)),
                out_shape=jax.ShapeDtypeStruct((b, sq, n_groups * vw), BF16),
                scratch=scratch, name="attn_" + kind)


def _remap(spec, ids):
    if spec.index_map is None:
        return spec
    return pl.BlockSpec(spec.block_shape, lambda *g, _m=spec.index_map: _m(*ids(*g)),
                        pipeline_mode=spec.pipeline_mode, memory_space=spec.memory_space)


def _attention_fused(pp, ps):
    bp, gp, nq = pp["grid"]
    bs, gs, one = ps["grid"]
    assert one == 1 and bp * gp == bs * gs and nq >= 2
    n_pi, n_si, n_ps = len(pp["in_specs"]), len(ps["in_specs"]), len(pp["scratch"])

    def p_ids(bi, g, qi):
        return bi, g, jnp.maximum(qi - 1, 0)

    def s_ids(bi, g, qi, ahead=True):
        idx = bi * gp + g
        if ahead:
            idx = jnp.minimum(idx + jnp.where(qi >= nq, 1, 0), bs * gs - 1)
        return idx // gs, idx % gs, 0

    def kern(*refs):
        ins_p, ins_s = refs[:n_pi], refs[n_pi:n_pi + n_si]
        out_p, out_s = refs[n_pi + n_si:n_pi + n_si + 2]
        scr = refs[n_pi + n_si + 2:]
        bi, g, qi = pl.program_id(0), pl.program_id(1), pl.program_id(2)

        @pl.when(qi == 0)
        def _():
            ps["kern"](*ins_s, out_s, *scr[n_ps:], ids=((bi * gp + g) % gs, jnp.int32(0)))

        @pl.when(qi > 0)
        def _():
            pp["kern"](*ins_p, out_p, *scr[:n_ps], ids=(g, qi - 1))

    return pl.pallas_call(
        kern,
        grid=(bp, gp, nq + 1),
        in_specs=[_remap(s, p_ids) for s in pp["in_specs"]] + [_remap(s, s_ids) for s in ps["in_specs"]],
        out_specs=[_remap(pp["out_spec"], p_ids),
                   _remap(ps["out_spec"], functools.partial(s_ids, ahead=False))],
        out_shape=[pp["out_shape"], ps["out_shape"]],
        scratch_shapes=pp["scratch"] + ps["scratch"],
        compiler_params=_cparams(3),
        name=pp["name"] + "_fused",
    )(*pp["args"], *ps["args"])


def _merge_kernel(x_ref, gmix_ref, wg_ref, om_ref, os_ref, od_ref, wbr_ref, wout_ref, o_ref):
    x = x_ref[...]
    d = x.shape[1]
    h = _rmsnorm(x, gmix_ref[...]).astype(BF16)
    merged = None
    for i, br_ref in enumerate((om_ref, os_ref, od_ref)):
        gate = _sigmoid(_dot(h, wg_ref[:, i * d:(i + 1) * d]))
        term = gate * _dot(br_ref[...], wbr_ref[i])
        merged = term if merged is None else merged + term
    o_ref[...] = x + _dot(merged.astype(BF16), wout_ref[...])


def _merge(x2d, o_mla, o_sb, o_diff, lw):
    t, d = x2d.shape
    tm = ROW_TILE
    row = lambda w: pl.BlockSpec((tm, w), lambda i: (i, 0))
    return pl.pallas_call(
        _merge_kernel,
        grid=(t // tm,),
        in_specs=[row(d), _const_spec(lw["gmix"].shape), _const_spec(lw["wg"].shape),
                  row(o_mla.shape[1]), row(o_sb.shape[1]), row(o_diff.shape[1]),
                  _const_spec(lw["wbr"].shape), _const_spec(lw["wout"].shape)],
        out_specs=row(d),
        out_shape=jax.ShapeDtypeStruct((t, d), F32),
        compiler_params=_cparams(1),
        name="merge",
    )(x2d, lw["gmix"], lw["wg"], o_mla, o_sb, o_diff, lw["wbr"], lw["wout"])


def _conv_rows(a, h0, h1, cw, cb):
    n = a.shape[0]
    row = lax.broadcasted_iota(jnp.int32, (n, 1), 0)
    p1 = jnp.where(row == 0, h1, pltpu.roll(a, 1, 0))
    p2 = jnp.where(row == 0, h0, jnp.where(row == 1, h1, pltpu.roll(a, 2, 0)))
    return p2 * cw[0:1] + p1 * cw[1:2] + a * cw[2:3] + cb


def _ffn_kernel(*refs, seg, d_ff, has_hist):
    it = iter(refs)
    x_ref, g_ref, wup_ref, cw_ref, cb_ref = next(it), next(it), next(it), next(it), next(it)
    hist_ref = next(it) if has_hist else None
    wdn_ref, o_ref, conv_ref = next(it), next(it), next(it)
    carry_ref = next(it) if not has_hist else None
    act_ref = next(it) if has_hist else None

    x = x_ref[...]
    tm = x.shape[0]
    h = _rmsnorm(x, g_ref[...]).astype(BF16)
    cw, cb = cw_ref[...], cb_ref[...]

    if not has_hist:
        i = pl.program_id(0)
        tiles_per_seq = seg // tm

        @pl.when((i % tiles_per_seq) == 0)
        def _():
            carry_ref[...] = jnp.zeros_like(carry_ref)

        tail = carry_ref[...]
        y, tails = x, []
        chunks = [(c0, min(FF_CHUNK, d_ff - c0)) for c0 in range(0, d_ff, FF_CHUNK)]
        up = lambda c0, cs: (_dot(h, wup_ref[:, c0:c0 + cs]), _dot(h, wup_ref[:, d_ff + c0:d_ff + c0 + cs]))
        nxt = up(*chunks[0])
        for idx, (c0, cs) in enumerate(chunks):
            a, u = nxt
            if idx + 1 < len(chunks):
                nxt = up(*chunks[idx + 1])
            c = _conv_rows(a, tail[6:7, c0:c0 + cs], tail[7:8, c0:c0 + cs], cw[:, c0:c0 + cs],
                           cb[:, c0:c0 + cs])
            tails.append(a[tm - 8:tm])
            y = y + _dot((c * _sigmoid(c) * u).astype(BF16), wdn_ref[c0:c0 + cs, :])
        new_tail = jnp.concatenate(tails, axis=1)
        carry_ref[...] = new_tail

        @pl.when((i % tiles_per_seq) == tiles_per_seq - 1)
        def _():
            conv_ref[0] = new_tail[6:8]

        o_ref[...] = y
    else:
        a = _dot(h, wup_ref[:, 0:d_ff])
        u = _dot(h, wup_ref[:, d_ff:2 * d_ff])
        for s in range(tm // seg):
            a_s = a[s * seg:(s + 1) * seg]
            hs = hist_ref[0, s]
            c = _conv_rows(a_s, hs[0:1], hs[1:2], cw, cb)
            act_ref[s * seg:(s + 1) * seg, :] = (c * _sigmoid(c) * u[s * seg:(s + 1) * seg]).astype(BF16)
            conv_ref[s] = a_s[seg - 2:seg]
        o_ref[...] = x + _dot(act_ref[...], wdn_ref[...])


def _ffn(x2d, lw, *, seg, hist=None, layer=0):
    t, d = x2d.shape
    tm = ROW_TILE
    d_ff = lw["wdn"].shape[0]
    has_hist = hist is not None
    row = lambda w: pl.BlockSpec((tm, w), lambda i: (i, 0))
    in_specs = [row(d), _const_spec(lw["gffn"].shape), _const_spec(lw["wup"].shape),
                _const_spec(lw["cw"].shape), _const_spec(lw["cb"].shape)]
    args = [x2d, lw["gffn"], lw["wup"], lw["cw"], lw["cb"]]
    if has_hist:
        assert tm % seg == 0 and seg % 8 == 0
        nseq = tm // seg
        in_specs.append(pl.BlockSpec((1, nseq, 2, d_ff), lambda i: (layer, i, 0, 0)))
        args.append(hist)
        conv_spec = pl.BlockSpec((nseq, 2, d_ff), lambda i: (i, 0, 0))
        scratch = [pltpu.VMEM((tm, d_ff), BF16)]
    else:
        assert seg % tm == 0
        tps = seg // tm
        conv_spec = pl.BlockSpec((1, 2, d_ff), lambda i: (i // tps, 0, 0))
        scratch = [pltpu.VMEM((8, d_ff), F32)]
    in_specs.append(_const_spec(lw["wdn"].shape))
    args.append(lw["wdn"])
    return pl.pallas_call(
        functools.partial(_ffn_kernel, seg=seg, d_ff=d_ff, has_hist=has_hist),
        grid=(t // tm,),
        in_specs=in_specs,
        out_specs=[row(d), conv_spec],
        out_shape=[jax.ShapeDtypeStruct((t, d), F32),
                   jax.ShapeDtypeStruct((t // seg, 2, d_ff), F32)],
        scratch_shapes=scratch,
        compiler_params=_cparams(1),
        name="ffn",
    )(*args)


def _group_matrix(groups, width):
    m = np.zeros((width, width), np.float32)
    for g in groups:
        for i in g:
            m[i, g] = 1.0 / len(g)
    return jnp.asarray(m, dtype=BF16)


def _pack_layer(w, l, dims):
    nope, rope, heads = dims["nope"], dims["rope"], dims["mla_heads"]
    q_rank, kv_rank = dims["q_rank"], dims["kv_rank"]
    sb_w, df_w = dims["sb_w"], dims["df_w"]
    half = rope // 2
    assert nope == HEAD and rope == 32 and heads % 2 == 0 and q_rank + kv_rank + rope <= 4 * LANES
    assert (q_rank + kv_rank) % LANES == 0 and sb_w % (2 * LANES) == 0 and df_w % (2 * LANES) == 0
    w_in = w["w_in"][l]
    o_kr = q_rank + kv_rank
    o_sb = o_kr + rope
    o_df = o_sb + 3 * sb_w
    o_g = o_df + 3 * df_w
    col = lambda a, b: w_in[:, a:b]
    wa = jnp.pad(col(0, o_sb), ((0, 0), (0, 4 * LANES - o_sb)))
    w1 = jnp.concatenate([wa, col(o_sb, o_g)], axis=1).astype(BF16)
    wn = jnp.concatenate([col(0, o_kr), col(o_sb, o_sb + sb_w), col(o_df, o_df + df_w),
                          col(o_df + 2 * df_w, o_g)], axis=1).astype(BF16)
    wt = jnp.concatenate([col(o_sb + sb_w, o_df), col(o_df + df_w, o_df + 2 * df_w),
                          col(o_kr, o_sb)], axis=1).T.astype(BF16)

    cols = np.full((heads // 2) * 2 * LANES, -1, np.int64)
    gain_src = np.full(cols.shape, -1, np.int64)
    for p in range(heads // 2):
        base = 2 * LANES * p
        for j, hd in enumerate((2 * p, 2 * p + 1)):
            hb = hd * (nope + rope)
            cols[base + nope * j:base + nope * (j + 1)] = hb + np.arange(nope)
            gain_src[base + nope * j:base + nope * (j + 1)] = np.arange(nope)
            x1 = base + 2 * nope + half * j
            x2 = base + 2 * nope + 2 * half + half * j
            cols[x1:x1 + half] = hb + nope + np.arange(half)
            cols[x2:x2 + half] = hb + nope + half + np.arange(half)
            gain_src[x1:x1 + half] = nope + np.arange(half)
            gain_src[x2:x2 + half] = nope + half + np.arange(half)
    valid = jnp.asarray(cols >= 0)
    wuq = jnp.where(valid[None, :], w["mla_w_uq"][l][:, np.maximum(cols, 0)], 0.0).astype(BF16)
    qg = jnp.concatenate([w["mla_qn_g"][l], w["mla_qr_g"][l]])
    gq = jnp.where(valid, qg[np.maximum(gain_src, 0)], 0.0) * ((nope + rope) ** -0.5 * LOG2E)

    groups = [list(range(0, nope)), list(range(nope, 2 * nope))]
    for j in range(2):
        groups.append(list(range(2 * nope + half * j, 2 * nope + half * (j + 1)))
                      + list(range(2 * nope + 2 * half + half * j, 2 * nope + 2 * half + half * (j + 1))))
    gm_mla = _group_matrix(groups, 2 * LANES)
    gm64 = _group_matrix([list(range(HEAD * j, HEAD * (j + 1))) for j in range(4)], 2 * LANES)

    rep = np.zeros((rope, LANES), np.float32)
    for j in range(half):
        rep[j, [j, half + j]] = 1.0
        rep[half + j, [2 * half + j, 3 * half + j]] = 1.0

    row = lambda v: v.reshape(1, -1).astype(F32)
    colv = lambda v: v.reshape(-1, 1).astype(F32)
    gdk = jnp.tile(w["diff_kn_g"][l], df_w // HEAD)
    gkn = jnp.tile(w["mla_kn_g"][l], heads)
    return dict(
        gmix=row(w["mix_norm_g"][l]), w1=w1, wn=wn, wt=wt, gcq=row(w["mla_q_norm_g"][l]), wuq=wuq,
        gm_mla=gm_mla, gq=row(gq), gckv=row(w["mla_kv_norm_g"][l]),
        gkr=row(jnp.pad(w["mla_kr_g"][l], (0, LANES - rope))), gkr_col=colv(w["mla_kr_g"][l]),
        gm64=gm64,
        gdq=row(jnp.tile(w["diff_qn_g"][l], df_w // HEAD) * (LOG2E / 8.0)),
        gdk=row(gdk), gdk_col=colv(gdk),
        wukv=jnp.concatenate([w["mla_w_uk"][l], w["mla_w_uv"][l]], axis=1).astype(BF16),
        wukt=w["mla_w_uk"][l].T.astype(BF16), wuvt=w["mla_w_uv"][l].T.astype(BF16),
        gkn=row(gkn), gkn_col=colv(gkn),
        rep=jnp.asarray(rep, dtype=BF16),
        lam=w["diff_lambda"][l].astype(F32), gsub=row(w["diff_subln_g"][l]),
        wg=w_in[:, o_g:].astype(BF16),
        wbr=jnp.stack([w["w_br_mla"][l], w["w_br_sb"][l], w["w_br_diff"][l]]).astype(BF16),
        wout=w["w_out"][l].astype(BF16),
        gffn=row(w["ffn_norm_g"][l]), wup=w["ffn_w_up"][l].astype(BF16),
        cw=w["ffn_conv_w"][l].astype(F32), cb=row(w["ffn_conv_b"][l]),
        wdn=w["ffn_w_down"][l].astype(BF16),
    )


def _rope_tables(pos, rope, n_rows):
    half = rope // 2
    inv = ROPE_BASE ** (-jnp.arange(half, dtype=F32) / half)
    ang = pos.astype(F32)[:, None] * inv[None, :]
    cos, sin = jnp.cos(ang), jnp.sin(ang)
    z = lambda n: jnp.zeros((pos.shape[0], n), F32)
    tab = jnp.concatenate([
        cos, cos, cos, cos, z(LANES - 4 * half),
        -sin, -sin, sin, sin, z(LANES - 4 * half),
        cos, cos, z(LANES - 2 * half),
        -sin, sin, z(LANES - 2 * half)], axis=1)
    return jnp.tile(tab, (n_rows // pos.shape[0], 1)), jnp.concatenate([cos.T, sin.T], axis=0)


def _tail(x2d, o_mla, o_sb, o_df, lw, *, seq, hist=None, layer=0):
    f2 = lambda a: a.reshape(x2d.shape[0], a.shape[-1])
    x1 = _merge(x2d, f2(o_mla), f2(o_sb), f2(o_df), lw)
    return _ffn(x1, lw, seg=seq, hist=hist, layer=layer)


def _layer(xp, xs, lw, dims, *, bp, sp, bs, ss, tabs_p, tab_s, layer, depth, bufs, caches, conv_state,
           slopes):
    q, sq, dq, *bufs = _in_proj_t(xp, lw, tabs_p[0], tabs_p[1], dims, batch=bp, seq=sp,
                                  layer=layer, depth=depth, prev=bufs)
    ckv, krt, skt, svt, dkt, dvi = bufs
    qs_, ckv_s, kr_s, sqs, sks, svs, dqs, dks, dvs = _in_proj(xs, lw, tab_s, dims)
    kpk, mv = _kv_up(ckv_s, kr_s, lw)
    c_ckv, c_krt, c_skt, c_svt, c_dkt, c_dv = caches
    plen = c_ckv.shape[2]
    n_dv = dims["df_w"] // LANES
    p3 = lambda a: a.reshape(bp, sp, a.shape[-1])
    s3 = lambda a: a.reshape(bs, ss, a.shape[-1])
    s4 = lambda a: a.reshape(1, bs, ss, a.shape[-1])
    lam_init = 0.8 - 0.6 * math.exp(-0.3 * layer)
    extras = (slopes, lw["lam"], lw["gsub"])

    def own(k, v, **kw):
        return [dict(k=k, v=v, layer=layer, kt=True, pos0=0, mode="self", **kw)]

    def past(ck, cv, nk, nv, tk=plen, **kw):
        return [dict(k=ck, v=cv, layer=layer, kt=True, pos0=0, mode="full", tk=tk, **kw),
                dict(k=s4(nk), v=s4(nv), layer=0, kt=False, vt=False, pos0=plen, mode="masked")]

    lat = lambda c, kr: dict(ckv=c, krt=kr, wukt=lw["wukt"], wuvt=lw["wuvt"], gkn=lw["gkn_col"])
    o_mla, o_mla_s = _attention_fused(
        _attention_plan("mla", p3(q), own(None, None, latent=lat(ckv, krt)), ns=2, q_pos0=0),
        _attention_plan("mla", s3(qs_), past(None, None, kpk, mv, latent=lat(c_ckv, c_krt)), ns=4,
                        q_pos0=plen))
    o_sb, o_sb_s = _attention_fused(
        _attention_plan("sb", p3(sq), own(skt, svt, vt=True), ns=2, q_pos0=0),
        _attention_plan("sb", s3(sqs), past(c_skt, c_svt, sks, svs, tk=min(SB_CACHE_TILE, plen), vt=True),
                        ns=4, q_pos0=plen))
    o_df, o_df_s = _attention_fused(
        _attention_plan("diff", p3(dq), own(dkt, dvi, vt=False, vil=n_dv), ns=2, q_pos0=0,
                        extras=extras, lam_init=lam_init),
        _attention_plan("diff", s3(dqs), past(c_dkt, c_dv, dks, dvs, vt=False, vil=n_dv), ns=4,
                        q_pos0=plen, extras=extras, lam_init=lam_init))
    xp2, conv_p = _tail(xp, o_mla, o_sb, o_df, lw, seq=sp)
    xs2, conv_s = _tail(xs, o_mla_s, o_sb_s, o_df_s, lw, seq=ss, hist=conv_state, layer=layer)
    return xp2, xs2, tuple(bufs), (ckv_s, kr_s, sks, svs, dks, dvs), conv_p, conv_s


def kernel(x_prompt, x_sample, cache_mla_ckv, cache_mla_krope, cache_sb_k, cache_sb_v, cache_diff_k, cache_diff_v, state_ffn_conv, mix_norm_g, w_in, mla_q_norm_g, mla_w_uq, mla_kv_norm_g, mla_w_uk, mla_w_uv, mla_qn_g, mla_kn_g, mla_qr_g, mla_kr_g, diff_qn_g, diff_kn_g, diff_lambda, diff_subln_g, w_br_mla, w_br_sb, w_br_diff, w_out, ffn_norm_g, ffn_w_up, ffn_conv_w, ffn_conv_b, ffn_w_down):
    w = dict(mix_norm_g=mix_norm_g, w_in=w_in, mla_q_norm_g=mla_q_norm_g, mla_w_uq=mla_w_uq,
             mla_kv_norm_g=mla_kv_norm_g, mla_w_uk=mla_w_uk, mla_w_uv=mla_w_uv, mla_qn_g=mla_qn_g,
             mla_kn_g=mla_kn_g, mla_qr_g=mla_qr_g, mla_kr_g=mla_kr_g, diff_qn_g=diff_qn_g,
             diff_kn_g=diff_kn_g, diff_lambda=diff_lambda, diff_subln_g=diff_subln_g,
             w_br_mla=w_br_mla, w_br_sb=w_br_sb, w_br_diff=w_br_diff, w_out=w_out,
             ffn_norm_g=ffn_norm_g, ffn_w_up=ffn_w_up, ffn_conv_w=ffn_conv_w,
             ffn_conv_b=ffn_conv_b, ffn_w_down=ffn_w_down)
    depth = w_in.shape[0]
    bp, sp, d = x_prompt.shape
    bs, ss, _ = x_sample.shape
    past_len = cache_mla_ckv.shape[2]
    sb_heads, sb_dim = cache_sb_k.shape[3], cache_sb_k.shape[4]
    df_heads, df_dim = cache_diff_k.shape[3], cache_diff_k.shape[5]
    assert sb_dim == HEAD and df_dim == HEAD and ffn_conv_w.shape[1] == 3
    nope, rope = mla_qn_g.shape[1], mla_qr_g.shape[1]
    dims = dict(d_model=d, q_rank=mla_q_norm_g.shape[1], kv_rank=mla_kv_norm_g.shape[1],
                nope=nope, rope=rope, mla_heads=mla_w_uk.shape[2] // nope,
                sb_w=sb_heads * sb_dim, df_w=2 * df_heads * df_dim)
    sb_w, df_w = dims["sb_w"], dims["df_w"]

    tabs_p = _rope_tables(jnp.arange(sp, dtype=jnp.int32), rope, max(sp, ROW_TILE))
    tab_s, _ = _rope_tables(past_len + jnp.arange(ss, dtype=jnp.int32), rope, max(ss, ROW_TILE))
    slopes = 2.0 ** (-8.0 * jnp.arange(1, df_heads + 1, dtype=F32) / df_heads) * LOG2E

    feat = lambda c: jnp.moveaxis(c, 2, -1).reshape(depth, bs, -1, past_len)
    assert 2 * df_dim == LANES
    caches = (cache_mla_ckv, feat(cache_mla_krope), feat(cache_sb_k), feat(cache_sb_v),
              feat(cache_diff_k), cache_diff_v.reshape(depth, bs, past_len * df_heads, 2 * df_dim))

    xp = x_prompt.reshape(bp * sp, d)
    xs = x_sample.reshape(bs * ss, d)
    bufs, rows_s, conv_p, conv_s = None, [], [], []
    for l in range(depth):
        lw = _pack_layer(w, l, dims)
        xp, xs, bufs, rs, cp, cs = _layer(xp, xs, lw, dims, bp=bp, sp=sp, bs=bs, ss=ss, tabs_p=tabs_p,
                                          tab_s=tab_s, layer=l, depth=depth, bufs=bufs, caches=caches,
                                          conv_state=state_ffn_conv, slopes=slopes)
        rows_s.append(rs)
        conv_p.append(cp)
        conv_s.append(cs)

    p_ckv, p_krt, p_skt, p_svt, p_dkt, p_dvi = bufs
    tok = lambda t, tail: jnp.moveaxis(t.reshape((depth, bp) + tail + (sp,)), -1, 2)
    prompt_rows = (p_ckv, tok(p_krt, (rope,)), tok(p_skt, (sb_heads, sb_dim)),
                   tok(p_svt, (sb_heads, sb_dim)), tok(p_dkt, (df_heads, 2, df_dim)),
                   p_dvi.reshape(depth, bp, sp, df_heads, 2 * df_dim))

    def stack(i, tail):
        return jnp.stack([r[i] for r in rows_s]).reshape((depth, bs, ss) + tail)

    sample_rows = (stack(0, (dims["kv_rank"],)), stack(1, (rope,)), stack(2, (sb_heads, sb_dim)),
                   stack(3, (sb_heads, sb_dim)), stack(4, (df_heads, 2, df_dim)),
                   stack(5, (df_heads, 2 * df_dim)))
    return ((xp.reshape(bp, sp, d), xs.reshape(bs, ss, d)) + prompt_rows + (jnp.stack(conv_p),)
            + sample_rows + (jnp.stack(conv_s),))
```

```python
import functools
import math

import numpy as np
import jax
import jax.numpy as jnp
from jax import lax
from jax.experimental import pallas as pl
from jax.experimental.pallas import tpu as pltpu

F32 = jnp.float32
BF16 = jnp.bfloat16

EPS = 1e-6
CHUNK = 64
CHUNK_SHIFT = CHUNK.bit_length() - 1
assert 1 << CHUNK_SHIFT == CHUNK
ROPE_BASE = 10000.0
NEG_BIG = -1e30
HEAD = 64

LANES = 128
ROW_TILE = 512
KV_ROW_TILE = 1024
ATTN_TILE = 512
SB_BLOCK = 256
SB_CACHE_TILE = 1024
FF_CHUNK = 512
SB_SKIP_LOG2 = -160.0
LOG2E = math.log2(math.e)
VMEM_LIMIT = 52 * 1024 * 1024


def _cparams(n_axes):
    return pltpu.CompilerParams(dimension_semantics=("arbitrary",) * n_axes,
                                vmem_limit_bytes=VMEM_LIMIT)


def _const_spec(shape):
    nd = len(shape)
    return pl.BlockSpec(shape, lambda *_: (0,) * nd, pipeline_mode=pl.Buffered(1))


def _dot(a, b):
    return jnp.dot(a, b, preferred_element_type=F32)


def _dot_nt(a, b):
    return lax.dot_general(a, b, (((1,), (1,)), ((), ())), preferred_element_type=F32)


def _rmsnorm(x, g):
    return x * lax.rsqrt(jnp.mean(x * x, axis=-1, keepdims=True) + EPS) * g


def _group_mean_sq(x, gmat):
    sq = x * x
    hi = sq.astype(BF16)
    lo = (sq - hi.astype(F32)).astype(BF16)
    return _dot(hi, gmat) + _dot(lo, gmat)


def _row_group_rmsnorm(xt, gcol):
    r, n = xt.shape
    x3 = xt.reshape(r // HEAD, HEAD, n)
    y3 = x3 * lax.rsqrt(jnp.mean(x3 * x3, axis=1, keepdims=True) + EPS)
    return y3.reshape(r, n) * gcol


def _sigmoid(x):
    return 1.0 / (1.0 + jnp.exp(-x))


def _mla_query(cq, gcq_ref, wuq_ref, gm_mla_ref, gq_ref, cos_q, sin_q, q_ref, half):
    tm = cq.shape[0]
    cqn = _rmsnorm(cq, gcq_ref[...]).astype(BF16)
    q = _dot(cqn, wuq_ref[...])
    lane = lax.broadcasted_iota(jnp.int32, (tm, LANES), 1)
    for p in range(q.shape[1] // (2 * LANES)):
        lo = 2 * LANES * p
        blk = q[:, lo:lo + 2 * LANES]
        y = blk * lax.rsqrt(_group_mean_sq(blk, gm_mla_ref[...]) + EPS) * gq_ref[:, lo:lo + 2 * LANES]
        rp = y[:, LANES:]
        partner = jnp.where(lane < 2 * half, pltpu.roll(rp, LANES - 2 * half, 1),
                            pltpu.roll(rp, 2 * half, 1))
        q_ref[:, lo:lo + LANES] = y[:, :LANES].astype(BF16)
        q_ref[:, lo + LANES:lo + 2 * LANES] = (rp * cos_q + partner * sin_q).astype(BF16)


def _diff_query(zq, gm64_ref, gdq_ref, dq_ref):
    for s in range(zq.shape[1] // (2 * LANES)):
        lo = 2 * LANES * s
        blk = zq[:, lo:lo + 2 * LANES]
        r = lax.rsqrt(_group_mean_sq(blk, gm64_ref[...]) + EPS)
        dq_ref[:, lo:lo + 2 * LANES] = (blk * r * gdq_ref[:, lo:lo + 2 * LANES]).astype(BF16)


def _in_proj_kernel(x_ref, gmix_ref, w_ref, gcq_ref, wuq_ref, gm_mla_ref, gq_ref, gckv_ref,
                    gkr_ref, tab_ref, gm64_ref, gdq_ref, gdk_ref,
                    q_ref, ckv_ref, kr_ref, sq_ref, sk_ref, sv_ref, dq_ref, dk_ref, dv_ref,
                    *, q_rank, kv_rank, rope, sb_w, df_w):
    tm = x_ref.shape[0]
    half = rope // 2
    h = _rmsnorm(x_ref[...], gmix_ref[...]).astype(BF16)
    tab = tab_ref[...]
    cos_k, sin_k = tab[:, 2 * LANES:3 * LANES], tab[:, 3 * LANES:4 * LANES]
    lane = lax.broadcasted_iota(jnp.int32, (tm, LANES), 1)

    za = _dot(h, w_ref[:, 0:4 * LANES])
    _mla_query(za[:, 0:q_rank], gcq_ref, wuq_ref, gm_mla_ref, gq_ref, tab[:, 0:LANES],
               tab[:, LANES:2 * LANES], q_ref, half)
    ckv_ref[...] = _rmsnorm(za[:, q_rank:q_rank + kv_rank], gckv_ref[...])

    kr = za[:, q_rank + kv_rank:4 * LANES]
    ms = jnp.sum(kr * kr, axis=-1, keepdims=True) * (1.0 / rope)
    krn = kr * lax.rsqrt(ms + EPS) * gkr_ref[...]
    partner = jnp.where(lane < half, pltpu.roll(krn, LANES - half, 1), pltpu.roll(krn, half, 1))
    kr_ref[...] = (krn * cos_k + partner * sin_k)[:, :rope]

    o = 4 * LANES
    zs = _dot(h, w_ref[:, o:o + 3 * sb_w])
    sq_ref[...] = (zs[:, 0:sb_w] * (LOG2E / 8.0)).astype(BF16)
    sk_ref[...] = zs[:, sb_w:2 * sb_w]
    sv_ref[...] = zs[:, 2 * sb_w:3 * sb_w]

    o = o + 3 * sb_w
    zd = _dot(h, w_ref[:, o:o + 3 * df_w])
    _diff_query(zd[:, 0:df_w], gm64_ref, gdq_ref, dq_ref)
    for s in range(df_w // (2 * LANES)):
        lo = 2 * LANES * s
        blk = zd[:, df_w + lo:df_w + lo + 2 * LANES]
        r = lax.rsqrt(_group_mean_sq(blk, gm64_ref[...]) + EPS)
        dk_ref[:, lo:lo + 2 * LANES] = blk * r * gdk_ref[:, lo:lo + 2 * LANES]
    dv_ref[...] = zd[:, 2 * df_w:3 * df_w]


def _in_proj(x2d, lw, tab, dims):
    t, d = x2d.shape
    tm = ROW_TILE
    assert t % tm == 0 and tab.shape[0] % tm == 0
    n_tab = tab.shape[0] // tm
    row = lambda w: pl.BlockSpec((tm, w), lambda i: (i, 0))
    consts = [lw["gmix"], lw["w1"], lw["gcq"], lw["wuq"], lw["gm_mla"], lw["gq"], lw["gckv"], lw["gkr"]]
    consts2 = [lw["gm64"], lw["gdq"], lw["gdk"]]
    sb_w, df_w = dims["sb_w"], dims["df_w"]
    out_shape = [
        jax.ShapeDtypeStruct((t, lw["wuq"].shape[1]), BF16),
        jax.ShapeDtypeStruct((t, dims["kv_rank"]), F32),
        jax.ShapeDtypeStruct((t, dims["rope"]), F32),
        jax.ShapeDtypeStruct((t, sb_w), BF16),
        jax.ShapeDtypeStruct((t, sb_w), F32),
        jax.ShapeDtypeStruct((t, sb_w), F32),
        jax.ShapeDtypeStruct((t, df_w), BF16),
        jax.ShapeDtypeStruct((t, df_w), F32),
        jax.ShapeDtypeStruct((t, df_w), F32),
    ]
    kern = functools.partial(_in_proj_kernel, q_rank=dims["q_rank"], kv_rank=dims["kv_rank"],
                             rope=dims["rope"], sb_w=sb_w, df_w=df_w)
    return pl.pallas_call(
        kern,
        grid=(t // tm,),
        in_specs=([row(d)] + [_const_spec(c.shape) for c in consts]
                  + [pl.BlockSpec((tm, tab.shape[1]), lambda i: (i % n_tab, 0))]
                  + [_const_spec(c.shape) for c in consts2]),
        out_specs=[row(s.shape[1]) for s in out_shape],
        out_shape=out_shape,
        compiler_params=_cparams(1),
        name="in_proj",
    )(x2d, *consts, tab, *consts2)


def _in_proj_t_kernel(*refs, q_rank, kv_rank, rope, sb_w, df_w, n_alias):
    (x_ref, gmix_ref, wn_ref, wt_ref, gcq_ref, wuq_ref, gm_mla_ref, gq_ref, gckv_ref, gkr_ref,
     tab_ref, tabt_ref, gm64_ref, gdq_ref, gdk_ref) = refs[:15]
    (q_ref, sq_ref, dq_ref, ckv_ref, krt_ref, skt_ref, svt_ref, dkt_ref, dvi_ref) = refs[15 + n_alias:]
    half = rope // 2
    tm = x_ref.shape[0]
    h = _rmsnorm(x_ref[...], gmix_ref[...]).astype(BF16)
    tab = tab_ref[...]

    o = q_rank + kv_rank
    za = _dot(h, wn_ref[:, 0:o])
    _mla_query(za[:, 0:q_rank], gcq_ref, wuq_ref, gm_mla_ref, gq_ref, tab[:, 0:LANES],
               tab[:, LANES:2 * LANES], q_ref, half)
    ckv_ref[0, 0] = _rmsnorm(za[:, q_rank:o], gckv_ref[...])

    sq_ref[...] = (_dot(h, wn_ref[:, o:o + sb_w]) * (LOG2E / 8.0)).astype(BF16)
    o = o + sb_w
    zd = _dot(h, wn_ref[:, o:o + 2 * df_w])
    _diff_query(zd[:, 0:df_w], gm64_ref, gdq_ref, dq_ref)
    n_dv = df_w // LANES
    for hh in range(n_dv):
        dvi_ref[0, 0, pl.ds(hh, tm, stride=n_dv), :] = zd[:, df_w + LANES * hh:df_w + LANES * (hh + 1)]

    skt_ref[0, 0] = _dot_nt(wt_ref[0:sb_w, :], h)
    svt_ref[0, 0] = _dot_nt(wt_ref[sb_w:2 * sb_w, :], h)
    o = 2 * sb_w
    dkt_ref[0, 0] = _row_group_rmsnorm(_dot_nt(wt_ref[o:o + df_w, :], h), gdk_ref[...])
    o = o + df_w
    krt = _dot_nt(wt_ref[o:o + rope, :], h)
    krn = krt * lax.rsqrt(jnp.mean(krt * krt, axis=0, keepdims=True) + EPS) * gkr_ref[...]
    x1, x2 = krn[:half], krn[half:]
    cos_t, sin_t = tabt_ref[0:half, :], tabt_ref[half:rope, :]
    krt_ref[0, 0] = jnp.concatenate([x1 * cos_t - x2 * sin_t, x1 * sin_t + x2 * cos_t], axis=0)


def _in_proj_t(x2d, lw, tab, tabt, dims, *, batch, seq, layer, depth, prev):
    t, d = x2d.shape
    tm = ROW_TILE
    assert seq % tm == 0 and t == batch * seq
    nst = seq // tm
    row = lambda w: pl.BlockSpec((tm, w), lambda i: (i, 0))
    consts = [lw["gmix"], lw["wn"], lw["wt"], lw["gcq"], lw["wuq"], lw["gm_mla"], lw["gq"], lw["gckv"],
              lw["gkr_col"]]
    consts2 = [lw["gm64"], lw["gdq"], lw["gdk_col"]]
    sb_w, df_w, rope, kvr = dims["sb_w"], dims["df_w"], dims["rope"], dims["kv_rank"]
    n_alias = 0 if prev is None else len(prev)
    in_specs = ([row(d)] + [_const_spec(c.shape) for c in consts]
                + [pl.BlockSpec((tm, tab.shape[1]), lambda i: (i % nst, 0)),
                   pl.BlockSpec((rope, tm), lambda i: (0, i % nst))]
                + [_const_spec(c.shape) for c in consts2]
                + [pl.BlockSpec(memory_space=pl.ANY)] * n_alias)
    n_plain = len(in_specs) - n_alias
    tok = lambda w: pl.BlockSpec((1, 1, tm, w), lambda i: (layer, i // nst, i % nst, 0))
    feat = lambda r: pl.BlockSpec((1, 1, r, tm), lambda i: (layer, i // nst, 0, i % nst))
    n_dv = df_w // LANES
    out_shape = [
        jax.ShapeDtypeStruct((t, lw["wuq"].shape[1]), BF16),
        jax.ShapeDtypeStruct((t, sb_w), BF16),
        jax.ShapeDtypeStruct((t, df_w), BF16),
        jax.ShapeDtypeStruct((depth, batch, seq, kvr), F32),
        jax.ShapeDtypeStruct((depth, batch, rope, seq), F32),
        jax.ShapeDtypeStruct((depth, batch, sb_w, seq), F32),
        jax.ShapeDtypeStruct((depth, batch, sb_w, seq), F32),
        jax.ShapeDtypeStruct((depth, batch, df_w, seq), F32),
        jax.ShapeDtypeStruct((depth, batch, seq * n_dv, LANES), F32),
    ]
    out_specs = [row(out_shape[0].shape[1]), row(sb_w), row(df_w),
                 tok(kvr), feat(rope), feat(sb_w), feat(sb_w), feat(df_w),
                 pl.BlockSpec((1, 1, tm * n_dv, LANES), lambda i: (layer, i // nst, i % nst, 0))]
    kern = functools.partial(_in_proj_t_kernel, q_rank=dims["q_rank"], kv_rank=kvr, rope=rope,
                             sb_w=sb_w, df_w=df_w, n_alias=n_alias)
    return pl.pallas_call(
        kern,
        grid=(t // tm,),
        in_specs=in_specs,
        out_specs=out_specs,
        out_shape=out_shape,
        input_output_aliases={n_plain + j: 3 + j for j in range(n_alias)},
        compiler_params=_cparams(1),
        name="in_proj_t",
    )(x2d, *consts, tab, tabt, *consts2, *(prev or ()))


def _kv_up_kernel(ckv_ref, kr_ref, wukv_ref, gm64_ref, gkn_ref, rep_ref, k_ref, v_ref, *, kn_w):
    c = ckv_ref[...].astype(BF16)
    kv = _dot(c, wukv_ref[...])
    krrep = _dot(kr_ref[...].astype(BF16), rep_ref[...]).astype(BF16)
    for b in range(kn_w // (2 * LANES)):
        s = 2 * LANES * b
        blk = kv[:, s:s + 2 * LANES]
        r = lax.rsqrt(_group_mean_sq(blk, gm64_ref[...]) + EPS)
        kn = (blk * r * gkn_ref[:, s:s + 2 * LANES]).astype(BF16)
        for j in range(2):
            p = 2 * b + j
            k_ref[:, 2 * LANES * p:2 * LANES * p + LANES] = kn[:, LANES * j:LANES * (j + 1)]
            k_ref[:, 2 * LANES * p + LANES:2 * LANES * (p + 1)] = krrep
    v_ref[...] = kv[:, kn_w:].astype(BF16)


def _kv_up(ckv2d, kr2d, lw):
    r, kvr = ckv2d.shape
    tr = min(KV_ROW_TILE, r)
    assert r % tr == 0
    kn_w = lw["gkn"].shape[1]
    consts = [lw["wukv"], lw["gm64"], lw["gkn"], lw["rep"]]
    out_shape = [jax.ShapeDtypeStruct((r, 2 * kn_w), BF16),
                 jax.ShapeDtypeStruct((r, lw["wukv"].shape[1] - kn_w), BF16)]
    return pl.pallas_call(
        functools.partial(_kv_up_kernel, kn_w=kn_w),
        grid=(r // tr,),
        in_specs=[pl.BlockSpec((tr, kvr), lambda i: (i, 0)),
                  pl.BlockSpec((tr, kr2d.shape[1]), lambda i: (i, 0))]
                 + [_const_spec(c.shape) for c in consts],
        out_specs=[pl.BlockSpec((tr, s.shape[1]), lambda i: (i, 0)) for s in out_shape],
        out_shape=out_shape,
        compiler_params=_cparams(1),
        name="mla_kv_up",
    )(ckv2d, kr2d, *consts)


def _mla_kv_feature_major(c, kr, wukt, wuvt, gkn, kt_ref, vt_ref):
    half = kr.shape[0] // 2
    n = c.shape[0]
    knt = _row_group_rmsnorm(_dot_nt(wukt, c), gkn).astype(BF16)
    x1, x2 = kr[:half], kr[half:]
    krrep = jnp.concatenate([x1, x1, x2, x2, jnp.zeros((LANES - 4 * half, n), BF16)], axis=0)
    for p in range(knt.shape[0] // LANES):
        kt_ref[2 * LANES * p:2 * LANES * p + LANES, :] = knt[LANES * p:LANES * (p + 1)]
        kt_ref[2 * LANES * p + LANES:2 * LANES * (p + 1), :] = krrep
    vt_ref[...] = _dot_nt(wuvt, c).astype(BF16)


def _stream_lane_masks(kind, ns, width):
    lane = lax.broadcasted_iota(jnp.int32, (1, width), 1)
    masks = []
    for i in range(ns):
        if kind == "mla":
            r = lane - 2 * LANES * (i // 2)
            j = i % 2
            masks.append(((r >= HEAD * j) & (r < HEAD * (j + 1)))
                         | ((r >= 128 + 16 * j) & (r < 128 + 16 * (j + 1)))
                         | ((r >= 160 + 16 * j) & (r < 160 + 16 * (j + 1))))
        else:
            masks.append((lane >= HEAD * i) & (lane < HEAD * (i + 1)))
    return masks


def _attn_kernel(*refs, kind, srcs, ns, tq, q_pos0, lam_init):
    it = iter(refs)
    slopes_ref = next(it) if kind == "diff" else None
    q_ref = next(it)
    kv_refs = [tuple(next(it) for _ in range(5 if s["latent"] else 2)) for s in srcs]
    if kind == "diff":
        lam_ref, gsub_ref = next(it), next(it)
    o_ref = next(it)
    scr = [(next(it), next(it)) if s["cast"] else None for s in srcs]

    grp = pl.program_id(1)
    qi = pl.program_id(2)
    rows = ns * tq
    vw = HEAD * ns

    for s, sc, in_refs in zip(srcs, scr, kv_refs):
        if s["latent"]:
            @pl.when(qi == 0)
            def _(s=s, sc=sc, in_refs=in_refs):
                ckv_ref, krt_ref, wukt_ref, wuvt_ref, gkn_ref = in_refs
                _mla_kv_feature_major(ckv_ref[0, 0].astype(BF16), krt_ref[0, 0].astype(BF16), wukt_ref[...],
                                      wuvt_ref[...], gkn_ref[...], sc[0], sc[1])
        elif sc is not None:
            k_ref, v_ref = in_refs

            @pl.when(qi == 0)
            def _(s=s, sc=sc, k_ref=k_ref, v_ref=v_ref):
                sc[0][...] = k_ref[0, 0].astype(BF16)
                if s["vil"]:
                    for j in range(vw // LANES):
                        rows_j = pl.ds(grp * (vw // LANES) + j, s["sk"], stride=s["vil"])
                        sc[1][:, LANES * j:LANES * (j + 1)] = v_ref[0, 0, rows_j, :].astype(BF16)
                else:
                    sc[1][...] = v_ref[0, 0].astype(BF16)

    q = q_ref[0]
    zero = jnp.zeros_like(q)
    qs = jnp.concatenate([jnp.where(m, q, zero) for m in _stream_lane_masks(kind, ns, q.shape[1])],
                         axis=0)
    q_start = q_pos0 + qi * tq
    qpos = jnp.concatenate([q_start + lax.broadcasted_iota(jnp.int32, (tq, 1), 0)] * ns, axis=0)
    if kind == "diff":
        slope = jnp.concatenate(
            [jnp.full((tq, 1), slopes_ref[grp * (ns // 2) + i // 2], F32) for i in range(ns)], axis=0)

    if kind == "sb":
        init = (jnp.zeros((rows, 1), F32), jnp.zeros((rows, vw), F32))
    else:
        init = (jnp.full((rows, 1), NEG_BIG, F32), jnp.zeros((rows, 1), F32),
                jnp.zeros((rows, vw), F32))

    def tile_step(st, k_t, v_t, k_start, src, masked, upper):
        tk = src["tk"]
        kpos = k_start + lax.broadcasted_iota(jnp.int32, (1, tk), 1)
        s = _dot(qs, k_t) if src["kt"] else _dot_nt(qs, k_t)
        pv = (lambda p: _dot_nt(p, v_t)) if src["vt"] else (lambda p: _dot(p, v_t))
        if masked:
            valid = (kpos < qpos) if kind == "sb" else ((kpos >> CHUNK_SHIFT) <= (qpos >> CHUNK_SHIFT))
        if kind == "sb":
            c, acc = st
            lp = jnp.log(1.0 + jnp.exp2(-jnp.abs(s))) * LOG2E
            ls = jnp.minimum(s, 0.0) - lp
            l1 = ls - s
            if masked:
                l1 = jnp.where(valid, l1, 0.0)
            hi = l1.astype(BF16)
            lo = (l1 - hi.astype(F32)).astype(BF16)
            cb = upper.shape[0]
            nb = tk // cb
            if nb > 1:
                hi = jnp.concatenate([hi[:, cb * j:cb * (j + 1)] for j in range(nb)], axis=0)
                lo = jnp.concatenate([lo[:, cb * j:cb * (j + 1)] for j in range(nb)], axis=0)
            suffix = _dot(hi, upper) + _dot(lo, upper)
            parts = [None] * nb
            for j in reversed(range(nb)):
                parts[j] = jnp.exp2(ls[:, cb * j:cb * (j + 1)] + suffix[rows * j:rows * (j + 1)] + c)
                c = c + jnp.sum(l1[:, cb * j:cb * (j + 1)], axis=-1, keepdims=True)
            a = parts[0] if nb == 1 else jnp.concatenate(parts, axis=1)
            if masked:
                a = jnp.where(valid, a, 0.0)
            return (c, acc + pv(a.astype(BF16)))
        m, l, acc = st
        if kind == "diff":
            dist = qpos - kpos
            s = s - slope * (jnp.abs(dist) if masked else dist).astype(F32)
        if masked:
            s = jnp.where(valid, s, -jnp.inf)
        m_new = jnp.maximum(m, jnp.max(s, axis=-1, keepdims=True))
        alpha = jnp.exp2(m - m_new)
        p = jnp.exp2(s - m_new)
        l = alpha * l + jnp.sum(p, axis=-1, keepdims=True)
        return (m_new, l, alpha * acc + pv(p.astype(BF16)))

    def upper_tri(tk):
        if kind != "sb":
            return None
        cb = min(tk, SB_BLOCK)
        r = lax.broadcasted_iota(jnp.int32, (cb, cb), 0)
        c = lax.broadcasted_iota(jnp.int32, (cb, cb), 1)
        return jnp.where(r > c, 1.0, 0.0).astype(BF16)

    st = init
    for s, sc, in_refs in reversed(list(zip(srcs, scr, kv_refs))):
        k_ref, v_ref = (None, None) if s["latent"] else in_refs
        tk = s["tk"]
        upper = upper_tri(tk)

        def load(k_off, src, _sc=sc, _k=k_ref, _v=v_ref):
            def one(ref4, ref2, transposed):
                win = pl.ds(k_off, src["tk"])
                if ref2 is not None:
                    return ref2[:, win] if transposed else ref2[win, :]
                return ref4[0, 0, :, win] if transposed else ref4[0, 0, win, :]
            return (one(_k, _sc and _sc[0], src["kt"]), one(_v, _sc and _sc[1], src["vt"]))

        if s["mode"] == "self":
            k_t, v_t = load(pl.multiple_of(qi * tk, tk), s)
            st = tile_step(st, k_t, v_t, s["pos0"] + qi * tk, s, True, upper)
            if kind == "sb" and tk > SB_BLOCK:
                s = dict(s, tk=SB_BLOCK)
            n_full = qi * (tk // s["tk"])
        elif s["mode"] == "masked":
            k_t, v_t = load(0, s)
            st = tile_step(st, k_t, v_t, s["pos0"], s, True, upper)
            continue
        else:
            n_full = s["sk"] // tk

        def body(i, st_, _s=s, _load=load, _n=n_full, _upper=upper):
            kj = _n - 1 - i
            k_t, v_t = _load(pl.multiple_of(kj * _s["tk"], _s["tk"]), _s)
            return tile_step(st_, k_t, v_t, _s["pos0"] + kj * _s["tk"], _s, False, _upper)

        if isinstance(n_full, int) and n_full <= 2:
            for i in range(n_full):
                st = body(i, st)
        elif kind == "sb":
            def live(st_):
                return (jnp.max(st_[0]) > SB_SKIP_LOG2).astype(jnp.int32)

            def wbody(carry, _body=body):
                i, _, st_ = carry
                st_ = _body(i, st_)
                return i + 1, live(st_), st_

            _, _, st = lax.while_loop(lambda carry, _n=n_full: (carry[0] < _n) & (carry[1] > 0),
                                      wbody, (jnp.int32(0), live(st), st))
        else:
            st = lax.fori_loop(0, n_full, body, st)

    acc = st[-1]
    if kind == "diff":
        l = st[1]
        lv = lam_ref[...]
        lam = (jnp.exp(jnp.sum(lv[0:1] * lv[1:2], axis=-1, keepdims=True))
               - jnp.exp(jnp.sum(lv[2:3] * lv[3:4], axis=-1, keepdims=True)) + lam_init)
        for hh in range(ns // 2):
            lanes = slice(2 * HEAD * hh, 2 * HEAD * (hh + 1))
            r0, r1 = slice(tq * 2 * hh, tq * (2 * hh + 1)), slice(tq * (2 * hh + 1), tq * (2 * hh + 2))
            o = acc[r0, lanes] / l[r0] - lam * (acc[r1, lanes] / l[r1])
            o_ref[0, :, lanes] = (_rmsnorm(o, gsub_ref[...]) * (1.0 - lam_init)).astype(BF16)
    else:
        lane = lax.broadcasted_iota(jnp.int32, (tq, vw), 1)
        o = jnp.zeros((tq, vw), F32)
        for i in range(ns):
            blk = acc[tq * i:tq * (i + 1)]
            if kind == "mla":
                blk = blk / st[1][tq * i:tq * (i + 1)]
            o = jnp.where((lane >= HEAD * i) & (lane < HEAD * (i + 1)), blk, o)
        o_ref[0] = o.astype(BF16)


def _attention(kind, q3, sources, *, ns, q_pos0, extras=(), lam_init=0.0):
    b, sq, qtot = q3.shape
    w = (2 * HEAD if kind == "mla" else HEAD) * ns
    vw = HEAD * ns
    n_groups = qtot // w
    tq = min(ATTN_TILE, sq)
    assert sq % tq == 0 and qtot % w == 0
    srcs, in_specs, args, scratch = [], [], [], []
    if kind == "diff":
        in_specs.append(pl.BlockSpec(memory_space=pltpu.SMEM))
        args.append(extras[0])
    in_specs.append(pl.BlockSpec((1, tq, w), lambda bi, g, qi: (bi, qi, g)))
    args.append(q3)

    def kv_spec(arr, width, transposed, layer):
        if transposed:
            assert arr.shape[2] == n_groups * width
            return pl.BlockSpec((1, 1, width, arr.shape[3]), lambda bi, g, qi: (layer, bi, g, 0))
        assert arr.shape[3] == n_groups * width
        return pl.BlockSpec((1, 1, arr.shape[2], width), lambda bi, g, qi: (layer, bi, 0, g))

    for s in sources:
        lat = s.get("latent")
        if lat is not None:
            assert kind == "mla" and s["mode"] in ("full", "self")
            ckv, krt, layer = lat["ckv"], lat["krt"], s["layer"]
            sk = ckv.shape[2]
            tk = tq if s["mode"] == "self" else s["tk"]
            assert sk % tk == 0
            srcs.append(dict(mode=s["mode"], pos0=s["pos0"], sk=sk, tk=tk, cast=True, kt=True, vt=True,
                             vil=0, latent=True))
            in_specs += [pl.BlockSpec((1, 1, sk, ckv.shape[3]), lambda bi, g, qi, _l=layer: (_l, bi, 0, 0)),
                         pl.BlockSpec((1, 1, krt.shape[2], sk), lambda bi, g, qi, _l=layer: (_l, bi, 0, 0)),
                         pl.BlockSpec((w // 2, ckv.shape[3]), lambda bi, g, qi: (g, 0)),
                         pl.BlockSpec((vw, ckv.shape[3]), lambda bi, g, qi: (g, 0)),
                         pl.BlockSpec((w // 2, 1), lambda bi, g, qi: (g, 0))]
            args += [ckv, krt, lat["wukt"], lat["wuvt"], lat["gkn"]]
            scratch += [pltpu.VMEM((w, sk), BF16), pltpu.VMEM((vw, sk), BF16)]
            continue
        k, v, kt, vt = s["k"], s["v"], s["kt"], s["vt"]
        sk = k.shape[3] if kt else k.shape[2]
        tk = tq if s["mode"] == "self" else (sk if s["mode"] == "masked" else s.get("tk", min(ATTN_TILE, sk)))
        assert sk % tk == 0 and (tk <= SB_BLOCK or tk % SB_BLOCK == 0)
        cast = k.dtype != BF16
        assert cast == (v.dtype != BF16)
        vil = s.get("vil", 0)
        vlayer = s.get("vlayer", s["layer"])
        srcs.append(dict(mode=s["mode"], pos0=s["pos0"], sk=sk, tk=tk, cast=cast, kt=kt, vt=vt, vil=vil,
                         latent=False))
        if vil:
            assert cast and not vt and v.shape[2:] == (sk * vil, LANES) and vw % LANES == 0
            v_spec = pl.BlockSpec((1, 1, sk * vil, LANES), lambda bi, g, qi, _l=vlayer: (_l, bi, 0, 0))
        else:
            v_spec = kv_spec(v, vw, vt, vlayer)
        in_specs += [kv_spec(k, w, kt, s["layer"]), v_spec]
        args += [k, v]
        if cast:
            scratch += [pltpu.VMEM((w, sk) if kt else (sk, w), BF16),
                        pltpu.VMEM((vw, sk) if vt else (sk, vw), BF16)]
    if kind == "diff":
        in_specs += [_const_spec(extras[1].shape), _const_spec(extras[2].shape)]
        args += [extras[1], extras[2]]
    kern = functools.partial(_attn_kernel, kind=kind, srcs=srcs, ns=ns, tq=tq, q_pos0=q_pos0,
                             lam_init=lam_init)
    return pl.pallas_call(
        kern,
        grid=(b, n_groups, sq // tq),
        in_specs=in_specs,
        out_specs=pl.BlockSpec((1, tq, vw), lambda bi, g, qi: (bi, qi, g)),
        out_shape=jax.ShapeDtypeStruct((b, sq, n_groups * vw), BF16),
        scratch_shapes=scratch,
        compiler_params=_cparams(3),
        name="attn_" + kind,
    )(*args)


def _merge_kernel(x_ref, gmix_ref, wg_ref, om_ref, os_ref, od_ref, wbr_ref, wout_ref, o_ref):
    x = x_ref[...]
    d = x.shape[1]
    h = _rmsnorm(x, gmix_ref[...]).astype(BF16)
    merged = None
    for i, br_ref in enumerate((om_ref, os_ref, od_ref)):
        gate = _sigmoid(_dot(h, wg_ref[:, i * d:(i + 1) * d]))
        term = gate * _dot(br_ref[...], wbr_ref[i])
        merged = term if merged is None else merged + term
    o_ref[...] = x + _dot(merged.astype(BF16), wout_ref[...])


def _merge(x2d, o_mla, o_sb, o_diff, lw):
    t, d = x2d.shape
    tm = ROW_TILE
    row = lambda w: pl.BlockSpec((tm, w), lambda i: (i, 0))
    return pl.pallas_call(
        _merge_kernel,
        grid=(t // tm,),
        in_specs=[row(d), _const_spec(lw["gmix"].shape), _const_spec(lw["wg"].shape),
                  row(o_mla.shape[1]), row(o_sb.shape[1]), row(o_diff.shape[1]),
                  _const_spec(lw["wbr"].shape), _const_spec(lw["wout"].shape)],
        out_specs=row(d),
        out_shape=jax.ShapeDtypeStruct((t, d), F32),
        compiler_params=_cparams(1),
        name="merge",
    )(x2d, lw["gmix"], lw["wg"], o_mla, o_sb, o_diff, lw["wbr"], lw["wout"])


def _conv_rows(a, h0, h1, cw, cb):
    n = a.shape[0]
    row = lax.broadcasted_iota(jnp.int32, (n, 1), 0)
    p1 = jnp.where(row == 0, h1, pltpu.roll(a, 1, 0))
    p2 = jnp.where(row == 0, h0, jnp.where(row == 1, h1, pltpu.roll(a, 2, 0)))
    return p2 * cw[0:1] + p1 * cw[1:2] + a * cw[2:3] + cb


def _ffn_kernel(*refs, seg, d_ff, has_hist):
    it = iter(refs)
    x_ref, g_ref, wup_ref, cw_ref, cb_ref = next(it), next(it), next(it), next(it), next(it)
    hist_ref = next(it) if has_hist else None
    wdn_ref, o_ref, conv_ref = next(it), next(it), next(it)
    carry_ref = next(it) if not has_hist else None
    act_ref = next(it) if has_hist else None

    x = x_ref[...]
    tm = x.shape[0]
    h = _rmsnorm(x, g_ref[...]).astype(BF16)
    cw, cb = cw_ref[...], cb_ref[...]

    if not has_hist:
        i = pl.program_id(0)
        tiles_per_seq = seg // tm

        @pl.when((i % tiles_per_seq) == 0)
        def _():
            carry_ref[...] = jnp.zeros_like(carry_ref)

        tail = carry_ref[...]
        y, tails = x, []
        chunks = [(c0, min(FF_CHUNK, d_ff - c0)) for c0 in range(0, d_ff, FF_CHUNK)]
        up = lambda c0, cs: (_dot(h, wup_ref[0, :, c0:c0 + cs]),
                             _dot(h, wup_ref[0, :, d_ff + c0:d_ff + c0 + cs]))
        nxt = up(*chunks[0])
        for idx, (c0, cs) in enumerate(chunks):
            a, u = nxt
            if idx + 1 < len(chunks):
                nxt = up(*chunks[idx + 1])
            c = _conv_rows(a, tail[6:7, c0:c0 + cs], tail[7:8, c0:c0 + cs], cw[:, c0:c0 + cs],
                           cb[:, c0:c0 + cs])
            tails.append(a[tm - 8:tm])
            y = y + _dot((c * _sigmoid(c) * u).astype(BF16), wdn_ref[0, c0:c0 + cs, :])
        new_tail = jnp.concatenate(tails, axis=1)
        carry_ref[...] = new_tail

        @pl.when((i % tiles_per_seq) == tiles_per_seq - 1)
        def _():
            conv_ref[0] = new_tail[6:8]

        o_ref[...] = y
    else:
        a = _dot(h, wup_ref[0, :, 0:d_ff])
        u = _dot(h, wup_ref[0, :, d_ff:2 * d_ff])
        for s in range(tm // seg):
            a_s = a[s * seg:(s + 1) * seg]
            hs = hist_ref[0, s]
            c = _conv_rows(a_s, hs[0:1], hs[1:2], cw, cb)
            act_ref[s * seg:(s + 1) * seg, :] = (c * _sigmoid(c) * u[s * seg:(s + 1) * seg]).astype(BF16)
            conv_ref[s] = a_s[seg - 2:seg]
        o_ref[...] = x + _dot(act_ref[...], wdn_ref[0])


def _ffn(x2d, lw, *, seg, hist=None, layer=0):
    t, d = x2d.shape
    tm = ROW_TILE
    d_ff = lw["wdn"].shape[1]
    has_hist = hist is not None
    row = lambda w: pl.BlockSpec((tm, w), lambda i: (i, 0))
    stacked = lambda a: pl.BlockSpec((1,) + a.shape[1:], lambda i: (layer, 0, 0),
                                     pipeline_mode=pl.Buffered(1))
    in_specs = [row(d), _const_spec(lw["gffn"].shape), stacked(lw["wup"]),
                _const_spec(lw["cw"].shape), _const_spec(lw["cb"].shape)]
    args = [x2d, lw["gffn"], lw["wup"], lw["cw"], lw["cb"]]
    if has_hist:
        assert tm % seg == 0 and seg % 8 == 0
        nseq = tm // seg
        in_specs.append(pl.BlockSpec((1, nseq, 2, d_ff), lambda i: (layer, i, 0, 0)))
        args.append(hist)
        conv_spec = pl.BlockSpec((nseq, 2, d_ff), lambda i: (i, 0, 0))
        scratch = [pltpu.VMEM((tm, d_ff), BF16)]
    else:
        assert seg % tm == 0
        tps = seg // tm
        conv_spec = pl.BlockSpec((1, 2, d_ff), lambda i: (i // tps, 0, 0))
        scratch = [pltpu.VMEM((8, d_ff), F32)]
    in_specs.append(stacked(lw["wdn"]))
    args.append(lw["wdn"])
    return pl.pallas_call(
        functools.partial(_ffn_kernel, seg=seg, d_ff=d_ff, has_hist=has_hist),
        grid=(t // tm,),
        in_specs=in_specs,
        out_specs=[row(d), conv_spec],
        out_shape=[jax.ShapeDtypeStruct((t, d), F32),
                   jax.ShapeDtypeStruct((t // seg, 2, d_ff), F32)],
        scratch_shapes=scratch,
        compiler_params=_cparams(1),
        name="ffn",
    )(*args)


def _group_matrix(groups, width):
    m = np.zeros((width, width), np.float32)
    for g in groups:
        for i in g:
            m[i, g] = 1.0 / len(g)
    return jnp.asarray(m, dtype=BF16)


def _pack_layer(w, l, dims):
    nope, rope, heads = dims["nope"], dims["rope"], dims["mla_heads"]
    q_rank, kv_rank = dims["q_rank"], dims["kv_rank"]
    sb_w, df_w = dims["sb_w"], dims["df_w"]
    half = rope // 2
    assert nope == HEAD and rope == 32 and heads % 2 == 0 and q_rank + kv_rank + rope <= 4 * LANES
    assert (q_rank + kv_rank) % LANES == 0 and sb_w % (2 * LANES) == 0 and df_w % (2 * LANES) == 0
    w_in = w["w_in"][l]
    o_kr = q_rank + kv_rank
    o_sb = o_kr + rope
    o_df = o_sb + 3 * sb_w
    o_g = o_df + 3 * df_w
    col = lambda a, b: w_in[:, a:b]
    wa = jnp.pad(col(0, o_sb), ((0, 0), (0, 4 * LANES - o_sb)))
    w1 = jnp.concatenate([wa, col(o_sb, o_g)], axis=1).astype(BF16)
    wn = jnp.concatenate([col(0, o_kr), col(o_sb, o_sb + sb_w), col(o_df, o_df + df_w),
                          col(o_df + 2 * df_w, o_g)], axis=1).astype(BF16)
    wt = jnp.concatenate([col(o_sb + sb_w, o_df), col(o_df + df_w, o_df + 2 * df_w),
                          col(o_kr, o_sb)], axis=1).T.astype(BF16)

    cols = np.full((heads // 2) * 2 * LANES, -1, np.int64)
    gain_src = np.full(cols.shape, -1, np.int64)
    for p in range(heads // 2):
        base = 2 * LANES * p
        for j, hd in enumerate((2 * p, 2 * p + 1)):
            hb = hd * (nope + rope)
            cols[base + nope * j:base + nope * (j + 1)] = hb + np.arange(nope)
            gain_src[base + nope * j:base + nope * (j + 1)] = np.arange(nope)
            x1 = base + 2 * nope + half * j
            x2 = base + 2 * nope + 2 * half + half * j
            cols[x1:x1 + half] = hb + nope + np.arange(half)
            cols[x2:x2 + half] = hb + nope + half + np.arange(half)
            gain_src[x1:x1 + half] = nope + np.arange(half)
            gain_src[x2:x2 + half] = nope + half + np.arange(half)
    valid = jnp.asarray(cols >= 0)
    wuq = jnp.where(valid[None, :], w["mla_w_uq"][l][:, np.maximum(cols, 0)], 0.0).astype(BF16)
    qg = jnp.concatenate([w["mla_qn_g"][l], w["mla_qr_g"][l]])
    gq = jnp.where(valid, qg[np.maximum(gain_src, 0)], 0.0) * ((nope + rope) ** -0.5 * LOG2E)

    groups = [list(range(0, nope)), list(range(nope, 2 * nope))]
    for j in range(2):
        groups.append(list(range(2 * nope + half * j, 2 * nope + half * (j + 1)))
                      + list(range(2 * nope + 2 * half + half * j, 2 * nope + 2 * half + half * (j + 1))))
    gm_mla = _group_matrix(groups, 2 * LANES)
    gm64 = _group_matrix([list(range(HEAD * j, HEAD * (j + 1))) for j in range(4)], 2 * LANES)

    rep = np.zeros((rope, LANES), np.float32)
    for j in range(half):
        rep[j, [j, half + j]] = 1.0
        rep[half + j, [2 * half + j, 3 * half + j]] = 1.0

    row = lambda v: v.reshape(1, -1).astype(F32)
    colv = lambda v: v.reshape(-1, 1).astype(F32)
    gdk = jnp.tile(w["diff_kn_g"][l], df_w // HEAD)
    gkn = jnp.tile(w["mla_kn_g"][l], heads)
    return dict(
        gmix=row(w["mix_norm_g"][l]), w1=w1, wn=wn, wt=wt, gcq=row(w["mla_q_norm_g"][l]), wuq=wuq,
        gm_mla=gm_mla, gq=row(gq), gckv=row(w["mla_kv_norm_g"][l]),
        gkr=row(jnp.pad(w["mla_kr_g"][l], (0, LANES - rope))), gkr_col=colv(w["mla_kr_g"][l]),
        gm64=gm64,
        gdq=row(jnp.tile(w["diff_qn_g"][l], df_w // HEAD) * (LOG2E / 8.0)),
        gdk=row(gdk), gdk_col=colv(gdk),
        wukv=jnp.concatenate([w["mla_w_uk"][l], w["mla_w_uv"][l]], axis=1).astype(BF16),
        wukt=w["mla_w_uk"][l].T.astype(BF16), wuvt=w["mla_w_uv"][l].T.astype(BF16),
        gkn=row(gkn), gkn_col=colv(gkn),
        rep=jnp.asarray(rep, dtype=BF16),
        lam=w["diff_lambda"][l].astype(F32), gsub=row(w["diff_subln_g"][l]),
        wg=w_in[:, o_g:].astype(BF16),
        wbr=jnp.stack([w["w_br_mla"][l], w["w_br_sb"][l], w["w_br_diff"][l]]).astype(BF16),
        wout=w["w_out"][l].astype(BF16),
        gffn=row(w["ffn_norm_g"][l]), wup=w["ffn_w_up"].astype(BF16),
        cw=w["ffn_conv_w"][l].astype(F32), cb=row(w["ffn_conv_b"][l]),
        wdn=w["ffn_w_down"].astype(BF16),
    )


def _rope_tables(pos, rope, n_rows):
    half = rope // 2
    inv = ROPE_BASE ** (-jnp.arange(half, dtype=F32) / half)
    ang = pos.astype(F32)[:, None] * inv[None, :]
    cos, sin = jnp.cos(ang), jnp.sin(ang)
    z = lambda n: jnp.zeros((pos.shape[0], n), F32)
    tab = jnp.concatenate([
        cos, cos, cos, cos, z(LANES - 4 * half),
        -sin, -sin, sin, sin, z(LANES - 4 * half),
        cos, cos, z(LANES - 2 * half),
        -sin, sin, z(LANES - 2 * half)], axis=1)
    return jnp.tile(tab, (n_rows // pos.shape[0], 1)), jnp.concatenate([cos.T, sin.T], axis=0)


def _tail(x2d, o_mla, o_sb, o_df, lw, *, seq, hist=None, layer=0):
    f2 = lambda a: a.reshape(x2d.shape[0], a.shape[-1])
    x1 = _merge(x2d, f2(o_mla), f2(o_sb), f2(o_df), lw)
    return _ffn(x1, lw, seg=seq, hist=hist, layer=layer)


def _layer_prompt(x2d, lw, dims, *, batch, seq, tabs, layer, depth, bufs, slopes):
    q, sq, dq, *bufs = _in_proj_t(x2d, lw, tabs[0], tabs[1], dims, batch=batch, seq=seq,
                                  layer=layer, depth=depth, prev=bufs)
    ckv, krt, skt, svt, dkt, dvi = bufs
    r3 = lambda a: a.reshape(batch, seq, a.shape[-1])
    src = lambda k, v, lyr, **kw: [dict(k=k, v=v, layer=lyr, kt=True, pos0=0, mode="self", **kw)]
    lam_init = 0.8 - 0.6 * math.exp(-0.3 * layer)
    latent = dict(ckv=ckv, krt=krt, wukt=lw["wukt"], wuvt=lw["wuvt"], gkn=lw["gkn_col"])
    o_mla = _attention("mla", r3(q), src(None, None, layer, latent=latent), ns=2, q_pos0=0)
    o_sb = _attention("sb", r3(sq), src(skt, svt, layer, vt=True), ns=2, q_pos0=0)
    o_df = _attention("diff", r3(dq), src(dkt, dvi, layer, vt=False, vil=dims["df_w"] // LANES),
                      ns=2, q_pos0=0, extras=(slopes, lw["lam"], lw["gsub"]), lam_init=lam_init)
    x2, conv = _tail(x2d, o_mla, o_sb, o_df, lw, seq=seq, layer=layer)
    return x2, conv, tuple(bufs)


def _layer_sample(x2d, lw, dims, *, batch, seq, tab, layer, caches, conv_state, slopes):
    q, ckv, kr, sq, sk, sv, dq, dk, dv = _in_proj(x2d, lw, tab, dims)
    r3 = lambda a: a.reshape(batch, seq, a.shape[-1])
    r4 = lambda a: a.reshape(1, batch, seq, a.shape[-1])
    kpk, mv = _kv_up(ckv, kr, lw)
    c_ckv, c_krt, c_skt, c_svt, c_dkt, c_dv = caches
    plen = c_ckv.shape[2]

    def srcs(ck, cv, lyr, nk, nv, tk=plen, **kw):
        return [dict(k=ck, v=cv, layer=lyr, kt=True, pos0=0, mode="full", tk=tk, **kw),
                dict(k=r4(nk), v=r4(nv), layer=0, kt=False, vt=False, pos0=plen, mode="masked")]

    lam_init = 0.8 - 0.6 * math.exp(-0.3 * layer)
    latent = dict(ckv=c_ckv, krt=c_krt, wukt=lw["wukt"], wuvt=lw["wuvt"], gkn=lw["gkn_col"])
    o_mla = _attention("mla", r3(q), srcs(None, None, layer, kpk, mv, latent=latent), ns=4, q_pos0=plen)
    o_sb = _attention("sb", r3(sq), srcs(c_skt, c_svt, layer, sk, sv, tk=min(SB_CACHE_TILE, plen), vt=True),
                      ns=4, q_pos0=plen)
    o_df = _attention("diff", r3(dq),
                      srcs(c_dkt, c_dv, layer, dk, dv, vt=False, vil=dims["df_w"] // LANES),
                      ns=4, q_pos0=plen, extras=(slopes, lw["lam"], lw["gsub"]), lam_init=lam_init)
    x2, conv = _tail(x2d, o_mla, o_sb, o_df, lw, seq=seq, hist=conv_state, layer=layer)
    return x2, (ckv, kr, sk, sv, dk, dv), conv


def kernel(x_prompt, x_sample, cache_mla_ckv, cache_mla_krope, cache_sb_k, cache_sb_v, cache_diff_k, cache_diff_v, state_ffn_conv, mix_norm_g, w_in, mla_q_norm_g, mla_w_uq, mla_kv_norm_g, mla_w_uk, mla_w_uv, mla_qn_g, mla_kn_g, mla_qr_g, mla_kr_g, diff_qn_g, diff_kn_g, diff_lambda, diff_subln_g, w_br_mla, w_br_sb, w_br_diff, w_out, ffn_norm_g, ffn_w_up, ffn_conv_w, ffn_conv_b, ffn_w_down):
    w = dict(mix_norm_g=mix_norm_g, w_in=w_in, mla_q_norm_g=mla_q_norm_g, mla_w_uq=mla_w_uq,
             mla_kv_norm_g=mla_kv_norm_g, mla_w_uk=mla_w_uk, mla_w_uv=mla_w_uv, mla_qn_g=mla_qn_g,
             mla_kn_g=mla_kn_g, mla_qr_g=mla_qr_g, mla_kr_g=mla_kr_g, diff_qn_g=diff_qn_g,
             diff_kn_g=diff_kn_g, diff_lambda=diff_lambda, diff_subln_g=diff_subln_g,
             w_br_mla=w_br_mla, w_br_sb=w_br_sb, w_br_diff=w_br_diff, w_out=w_out,
             ffn_norm_g=ffn_norm_g, ffn_w_up=ffn_w_up, ffn_conv_w=ffn_conv_w,
             ffn_conv_b=ffn_conv_b, ffn_w_down=ffn_w_down)
    depth = w_in.shape[0]
    bp, sp, d = x_prompt.shape
    bs, ss, _ = x_sample.shape
    past_len = cache_mla_ckv.shape[2]
    sb_heads, sb_dim = cache_sb_k.shape[3], cache_sb_k.shape[4]
    df_heads, df_dim = cache_diff_k.shape[3], cache_diff_k.shape[5]
    assert sb_dim == HEAD and df_dim == HEAD and ffn_conv_w.shape[1] == 3
    nope, rope = mla_qn_g.shape[1], mla_qr_g.shape[1]
    dims = dict(d_model=d, q_rank=mla_q_norm_g.shape[1], kv_rank=mla_kv_norm_g.shape[1],
                nope=nope, rope=rope, mla_heads=mla_w_uk.shape[2] // nope,
                sb_w=sb_heads * sb_dim, df_w=2 * df_heads * df_dim)
    sb_w, df_w = dims["sb_w"], dims["df_w"]

    tabs_p = _rope_tables(jnp.arange(sp, dtype=jnp.int32), rope, max(sp, ROW_TILE))
    tab_s, _ = _rope_tables(past_len + jnp.arange(ss, dtype=jnp.int32), rope, max(ss, ROW_TILE))
    slopes = 2.0 ** (-8.0 * jnp.arange(1, df_heads + 1, dtype=F32) / df_heads) * LOG2E

    feat = lambda c: jnp.moveaxis(c, 2, -1).reshape(depth, bs, -1, past_len)
    assert 2 * df_dim == LANES
    caches = (cache_mla_ckv, feat(cache_mla_krope), feat(cache_sb_k), feat(cache_sb_v),
              feat(cache_diff_k), cache_diff_v.reshape(depth, bs, past_len * df_heads, 2 * df_dim))

    xp = x_prompt.reshape(bp * sp, d)
    xs = x_sample.reshape(bs * ss, d)
    bufs, rows_s, conv_p, conv_s = None, [], [], []
    for l in range(depth):
        lw = _pack_layer(w, l, dims)
        xp, cp, bufs = _layer_prompt(xp, lw, dims, batch=bp, seq=sp, tabs=tabs_p, layer=l,
                                     depth=depth, bufs=bufs, slopes=slopes)
        xs, rs, cs = _layer_sample(xs, lw, dims, batch=bs, seq=ss, tab=tab_s, layer=l,
                                   caches=caches, conv_state=state_ffn_conv, slopes=slopes)
        rows_s.append(rs)
        conv_p.append(cp)
        conv_s.append(cs)

    p_ckv, p_krt, p_skt, p_svt, p_dkt, p_dvi = bufs
    tok = lambda t, tail: jnp.moveaxis(t.reshape((depth, bp) + tail + (sp,)), -1, 2)
    prompt_rows = (p_ckv, tok(p_krt, (rope,)), tok(p_skt, (sb_heads, sb_dim)),
                   tok(p_svt, (sb_heads, sb_dim)), tok(p_dkt, (df_heads, 2, df_dim)),
                   p_dvi.reshape(depth, bp, sp, df_heads, 2 * df_dim))

    def stack(i, tail):
        return jnp.stack([r[i] for r in rows_s]).reshape((depth, bs, ss) + tail)

    sample_rows = (stack(0, (dims["kv_rank"],)), stack(1, (rope,)), stack(2, (sb_heads, sb_dim)),
                   stack(3, (sb_heads, sb_dim)), stack(4, (df_heads, 2, df_dim)),
                   stack(5, (df_heads, 2 * df_dim)))
    return ((xp.reshape(bp, sp, d), xs.reshape(bs, ss, d)) + prompt_rows + (jnp.stack(conv_p),)
            + sample_rows + (jnp.stack(conv_s),))
```
